```python
import math
import numpy as np
import jax
import jax.numpy as jnp
from jax import lax

D_MODEL = 1024
BATCH = 8
SEQ = 4096
DEPTH = 1

GDN_HEADS = 8
GDN_DK = 128
GDN_DV = 128
GDN_CONV = 4
GDN_CHUNK = 64
NSA_HEADS = 8
NSA_GROUPS = 2
NSA_DH = 128
CMP_BLOCK = 32
CMP_STRIDE = 16
CMP_HIDDEN = 256
SEL_BLOCK = 64
SEL_TOPN = 16
WINDOW = 512
NSA_Q_BLOCK = 32
FORCE_BONUS = 1000.0
REL_BUCKETS = 32
REL_MAX_DIST = 1024
FFN_HIDDEN = ((8 * D_MODEL + 3 * 256 - 1) // (3 * 256)) * 256
NORM_EPS = 1e-6

GDN_QK_W = GDN_HEADS * GDN_DK
GDN_V_W = GDN_HEADS * GDN_DV
NSA_W = NSA_HEADS * NSA_DH
NSA_KV_W = NSA_GROUPS * NSA_DH
IN_SPLITS = (GDN_QK_W, GDN_QK_W, GDN_V_W, GDN_V_W, GDN_HEADS, GDN_HEADS,
             NSA_W, 6 * NSA_KV_W, 3 * NSA_HEADS, 2 * D_MODEL)
IN_WIDTH = sum(IN_SPLITS)

kernel_name = "hybrid_gdn_nsa_gated_merge_swiglu"


def rmsnorm(x, w):
    xf = x.astype(jnp.float32)
    y = xf * lax.rsqrt(jnp.mean(xf * xf, axis=-1, keepdims=True) + NORM_EPS)
    return y.astype(x.dtype) * w


def l2norm(x):
    xf = x.astype(jnp.float32)
    return (xf * lax.rsqrt(jnp.sum(xf * xf, axis=-1, keepdims=True) + NORM_EPS)).astype(x.dtype)


def masked_softmax(logits, mask):
    s = jnp.where(mask, logits, -1e30)
    m = jnp.max(s, axis=-1, keepdims=True)
    p = jnp.where(mask, jnp.exp(s - m), 0.0)
    return p / jnp.maximum(jnp.sum(p, axis=-1, keepdims=True), 1e-30)


def rel_bucket(dist):
    dist = jnp.maximum(dist, 0)
    max_exact = REL_BUCKETS // 2
    d_f = jnp.maximum(dist, 1).astype(jnp.float32)
    large = max_exact + (jnp.log(d_f / max_exact) / math.log(REL_MAX_DIST / max_exact)
                         * (REL_BUCKETS - max_exact)).astype(jnp.int32)
    large = jnp.minimum(large, REL_BUCKETS - 1)
    return jnp.where(dist < max_exact, dist, large)


def causal_conv(x, w):
    c = x.shape[-1]
    return lax.conv_general_dilated(x, w[:, None, :], window_strides=(1,), padding=[(GDN_CONV - 1, 0)],
                                    dimension_numbers=("NWC", "WIO", "NWC"), feature_group_count=c)


def gdn_chunked(q, k, v, g, beta):
    b_, s_, h, dk = q.shape
    dv = v.shape[-1]
    c = GDN_CHUNK
    nc = s_ // c
    f32 = jnp.float32

    def chunks(t):
        return t.astype(f32).reshape(b_, nc, c, h, -1).transpose(0, 3, 1, 2, 4)

    qc, kc, vc = chunks(q), chunks(k), chunks(v)
    gc = g.astype(f32).reshape(b_, nc, c, h).transpose(0, 3, 1, 2)
    bc = beta.astype(f32).reshape(b_, nc, c, h).transpose(0, 3, 1, 2)
    gcum = jnp.cumsum(gc, axis=-1)
    idx = jnp.arange(c)
    tril = idx[:, None] >= idx[None, :]
    strict = idx[:, None] > idx[None, :]
    diff = gcum[..., :, None] - gcum[..., None, :]
    decay = jnp.where(tril, jnp.exp(jnp.where(tril, diff, 0.0)), 0.0)
    kb = kc * bc[..., None]
    vb = vc * bc[..., None]
    a_mat = jnp.where(strict, jnp.einsum("bhnid,bhnjd->bhnij", kb, kc) * decay, 0.0)
    rhs = jnp.concatenate([vb, kb * jnp.exp(gcum)[..., None]], axis=-1)
    sol = lax.linalg.triangular_solve(a_mat + jnp.eye(c, dtype=f32), rhs, left_side=True,
                                      lower=True, unit_diagonal=True)
    value, kcd = sol[..., :dv], sol[..., dv:]
    intra = jnp.einsum("bhnid,bhnjd->bhnij", qc, kc) * decay
    qg = qc * jnp.exp(gcum)[..., None]
    kdec = kc * jnp.exp(gcum[..., -1:] - gcum)[..., None]
    chunk_decay = jnp.exp(gcum[..., -1])
    xs = tuple(jnp.moveaxis(t, 2, 0) for t in (value, kcd, intra, qg, kdec, chunk_decay))

    def step(state, inp):
        val, kcd_n, intra_n, qg_n, kdec_n, cd_n = inp
        v_new = val - jnp.einsum("bhcd,bhde->bhce", kcd_n, state)
        o = jnp.einsum("bhcd,bhde->bhce", qg_n, state) + jnp.einsum("bhij,bhje->bhie", intra_n, v_new)
        state = state * cd_n[..., None, None] + jnp.einsum("bhcd,bhce->bhde", kdec_n, v_new)
        return state, o

    s0 = jnp.zeros((b_, h, dk, dv), f32)
    _, o = lax.scan(step, s0, xs)
    return o.transpose(1, 0, 3, 2, 4).reshape(b_, s_, h, dv).astype(v.dtype)


def compress(raw, pe, w1, w2):
    b_, s_, g, dh = raw.shape
    ratio = CMP_BLOCK // CMP_STRIDE
    nchunk = s_ // CMP_STRIDE
    ncmp = nchunk - ratio + 1
    ch = raw.reshape(b_, nchunk, CMP_STRIDE, g, dh)
    blocks = jnp.concatenate([ch[:, r:r + ncmp] for r in range(ratio)], axis=2)
    blocks = blocks + pe[None, None, :, None, :]
    flat = blocks.transpose(0, 1, 3, 2, 4).reshape(b_, ncmp, g, CMP_BLOCK * dh)
    return jax.nn.silu(flat @ w1) @ w2


def gated_deltanet(q_in, k_in, v_in, z, a, b, conv_w, a_log, dt_bias, gdn_norm_w):
    b_, s_, _ = q_in.shape
    qkv = jax.nn.silu(causal_conv(jnp.concatenate([q_in, k_in, v_in], axis=-1), conv_w))
    q, k, v = jnp.split(qkv, [GDN_QK_W, 2 * GDN_QK_W], axis=-1)
    q = l2norm(q.reshape(b_, s_, GDN_HEADS, GDN_DK)) * (GDN_DK ** -0.5)
    k = l2norm(k.reshape(b_, s_, GDN_HEADS, GDN_DK))
    v = v.reshape(b_, s_, GDN_HEADS, GDN_DV)
    g = -jnp.exp(a_log.astype(jnp.float32)) * jax.nn.softplus(a.astype(jnp.float32) + dt_bias.astype(jnp.float32))
    beta = jax.nn.sigmoid(b.astype(jnp.float32))
    o = gdn_chunked(q, k, v, g, beta)
    o = rmsnorm(o, gdn_norm_w) * jax.nn.silu(z.reshape(b_, s_, GDN_HEADS, GDN_DV))
    return o.reshape(b_, s_, GDN_V_W)


def native_sparse_attention(q_in, kv_in, gate_in, cmp_pe, cmp_w1, cmp_w2, q_norm_w, k_norm_w, rel_bias):
    b_, s_, _ = q_in.shape
    g_, r_, dh = NSA_GROUPS, NSA_HEADS // NSA_GROUPS, NSA_DH
    qb_sz = NSA_Q_BLOCK
    scale = dh ** -0.5
    q = rmsnorm(q_in.reshape(b_, s_, NSA_HEADS, dh), q_norm_w)
    kc_raw, vc_raw, ks, vs, kw, vw = [t.reshape(b_, s_, g_, dh) for t in jnp.split(kv_in, 6, axis=-1)]
    gates = jax.nn.sigmoid(gate_in).reshape(b_, s_, NSA_HEADS, 3)

    kc = rmsnorm(compress(kc_raw, cmp_pe[0], cmp_w1[0], cmp_w2[0]), k_norm_w[0])
    vc = compress(vc_raw, cmp_pe[1], cmp_w1[1], cmp_w2[1])
    ks = rmsnorm(ks, k_norm_w[1])
    kw = rmsnorm(kw, k_norm_w[2])

    ncmp = kc.shape[1]
    nsel = s_ // SEL_BLOCK
    topn = min(SEL_TOPN, nsel)
    cmp_start = jnp.arange(ncmp) * CMP_STRIDE
    cmp_end = cmp_start + CMP_BLOCK - 1
    sel_start = jnp.arange(nsel) * SEL_BLOCK
    overlap = ((cmp_start[:, None] < sel_start[None, :] + SEL_BLOCK)
               & (cmp_start[:, None] + CMP_BLOCK > sel_start[None, :])).astype(jnp.float32)
    ks_blk = ks.reshape(b_, nsel, SEL_BLOCK, g_, dh).transpose(0, 3, 1, 2, 4)
    vs_blk = vs.reshape(b_, nsel, SEL_BLOCK, g_, dh).transpose(0, 3, 1, 2, 4)
    kw_pad = jnp.pad(kw, ((0, 0), (WINDOW, 0), (0, 0), (0, 0)))
    vw_pad = jnp.pad(vw, ((0, 0), (WINDOW, 0), (0, 0), (0, 0)))
    table_gr = rel_bias.reshape(REL_BUCKETS, g_, r_)
    bi = jnp.arange(b_)[:, None, None, None]
    gi = jnp.arange(g_)[None, :, None, None]
    jsel = jnp.arange(nsel)

    def head_bias(dist):
        return rel_bias[rel_bucket(dist)].transpose(2, 0, 1).reshape(g_, r_, *dist.shape).astype(jnp.float32)

    def block(qs):
        t = qs + jnp.arange(qb_sz)
        qb = lax.dynamic_slice_in_dim(q, qs, qb_sz, axis=1).reshape(b_, qb_sz, g_, r_, dh)
        gb = lax.dynamic_slice_in_dim(gates, qs, qb_sz, axis=1).reshape(b_, qb_sz, g_, r_, 3)
        lc = jnp.einsum("bqgrd,bngd->bgrqn", qb, kc).astype(jnp.float32) * scale \
            + head_bias(t[:, None] - cmp_end[None, :])
        pc = masked_softmax(lc, cmp_end[None, :] <= t[:, None])
        oc = jnp.einsum("bgrqn,bngd->bqgrd", pc.astype(vc.dtype), vc)
        imp = jnp.einsum("bgrqn,nj->bgqj", pc, overlap)
        tb = t // SEL_BLOCK
        forced = (jsel[None, :] == 0) | (jsel[None, :] == tb[:, None]) | (jsel[None, :] == tb[:, None] - 1)
        score = jnp.where(jsel[None, :] <= tb[:, None], imp + FORCE_BONUS * forced, -1e30)
        _, idx = lax.top_k(score, topn)
        k_sel = ks_blk[bi, gi, idx]
        v_sel = vs_blk[bi, gi, idx]
        pos = idx[..., None] * SEL_BLOCK + jnp.arange(SEL_BLOCK)
        dist = t[None, None, :, None, None] - pos
        bias_s = table_gr[rel_bucket(dist), gi[..., None]].transpose(0, 1, 5, 2, 3, 4).astype(jnp.float32)
        ls = jnp.einsum("bqgrd,bgqnkd->bgrqnk", qb, k_sel).astype(jnp.float32) * scale + bias_s
        ls = ls.reshape(b_, g_, r_, qb_sz, topn * SEL_BLOCK)
        ms = (dist >= 0).reshape(b_, g_, 1, qb_sz, topn * SEL_BLOCK)
        ps = masked_softmax(ls, ms)
        o_s = jnp.einsum("bgrqm,bgqmd->bqgrd", ps.astype(v_sel.dtype),
                         v_sel.reshape(b_, g_, qb_sz, topn * SEL_BLOCK, dh))
        kwin = lax.dynamic_slice_in_dim(kw_pad, qs, WINDOW + qb_sz, axis=1)
        vwin = lax.dynamic_slice_in_dim(vw_pad, qs, WINDOW + qb_sz, axis=1)
        wpos = qs - WINDOW + jnp.arange(WINDOW + qb_sz)
        wd = t[:, None] - wpos[None, :]
        lw = jnp.einsum("bqgrd,bkgd->bgrqk", qb, kwin).astype(jnp.float32) * scale + head_bias(wd)
        pw = masked_softmax(lw, (wd >= 0) & (wd < WINDOW) & (wpos[None, :] >= 0))
        ow = jnp.einsum("bgrqk,bkgd->bqgrd", pw.astype(vwin.dtype), vwin)
        o = gb[..., 0:1] * oc + gb[..., 1:2] * o_s + gb[..., 2:3] * ow
        return o.reshape(b_, qb_sz, NSA_W)

    out = lax.map(block, jnp.arange(s_ // qb_sz) * qb_sz)
    return out.transpose(1, 0, 2, 3).reshape(b_, s_, NSA_W)


def setup_inputs(seed: int = 0) -> dict:
    key = jax.random.key(seed)
    ks = jax.random.split(key, 21)
    L = DEPTH
    f32 = jnp.float32

    def nrm(k, shape, scale):
        return jax.random.normal(k, shape, f32) * scale

    dt = jnp.exp(jax.random.uniform(ks[5], (L, GDN_HEADS), f32, math.log(1e-3), math.log(1e-1)))
    return {
        "x": nrm(ks[0], (BATCH, SEQ, D_MODEL), 1.0),
        "norm1_w": 1.0 + nrm(ks[1], (L, D_MODEL), 0.02),
        "w_in": nrm(ks[2], (L, D_MODEL, IN_WIDTH), D_MODEL ** -0.5),
        "conv_w": nrm(ks[3], (L, GDN_CONV, 2 * GDN_QK_W + GDN_V_W), GDN_CONV ** -0.5),
        "a_log": jnp.log(jax.random.uniform(ks[4], (L, GDN_HEADS), f32, 1.0, 16.0)),
        "dt_bias": dt + jnp.log(-jnp.expm1(-dt)),
        "gdn_norm_w": 1.0 + nrm(ks[6], (L, GDN_DV), 0.02),
        "cmp_pe": nrm(ks[7], (L, 2, CMP_BLOCK, NSA_DH), 0.02),
        "cmp_w1": nrm(ks[8], (L, 2, CMP_BLOCK * NSA_DH, CMP_HIDDEN), (CMP_BLOCK * NSA_DH) ** -0.5),
        "cmp_w2": nrm(ks[9], (L, 2, CMP_HIDDEN, NSA_DH), CMP_HIDDEN ** -0.5),
        "q_norm_w": 1.0 + nrm(ks[10], (L, NSA_DH), 0.02),
        "k_norm_w": 1.0 + nrm(ks[11], (L, 3, NSA_DH), 0.02),
        "rel_bias": nrm(ks[12], (REL_BUCKETS, NSA_HEADS), 0.5),
        "w_proj_a": nrm(ks[13], (L, GDN_V_W, D_MODEL), GDN_V_W ** -0.5),
        "w_proj_b": nrm(ks[14], (L, NSA_W, D_MODEL), NSA_W ** -0.5),
        "w_out": nrm(ks[15], (L, D_MODEL, D_MODEL), D_MODEL ** -0.5),
        "norm2_w": 1.0 + nrm(ks[16], (L, D_MODEL), 0.02),
        "w_gate": nrm(ks[17], (L, D_MODEL, FFN_HIDDEN), D_MODEL ** -0.5),
        "w_up": nrm(ks[18], (L, D_MODEL, FFN_HIDDEN), D_MODEL ** -0.5),
        "w_down": nrm(ks[19], (L, FFN_HIDDEN, D_MODEL), FFN_HIDDEN ** -0.5),
    }


def reference(x, norm1_w, w_in, conv_w, a_log, dt_bias, gdn_norm_w, cmp_pe, cmp_w1, cmp_w2,
              q_norm_w, k_norm_w, rel_bias, w_proj_a, w_proj_b, w_out, norm2_w, w_gate, w_up, w_down):
    offsets = [int(o) for o in np.cumsum(IN_SPLITS)[:-1]]
    for l in range(DEPTH):
        h = rmsnorm(x, norm1_w[l])
        proj = h @ w_in[l]
        (gq, gk, gv, gz, ga, gb, nq, nkv, ngate, mgate) = jnp.split(proj, offsets, axis=-1)
        y_a = gated_deltanet(gq, gk, gv, gz, ga, gb, conv_w[l], a_log[l], dt_bias[l], gdn_norm_w[l])
        y_b = native_sparse_attention(nq, nkv, ngate, cmp_pe[l], cmp_w1[l], cmp_w2[l],
                                      q_norm_w[l], k_norm_w[l], rel_bias)
        gate_a, gate_b = jnp.split(jax.nn.sigmoid(mgate), 2, axis=-1)
        mixed = gate_a * (y_a @ w_proj_a[l]) + gate_b * (y_b @ w_proj_b[l])
        x = x + mixed @ w_out[l]
        h2 = rmsnorm(x, norm2_w[l])
        x = x + (jax.nn.silu(h2 @ w_gate[l]) * (h2 @ w_up[l])) @ w_down[l]
    return x
```

```python
import functools
import math

import numpy as np
import jax
import jax.numpy as jnp
from jax import lax
from jax.experimental import pallas as pl
from jax.experimental.pallas import tpu as pltpu

F32 = jnp.float32
BF = jnp.bfloat16

LANES = 128
D_MODEL = 1024
GDN_HEADS = 8
GDN_DK = 128
GDN_DV = 128
GDN_CONV = 4
GDN_CHUNK = 64
NSA_HEADS = 8
NSA_GROUPS = 2
NSA_REP = NSA_HEADS // NSA_GROUPS
NSA_DH = 128
CMP_BLOCK = 32
CMP_STRIDE = 16
CMP_HIDDEN = 256
SEL_BLOCK = 64
SEL_TOPN = 16
WINDOW = 512
FORCE_BONUS = 1000.0
REL_BUCKETS = 32
REL_MAX_DIST = 1024
FFN_HIDDEN = 2816
NORM_EPS = 1e-6
NEG = -1e30
M_INIT = -3e38

CB_GQ, CB_GK, CB_GV, CB_GZ = 0, 8, 16, 24
CB_NQ = 32
CB_MGA, CB_MGB = 40, 48
CB_KC, CB_VC, CB_KS, CB_VS, CB_KW, CB_VW = 56, 58, 60, 62, 64, 66
CB_SMALL = 68
N_CB = 70
SM_A, SM_B, SM_GATE = 0, 8, 16

GDN_ROWS = 256
NSA_TQ = 128
NSA_TK = 128


def _mm(a, b):
    return jnp.dot(a.astype(BF), b.astype(BF), preferred_element_type=F32)


def _mm_nt(a, b):
    return lax.dot_general(a.astype(BF), b.astype(BF), (((1,), (1,)), ((), ())),
                           preferred_element_type=F32)


def _mm_tn(a, b):
    return lax.dot_general(a.astype(BF), b.astype(BF), (((0,), (0,)), ((), ())),
                           preferred_element_type=F32)


def _split3(x):
    x1 = x.astype(BF)
    r1 = x - x1.astype(F32)
    x2 = r1.astype(BF)
    x3 = (r1 - x2.astype(F32)).astype(BF)
    return x1, x2, x3


def _silu(x):
    return x * jax.nn.sigmoid(x)


def _softplus(x):
    return jnp.maximum(x, 0.0) + jnp.log1p(jnp.exp(-jnp.abs(x)))


def _rel_thresholds():
    d = np.arange(0, 4 * REL_MAX_DIST, dtype=np.int64)
    max_exact = REL_BUCKETS // 2
    d_f = np.maximum(d, 1).astype(np.float32)
    large = max_exact + (np.log(d_f / np.float32(max_exact)) / np.float32(math.log(REL_MAX_DIST / max_exact))
                         * np.float32(REL_BUCKETS - max_exact)).astype(np.int32)
    large = np.minimum(large, REL_BUCKETS - 1)
    bucket = np.where(d < max_exact, d, large)
    assert np.all(np.diff(bucket) >= 0)
    return [int(np.argmax(bucket >= k)) for k in range(REL_BUCKETS)]


REL_THR = _rel_thresholds()


def _proj_body(x_ref, nw_ref, w_ref, o_ref, h_ref):
    @pl.when(pl.program_id(1) == 0)
    def _():
        x = x_ref[...]
        y = x * lax.rsqrt(jnp.mean(x * x, axis=-1, keepdims=True) + NORM_EPS)
        h_ref[...] = (y * nw_ref[...]).astype(BF)

    r = jnp.dot(h_ref[...], w_ref[...], preferred_element_type=F32)
    for j in range(o_ref.shape[0]):
        o_ref[j] = r[:, j * LANES:(j + 1) * LANES]


def _proj(x2, norm_w, w_all, tm, tn):
    t, d = x2.shape
    n = w_all.shape[1]
    nb = tn // LANES
    return pl.pallas_call(
        _proj_body,
        grid=(t // tm, n // tn),
        in_specs=[pl.BlockSpec((tm, d), lambda i, j: (i, 0)),
                  pl.BlockSpec((1, d), lambda i, j: (0, 0)),
                  pl.BlockSpec((d, tn), lambda i, j: (0, j))],
        out_specs=pl.BlockSpec((nb, tm, LANES), lambda i, j: (j, i, 0)),
        out_shape=jax.ShapeDtypeStruct((n // LANES, t, LANES), F32),
        scratch_shapes=[pltpu.VMEM((tm, d), BF)],
        compiler_params=pltpu.CompilerParams(dimension_semantics=("parallel", "arbitrary")),
        name="proj",
    )(x2, norm_w, w_all)


def _gdn_body(q_ref, k_ref, v_ref, z_ref, sm_ref, cw_ref, alog_ref, dtb_ref, nw_ref, o_ref,
              ext_ref, st_ref, gb_ref, bb_ref):
    rows = GDN_ROWS
    c = GDN_CHUNK
    nchunk = rows // c
    s = pl.program_id(1)

    @pl.when(s == 0)
    def _():
        ext_ref[:, 0:8, :] = jnp.zeros((3 * GDN_HEADS, 8, LANES), F32)
        st_ref[...] = jnp.zeros_like(st_ref)

    for j in range(GDN_HEADS):
        ext_ref[j, 8:8 + rows, :] = q_ref[j]
        ext_ref[GDN_HEADS + j, 8:8 + rows, :] = k_ref[j]
        ext_ref[2 * GDN_HEADS + j, 8:8 + rows, :] = v_ref[j]

    sm = sm_ref[0]
    for j in range(GDN_HEADS):
        gb_ref[j] = jnp.broadcast_to(sm[:, SM_A + j:SM_A + j + 1], (rows, LANES))
        bb_ref[j] = jnp.broadcast_to(sm[:, SM_B + j:SM_B + j + 1], (rows, LANES))

    def conv(j):
        acc = cw_ref[0, pl.ds(j, 1), :] * ext_ref[j, pl.ds(5, rows), :]
        for i in range(1, GDN_CONV):
            acc = acc + cw_ref[i, pl.ds(j, 1), :] * ext_ref[j, pl.ds(5 + i, rows), :]
        return acc

    def head(h, carry):
        ri = lax.broadcasted_iota(jnp.int32, (rows, rows), 0)
        ci = lax.broadcasted_iota(jnp.int32, (rows, rows), 1)
        same = (ri // c) == (ci // c)
        tril = same & (ri >= ci)
        strict = same & (ri > ci)
        l_tril = jnp.where(tril, 1.0, 0.0).astype(BF)
        l_same = jnp.where(same, 1.0, 0.0).astype(BF)
        eye = jnp.where(ri == ci, 1.0, 0.0).astype(BF)

        qh = _silu(conv(h))
        kh = _silu(conv(GDN_HEADS + h))
        vv = _silu(conv(2 * GDN_HEADS + h))
        qn = qh * lax.rsqrt(jnp.sum(qh * qh, axis=-1, keepdims=True) + NORM_EPS) * (GDN_DK ** -0.5)
        kn = kh * lax.rsqrt(jnp.sum(kh * kh, axis=-1, keepdims=True) + NORM_EPS)

        g = -jnp.exp(alog_ref[pl.ds(h, 1), :]) * _softplus(gb_ref[h] + dtb_ref[pl.ds(h, 1), :])
        beta = jax.nn.sigmoid(bb_ref[h])
        g1, g2, g3 = _split3(g)
        gcum = (jnp.dot(l_tril, g1, preferred_element_type=F32)
                + jnp.dot(l_tril, g2, preferred_element_type=F32)
                + jnp.dot(l_tril, g3, preferred_element_type=F32))
        glast = (jnp.dot(l_same, g1, preferred_element_type=F32)
                 + jnp.dot(l_same, g2, preferred_element_type=F32)
                 + jnp.dot(l_same, g3, preferred_element_type=F32))
        gc2 = jnp.concatenate([gcum, gcum], axis=1)
        c1, c2, c3 = _split3(gc2)
        dn = (((1,), (1,)), ((), ()))
        gr = (lax.dot_general(eye, c1, dn, preferred_element_type=F32)
              + lax.dot_general(eye, c2, dn, preferred_element_type=F32)
              + lax.dot_general(eye, c3, dn, preferred_element_type=F32))
        decay = jnp.where(tril, jnp.exp(jnp.where(tril, gc2 - gr, 0.0)), 0.0)

        kb = kn * beta
        vb = vv * beta
        eg = jnp.exp(gcum)
        kbg = kb * eg
        qg = qn * eg
        kdec = kn * jnp.exp(glast - gcum)
        cd = jnp.exp(glast)

        knb = kn.astype(BF)
        a = jnp.where(strict, _mm_nt(kb, knb) * decay, 0.0)
        intra = _mm_nt(qn, knb) * decay

        p1 = _mm(a, a)
        p2 = _mm(p1, p1)
        p3 = _mm(p2, p2)
        p4 = _mm(p3, p3)
        p5 = _mm(p4, p4)
        qq = p5
        qq = p4 + qq + _mm(p4, qq)
        qq = p3 + qq + _mm(p3, qq)
        qq = p2 + qq + _mm(p2, qq)
        qq = p1 + qq + _mm(p1, qq)
        tm = qq - a - _mm(a, qq)
        tmb = tm.astype(BF)
        value = vb + _mm(tmb, vb)
        kcd = kbg + _mm(tmb, kbg)

        st = st_ref[h]
        outs = []
        for n in range(nchunk):
            sl = slice(n * c, (n + 1) * c)
            vn = value[sl] - _mm(kcd[sl], st)
            outs.append(_mm(qg[sl], st) + _mm(intra[sl, n * c:(n + 1) * c], vn))
            st = st * jnp.concatenate([cd[sl], cd[sl]], axis=0) + _mm_tn(kdec[sl], vn)
        st_ref[h] = st
        o = jnp.concatenate(outs, axis=0)
        on = o * lax.rsqrt(jnp.mean(o * o, axis=-1, keepdims=True) + NORM_EPS) * nw_ref[...]
        o_ref[h] = on * _silu(z_ref[h])
        return carry

    lax.fori_loop(0, GDN_HEADS, head, 0)

    for j in range(3 * GDN_HEADS):
        ext_ref[j, 0:8, :] = ext_ref[j, rows:rows + 8, :]


def _gdn(p3, conv_w3, alog_b, dtb_b, gdn_norm_w, b_, s_):
    rows = GDN_ROWS
    ns = s_ // rows
    hb = GDN_HEADS

    def cb(base):
        return pl.BlockSpec((hb, rows, LANES), lambda b, s, base=base: (base // hb, b * ns + s, 0))

    return pl.pallas_call(
        _gdn_body,
        grid=(b_, ns),
        in_specs=[cb(CB_GQ), cb(CB_GK), cb(CB_GV), cb(CB_GZ),
                  pl.BlockSpec((1, rows, LANES), lambda b, s: (CB_SMALL, b * ns + s, 0)),
                  pl.BlockSpec((GDN_CONV, 3 * hb, LANES), lambda b, s: (0, 0, 0)),
                  pl.BlockSpec((hb, LANES), lambda b, s: (0, 0)),
                  pl.BlockSpec((hb, LANES), lambda b, s: (0, 0)),
                  pl.BlockSpec((1, LANES), lambda b, s: (0, 0))],
        out_specs=pl.BlockSpec((hb, rows, LANES), lambda b, s: (0, b * ns + s, 0)),
        out_shape=jax.ShapeDtypeStruct((hb, b_ * s_, LANES), F32),
        scratch_shapes=[pltpu.VMEM((3 * hb, rows + 8, LANES), F32),
                        pltpu.VMEM((hb, GDN_DK, GDN_DV), F32),
                        pltpu.VMEM((hb, rows, LANES), F32),
                        pltpu.VMEM((hb, rows, LANES), F32)],
        compiler_params=pltpu.CompilerParams(dimension_semantics=("parallel", "arbitrary")),
        name="gdn",
    )(p3, p3, p3, p3, p3, conv_w3, alog_b, dtb_b, gdn_norm_w)


def _cmp_body(x_ref, pe_ref, w1_ref, w2_ref, nw_ref, o_ref, c_ref):
    kv = pl.program_id(1)
    nch = c_ref.shape[0]
    half = CMP_STRIDE * NSA_DH
    for p in range(CMP_STRIDE):
        c_ref[:, p * NSA_DH:(p + 1) * NSA_DH] = x_ref[0, pl.ds(p, nch, stride=CMP_STRIDE), :]
    cc = c_ref[...]
    u = _mm(cc + pe_ref[0, 0:1, :], w1_ref[0, 0:half, :])
    v = _mm(cc + pe_ref[0, 1:2, :], w1_ref[0, half:2 * half, :])
    v_next = jnp.concatenate([v[1:], v[:1]], axis=0)
    hid = _silu(u + v_next)
    out = _mm(hid, w2_ref[0])
    normed = out * lax.rsqrt(jnp.mean(out * out, axis=-1, keepdims=True) + NORM_EPS) * nw_ref[...]
    o_ref[0, 0, 0] = jnp.where(kv == 0, normed, out)


def _cmp(p3, pe2, w1, w2, knw0, b_, s_):
    nch = s_ // CMP_STRIDE
    g_ = NSA_GROUPS
    return pl.pallas_call(
        _cmp_body,
        grid=(b_, 2, g_),
        in_specs=[pl.BlockSpec((1, s_, LANES), lambda b, kv, g: (CB_KC + 2 * kv + g, b, 0)),
                  pl.BlockSpec((1, 2, CMP_STRIDE * NSA_DH), lambda b, kv, g: (kv, 0, 0)),
                  pl.BlockSpec((1, CMP_BLOCK * NSA_DH, CMP_HIDDEN), lambda b, kv, g: (kv, 0, 0)),
                  pl.BlockSpec((1, CMP_HIDDEN, NSA_DH), lambda b, kv, g: (kv, 0, 0)),
                  pl.BlockSpec((1, NSA_DH), lambda b, kv, g: (0, 0))],
        out_specs=pl.BlockSpec((1, 1, 1, nch, NSA_DH), lambda b, kv, g: (b, kv, g, 0, 0)),
        out_shape=jax.ShapeDtypeStruct((b_, 2, g_, nch, NSA_DH), F32),
        scratch_shapes=[pltpu.VMEM((nch, CMP_STRIDE * NSA_DH), F32)],
        compiler_params=pltpu.CompilerParams(dimension_semantics=("parallel", "arbitrary", "arbitrary")),
        name="cmp",
    )(p3, pe2, w1, w2, knw0)


def _bias_of(d, rb_ref, h):
    val = jnp.full(d.shape, rb_ref[0, h], F32)
    for k in range(1, REL_BUCKETS):
        val = jnp.where(d >= REL_THR[k], rb_ref[k, h], val)
    return val


def _bias_body(rb_ref, tc_ref, ts_ref, tw_ref):
    h = pl.program_id(0)
    _, s_, ncp = tc_ref.shape

    def row_tile(it, carry):
        r0 = pl.multiple_of(it * NSA_TQ, NSA_TQ)
        t = r0 + lax.broadcasted_iota(jnp.int32, (NSA_TQ, ncp), 0)
        n = lax.broadcasted_iota(jnp.int32, (NSA_TQ, ncp), 1)
        d = t - (n * CMP_STRIDE + CMP_BLOCK - 1)
        tc_ref[0, pl.ds(r0, NSA_TQ), :] = jnp.where(d >= 0, _bias_of(d, rb_ref, h), NEG)
        return carry

    lax.fori_loop(0, s_ // NSA_TQ, row_tile, 0)
    i = lax.broadcasted_iota(jnp.int32, (NSA_TQ, NSA_TK), 0)
    j = lax.broadcasted_iota(jnp.int32, (NSA_TQ, NSA_TK), 1)
    for a in range(ts_ref.shape[1]):
        d = a * NSA_TK + i - j
        ts_ref[0, a] = jnp.where(d >= 0, _bias_of(d, rb_ref, h), NEG)
    for a in range(tw_ref.shape[1]):
        d = a * NSA_TK + i - j
        tw_ref[0, a] = jnp.where((d >= 0) & (d < WINDOW), _bias_of(d, rb_ref, h), NEG)


def _sel_table_len():
    a = 0
    while a * NSA_TK - (NSA_TK - 1) < REL_THR[REL_BUCKETS - 1]:
        a += 1
    return a + 1


def _bias_tables(rel_bias, s_):
    ncp = s_ // CMP_STRIDE
    nd = _sel_table_len()
    nw = WINDOW // NSA_TK + 1
    return pl.pallas_call(
        _bias_body,
        grid=(NSA_HEADS,),
        in_specs=[pl.BlockSpec(memory_space=pltpu.SMEM)],
        out_specs=[pl.BlockSpec((1, s_, ncp), lambda h: (h, 0, 0)),
                   pl.BlockSpec((1, nd, NSA_TQ, NSA_TK), lambda h: (h, 0, 0, 0)),
                   pl.BlockSpec((1, nw, NSA_TQ, NSA_TK), lambda h: (h, 0, 0, 0))],
        out_shape=[jax.ShapeDtypeStruct((NSA_HEADS, s_, ncp), F32),
                   jax.ShapeDtypeStruct((NSA_HEADS, nd, NSA_TQ, NSA_TK), F32),
                   jax.ShapeDtypeStruct((NSA_HEADS, nw, NSA_TQ, NSA_TK), F32)],
        compiler_params=pltpu.CompilerParams(dimension_semantics=("parallel",)),
        name="bias",
    )(rel_bias)


def _rms_rows(x, w):
    return x * lax.rsqrt(jnp.mean(x * x, axis=-1, keepdims=True) + NORM_EPS) * w


def _nsa_body(q_ref, sm_ref, kc_ref, vc_ref, ks_ref, vs_ref, kw_ref, vw_ref, tc_ref, ts_ref, tw_ref,
              ovl_ref, e_ref, qnw_ref, knw_ref, o_ref,
              ksb, vsb, kwb, vwb, qs_ref, m_ref, l_ref, acc_ref):
    tq, tk, r_ = NSA_TQ, NSA_TK, NSA_REP
    rr = r_ * tq
    g = pl.program_id(1)
    qi = pl.program_id(2)
    nsel = ovl_ref.shape[0]
    topn = min(SEL_TOPN, nsel)
    nd = ts_ref.shape[1]
    nw = tw_ref.shape[1]

    @pl.when(qi == 0)
    def _():
        ksb[...] = _rms_rows(ks_ref[0], knw_ref[1:2, :]).astype(BF)
        kwb[...] = _rms_rows(kw_ref[0], knw_ref[2:3, :]).astype(BF)
        vsb[...] = vs_ref[0].astype(BF)
        vwb[...] = vw_ref[0].astype(BF)

    scale = NSA_DH ** -0.5
    for r in range(r_):
        qs_ref[r * tq:(r + 1) * tq, :] = (_rms_rows(q_ref[r], qnw_ref[...]) * scale).astype(BF)
    qs = qs_ref[...]

    lc = _mm_nt(qs, kc_ref[0, 0, 0]) + jnp.concatenate([tc_ref[r] for r in range(r_)], axis=0)
    mc = jnp.max(lc, axis=-1, keepdims=True)
    pc = jnp.exp(lc - mc)
    lsum = jnp.sum(pc, axis=-1, keepdims=True)
    pc = pc * jnp.where(mc > 0.5 * NEG, 1.0 / jnp.maximum(lsum, 1e-30), 0.0)
    oc = _mm(pc, vc_ref[0, 0, 0])

    psum = pc[0:tq]
    for r in range(1, r_):
        psum = psum + pc[r * tq:(r + 1) * tq]
    s1, s2, s3 = _split3(psum)
    dn = (((1,), (1,)), ((), ()))
    ovl = ovl_ref[...]
    imp_t = (lax.dot_general(ovl, s1, dn, preferred_element_type=F32)
             + lax.dot_general(ovl, s2, dn, preferred_element_type=F32)
             + lax.dot_general(ovl, s3, dn, preferred_element_type=F32))
    jb = lax.broadcasted_iota(jnp.int32, (nsel, tq), 0)
    tpos = qi * tq + lax.broadcasted_iota(jnp.int32, (nsel, tq), 1)
    tb = tpos // SEL_BLOCK
    forced = (jb == 0) | (jb == tb) | (jb == tb - 1)
    score = jnp.where(jb <= tb, imp_t + jnp.where(forced, FORCE_BONUS, 0.0), NEG)
    sel_t = jnp.zeros((nsel, tq), F32)
    for _ in range(topn):
        mx = jnp.max(score, axis=0, keepdims=True)
        idx = jnp.min(jnp.where(score == mx, jb, nsel), axis=0, keepdims=True)
        hit = jb == idx
        sel_t = jnp.where(hit, 1.0, sel_t)
        score = jnp.where(hit, M_INIT, score)
    ri = lax.broadcasted_iota(jnp.int32, (tq, tq), 0)
    ci = lax.broadcasted_iota(jnp.int32, (tq, tq), 1)
    eye = jnp.where(ri == ci, 1.0, 0.0).astype(BF)
    sel_q = lax.dot_general(eye, sel_t.astype(BF), dn, preferred_element_type=F32).astype(BF)

    def flash_init():
        m_ref[...] = jnp.full(m_ref.shape, M_INIT, F32)
        l_ref[...] = jnp.zeros_like(l_ref)
        acc_ref[...] = jnp.zeros_like(acc_ref)

    def flash_update(s, v):
        m_prev = m_ref[...]
        m_next = jnp.maximum(m_prev, jnp.max(s, axis=-1, keepdims=True))
        p = jnp.exp(s - m_next)
        alpha = jnp.exp(m_prev - m_next)
        l_ref[...] = alpha * l_ref[...] + jnp.sum(p, axis=-1, keepdims=True)
        acc_ref[...] = alpha * acc_ref[...] + _mm(p, v)
        m_ref[...] = m_next

    def flash_result():
        return acc_ref[...] / jnp.maximum(l_ref[...], 1e-30)

    flash_init()

    def sel_step(kj, carry):
        off = pl.multiple_of(kj * tk, tk)
        s = _mm_nt(qs, ksb[pl.ds(off, tk), :])
        a = jnp.minimum(qi - kj, nd - 1)
        selb = (jnp.dot(sel_q, e_ref[kj], preferred_element_type=F32) - 1.0) * (-NEG)
        s = s + jnp.concatenate([ts_ref[r, a] + selb for r in range(r_)], axis=0)
        flash_update(s, vsb[pl.ds(off, tk), :])
        return carry

    lax.fori_loop(0, qi + 1, sel_step, 0)
    o_s = flash_result()

    flash_init()

    def win_step(kj, carry):
        off = pl.multiple_of(kj * tk, tk)
        s = _mm_nt(qs, kwb[pl.ds(off, tk), :])
        a = qi - kj
        s = s + jnp.concatenate([tw_ref[r, a] for r in range(r_)], axis=0)
        flash_update(s, vwb[pl.ds(off, tk), :])
        return carry

    lax.fori_loop(jnp.maximum(qi - (nw - 1), 0), qi + 1, win_step, 0)
    o_w = flash_result()

    sg = jax.nn.sigmoid(sm_ref[0])
    for r in range(r_):
        acc = None
        for br, ob in enumerate((oc, o_s, o_w)):
            c0 = SM_GATE + r * 3 + br
            c1 = SM_GATE + (r_ + r) * 3 + br
            gate = jnp.where(g == 0, sg[:, c0:c0 + 1], sg[:, c1:c1 + 1])
            term = gate * ob[r * tq:(r + 1) * tq]
            acc = term if acc is None else acc + term
        o_ref[r] = acc


def _nsa(p3, ckv, tab_c, tab_s, tab_w, ovl_t, e_sel, qnw, knw, b_, s_):
    tq, tk, r_ = NSA_TQ, NSA_TK, NSA_REP
    nq = s_ // tq
    ncp = s_ // CMP_STRIDE
    nsel = s_ // SEL_BLOCK
    nd = tab_s.shape[1]
    nw = tab_w.shape[1]

    def kv_spec(base):
        return pl.BlockSpec((1, s_, LANES), lambda b, g, q, base=base: (base + g, b, 0))

    return pl.pallas_call(
        _nsa_body,
        grid=(b_, NSA_GROUPS, nq),
        in_specs=[pl.BlockSpec((r_, tq, LANES), lambda b, g, q: (CB_NQ // r_ + g, b * nq + q, 0)),
                  pl.BlockSpec((1, tq, LANES), lambda b, g, q: (CB_SMALL, b * nq + q, 0)),
                  pl.BlockSpec((1, 1, 1, ncp, NSA_DH), lambda b, g, q: (b, 0, g, 0, 0)),
                  pl.BlockSpec((1, 1, 1, ncp, NSA_DH), lambda b, g, q: (b, 1, g, 0, 0)),
                  kv_spec(CB_KS), kv_spec(CB_VS), kv_spec(CB_KW), kv_spec(CB_VW),
                  pl.BlockSpec((r_, tq, ncp), lambda b, g, q: (g, q, 0)),
                  pl.BlockSpec((r_, nd, tq, tk), lambda b, g, q: (g, 0, 0, 0)),
                  pl.BlockSpec((r_, nw, tq, tk), lambda b, g, q: (g, 0, 0, 0)),
                  pl.BlockSpec((nsel, ncp), lambda b, g, q: (0, 0)),
                  pl.BlockSpec((s_ // tk, nsel, tk), lambda b, g, q: (0, 0, 0)),
                  pl.BlockSpec((1, NSA_DH), lambda b, g, q: (0, 0)),
                  pl.BlockSpec((3, NSA_DH), lambda b, g, q: (0, 0))],
        out_specs=pl.BlockSpec((r_, tq, LANES), lambda b, g, q: (g, b * nq + q, 0)),
        out_shape=jax.ShapeDtypeStruct((NSA_HEADS, b_ * s_, LANES), F32),
        scratch_shapes=[pltpu.VMEM((s_, NSA_DH), BF), pltpu.VMEM((s_, NSA_DH), BF),
                        pltpu.VMEM((s_, NSA_DH), BF), pltpu.VMEM((s_, NSA_DH), BF),
                        pltpu.VMEM((r_ * tq, NSA_DH), BF),
                        pltpu.VMEM((r_ * tq, LANES), F32), pltpu.VMEM((r_ * tq, LANES), F32),
                        pltpu.VMEM((r_ * tq, NSA_DH), F32)],
        compiler_params=pltpu.CompilerParams(dimension_semantics=("parallel", "arbitrary", "arbitrary")),
        name="nsa",
    )(p3, p3, ckv, ckv, p3, p3, p3, p3, tab_c, tab_s, tab_w, ovl_t, e_sel, qnw, knw)


def _merge_body(x_ref, ya_ref, yb_ref, ga_ref, gb_ref, wpa_ref, wpb_ref, wo_ref, o_ref):
    nh = ya_ref.shape[0]
    ya = jnp.concatenate([ya_ref[j] for j in range(nh)], axis=1).astype(BF)
    yb = jnp.concatenate([yb_ref[j] for j in range(nh)], axis=1).astype(BF)
    ga = jax.nn.sigmoid(jnp.concatenate([ga_ref[j] for j in range(nh)], axis=1))
    gb = jax.nn.sigmoid(jnp.concatenate([gb_ref[j] for j in range(nh)], axis=1))
    mixed = (ga * jnp.dot(ya, wpa_ref[...], preferred_element_type=F32)
             + gb * jnp.dot(yb, wpb_ref[...], preferred_element_type=F32))
    o_ref[...] = x_ref[...] + jnp.dot(mixed.astype(BF), wo_ref[...], preferred_element_type=F32)


def _merge(x2, ya, yb, p3, wpa, wpb, wo, tm):
    t, d = x2.shape
    nh = d // LANES
    hspec = pl.BlockSpec((nh, tm, LANES), lambda i: (0, i, 0))
    wspec = pl.BlockSpec((d, d), lambda i: (0, 0))
    return pl.pallas_call(
        _merge_body,
        grid=(t // tm,),
        in_specs=[pl.BlockSpec((tm, d), lambda i: (i, 0)), hspec, hspec,
                  pl.BlockSpec((nh, tm, LANES), lambda i: (CB_MGA // nh, i, 0)),
                  pl.BlockSpec((nh, tm, LANES), lambda i: (CB_MGB // nh, i, 0)),
                  wspec, wspec, wspec],
        out_specs=pl.BlockSpec((tm, d), lambda i: (i, 0)),
        out_shape=jax.ShapeDtypeStruct((t, d), F32),
        compiler_params=pltpu.CompilerParams(dimension_semantics=("parallel",)),
        name="merge",
    )(x2, ya, yb, p3, p3, wpa, wpb, wo)


def _ffn_body(x_ref, nw_ref, wg_ref, wu_ref, wd_ref, o_ref):
    x = x_ref[...]
    h = (x * lax.rsqrt(jnp.mean(x * x, axis=-1, keepdims=True) + NORM_EPS) * nw_ref[...]).astype(BF)
    gate = jnp.dot(h, wg_ref[...], preferred_element_type=F32)
    up = jnp.dot(h, wu_ref[...], preferred_element_type=F32)
    act = (_silu(gate) * up).astype(BF)
    o_ref[...] = x + jnp.dot(act, wd_ref[...], preferred_element_type=F32)


def _ffn(x2, norm_w, wg, wu, wd, tm):
    t, d = x2.shape
    f = wg.shape[1]
    return pl.pallas_call(
        _ffn_body,
        grid=(t // tm,),
        in_specs=[pl.BlockSpec((tm, d), lambda i: (i, 0)),
                  pl.BlockSpec((1, d), lambda i: (0, 0)),
                  pl.BlockSpec((d, f), lambda i: (0, 0)),
                  pl.BlockSpec((d, f), lambda i: (0, 0)),
                  pl.BlockSpec((f, d), lambda i: (0, 0))],
        out_specs=pl.BlockSpec((tm, d), lambda i: (i, 0)),
        out_shape=jax.ShapeDtypeStruct((t, d), F32),
        compiler_params=pltpu.CompilerParams(dimension_semantics=("parallel",)),
        name="ffn",
    )(x2, norm_w, wg, wu, wd)


def _arrange_w_in(w_in):
    o_ga = 4 * GDN_HEADS * GDN_DK
    o_gb = o_ga + GDN_HEADS
    o_nq = o_gb + GDN_HEADS
    o_nkv = o_nq + NSA_HEADS * NSA_DH
    o_ng = o_nkv + 6 * NSA_GROUPS * NSA_DH
    o_mg = o_ng + 3 * NSA_HEADS
    d = w_in.shape[0]
    small = jnp.concatenate([w_in[:, o_ga:o_nq], w_in[:, o_ng:o_mg],
                             jnp.zeros((d, 2 * LANES - (o_nq - o_ga) - (o_mg - o_ng)), w_in.dtype)], axis=1)
    return jnp.concatenate([w_in[:, :o_ga], w_in[:, o_nq:o_nkv], w_in[:, o_mg:], w_in[:, o_nkv:o_ng], small],
                           axis=1).astype(BF)


def _overlap_t(s_):
    ncp = s_ // CMP_STRIDE
    nsel = s_ // SEL_BLOCK
    cs = np.arange(ncp) * CMP_STRIDE
    ss = np.arange(nsel) * SEL_BLOCK
    ov = (cs[None, :] < ss[:, None] + SEL_BLOCK) & (cs[None, :] + CMP_BLOCK > ss[:, None])
    ov[:, ncp - 1] = False
    return jnp.asarray(ov.astype(np.float32), BF)


def _sel_expand(s_):
    nsel = s_ // SEL_BLOCK
    pos = np.arange(s_)
    e = (pos[None, :] // SEL_BLOCK == np.arange(nsel)[:, None]).astype(np.float32)
    return jnp.asarray(e.reshape(nsel, s_ // NSA_TK, NSA_TK).transpose(1, 0, 2), BF)


def kernel(x, norm1_w, w_in, conv_w, a_log, dt_bias, gdn_norm_w, cmp_pe, cmp_w1, cmp_w2, q_norm_w, k_norm_w,
           rel_bias, w_proj_a, w_proj_b, w_out, norm2_w, w_gate, w_up, w_down):
    b_, s_, d = x.shape
    t = b_ * s_
    x2 = x.reshape(t, d)
    tab_c, tab_s, tab_w = _bias_tables(rel_bias, s_)
    ovl_t = _overlap_t(s_)
    e_sel = _sel_expand(s_)
    for l in range(norm1_w.shape[0]):
        p3 = _proj(x2, norm1_w[l][None, :], _arrange_w_in(w_in[l]), tm=min(1024, t), tn=10 * LANES)
        conv_w3 = conv_w[l].reshape(GDN_CONV, 3 * GDN_HEADS, LANES)
        alog_b = jnp.broadcast_to(a_log[l][:, None], (GDN_HEADS, LANES))
        dtb_b = jnp.broadcast_to(dt_bias[l][:, None], (GDN_HEADS, LANES))
        y_a = _gdn(p3, conv_w3, alog_b, dtb_b, gdn_norm_w[l][None, :], b_, s_)
        pe2 = cmp_pe[l].reshape(2, 2, CMP_STRIDE * NSA_DH)
        ckv = _cmp(p3, pe2, cmp_w1[l].astype(BF), cmp_w2[l].astype(BF), k_norm_w[l][0:1], b_, s_)
        y_b = _nsa(p3, ckv, tab_c, tab_s, tab_w, ovl_t, e_sel, q_norm_w[l][None, :], k_norm_w[l], b_, s_)
        x2 = _merge(x2, y_a, y_b, p3, w_proj_a[l].astype(BF), w_proj_b[l].astype(BF), w_out[l].astype(BF),
                    tm=min(512, t))
        x2 = _ffn(x2, norm2_w[l][None, :], w_gate[l].astype(BF), w_up[l].astype(BF), w_down[l].astype(BF),
                  tm=min(512, t))
    return x2.reshape(b_, s_, d)
```

```python
import functools
import math

import numpy as np
import jax
import jax.numpy as jnp
from jax import lax
from jax.experimental import pallas as pl
from jax.experimental.pallas import tpu as pltpu

F32 = jnp.float32
BF = jnp.bfloat16

LANES = 128
D_MODEL = 1024
GDN_HEADS = 8
GDN_DK = 128
GDN_DV = 128
GDN_CONV = 4
GDN_CHUNK = 64
NSA_HEADS = 8
NSA_GROUPS = 2
NSA_REP = NSA_HEADS // NSA_GROUPS
NSA_DH = 128
CMP_BLOCK = 32
CMP_STRIDE = 16
CMP_HIDDEN = 256
SEL_BLOCK = 64
SEL_TOPN = 16
WINDOW = 512
FORCE_BONUS = 1000.0
REL_BUCKETS = 32
REL_MAX_DIST = 1024
FFN_HIDDEN = 2816
NORM_EPS = 1e-6
NEG = -1e30
M_INIT = -3e38
LOG2E = 1.4426950408889634

CB_GQ, CB_GK, CB_GV, CB_GZ = 0, 8, 16, 24
CB_NQ = 32
CB_MGA, CB_MGB = 40, 48
CB_KC, CB_VC, CB_KS, CB_VS, CB_KW, CB_VW = 56, 58, 60, 62, 64, 66
CB_SMALL = 68
N_CB = 70
SM_A, SM_B, SM_GATE = 0, 8, 16

GDN_ROWS = 256
GDN_GROUP = 4
NSA_TQ = 128
NSA_TB = 128
NSA_TK_SEL = 512
NSA_TK_WIN = 256
NSA_TU = 256
NSA_PAD = NSA_TK_SEL // NSA_TB - 1


def _mm(a, b):
    return jnp.dot(a.astype(BF), b.astype(BF), preferred_element_type=F32)


def _mm_nt(a, b):
    return lax.dot_general(a.astype(BF), b.astype(BF), (((1,), (1,)), ((), ())),
                           preferred_element_type=F32)


def _mm_tn(a, b):
    return lax.dot_general(a.astype(BF), b.astype(BF), (((0,), (0,)), ((), ())),
                           preferred_element_type=F32)


def _split3(x):
    x1 = x.astype(BF)
    r1 = x - x1.astype(F32)
    x2 = r1.astype(BF)
    x3 = (r1 - x2.astype(F32)).astype(BF)
    return x1, x2, x3


def _silu(x):
    return x * jax.nn.sigmoid(x)


def _softplus(x):
    return jnp.maximum(x, 0.0) + jnp.log1p(jnp.exp(-jnp.abs(x)))


def _rel_thresholds():
    d = np.arange(0, 4 * REL_MAX_DIST, dtype=np.int64)
    max_exact = REL_BUCKETS // 2
    d_f = np.maximum(d, 1).astype(np.float32)
    large = max_exact + (np.log(d_f / np.float32(max_exact)) / np.float32(math.log(REL_MAX_DIST / max_exact))
                         * np.float32(REL_BUCKETS - max_exact)).astype(np.int32)
    large = np.minimum(large, REL_BUCKETS - 1)
    bucket = np.where(d < max_exact, d, large)
    assert np.all(np.diff(bucket) >= 0)
    return [int(np.argmax(bucket >= k)) for k in range(REL_BUCKETS)]


REL_THR = _rel_thresholds()


def _proj_body(x_ref, nw_ref, w_ref, o_ref, h_ref):
    @pl.when(pl.program_id(1) == 0)
    def _():
        x = x_ref[...]
        y = x * lax.rsqrt(jnp.mean(x * x, axis=-1, keepdims=True) + NORM_EPS)
        h_ref[...] = (y * nw_ref[...]).astype(BF)

    r = jnp.dot(h_ref[...], w_ref[...], preferred_element_type=F32)
    for j in range(o_ref.shape[0]):
        o_ref[j] = r[:, j * LANES:(j + 1) * LANES]


def _proj(x2, norm_w, w_all, tm, tn):
    t, d = x2.shape
    n = w_all.shape[1]
    nb = tn // LANES
    return pl.pallas_call(
        _proj_body,
        grid=(t // tm, n // tn),
        in_specs=[pl.BlockSpec((tm, d), lambda i, j: (i, 0)),
                  pl.BlockSpec((1, d), lambda i, j: (0, 0)),
                  pl.BlockSpec((d, tn), lambda i, j: (0, j))],
        out_specs=pl.BlockSpec((nb, tm, LANES), lambda i, j: (j, i, 0)),
        out_shape=jax.ShapeDtypeStruct((n // LANES, t, LANES), F32),
        scratch_shapes=[pltpu.VMEM((tm, d), BF)],
        compiler_params=pltpu.CompilerParams(dimension_semantics=("parallel", "arbitrary")),
        name="proj",
    )(x2, norm_w, w_all)


def _gdn_body(q_ref, k_ref, v_ref, z_ref, sm_ref, cw_ref, alog_ref, dtb_ref, nw_ref, o_ref,
              ext_ref, st_ref, gb_ref, bb_ref):
    rows = GDN_ROWS
    c = GDN_CHUNK
    nchunk = rows // c
    s = pl.program_id(1)

    @pl.when(s == 0)
    def _():
        ext_ref[:, 0:8, :] = jnp.zeros((3 * GDN_HEADS, 8, LANES), F32)
        st_ref[...] = jnp.zeros_like(st_ref)

    for j in range(GDN_HEADS):
        ext_ref[j, 8:8 + rows, :] = q_ref[j]
        ext_ref[GDN_HEADS + j, 8:8 + rows, :] = k_ref[j]
        ext_ref[2 * GDN_HEADS + j, 8:8 + rows, :] = v_ref[j]

    sm = sm_ref[0]
    for j in range(GDN_HEADS):
        gb_ref[j] = jnp.broadcast_to(sm[:, SM_A + j:SM_A + j + 1], (rows, LANES))
        bb_ref[j] = jnp.broadcast_to(sm[:, SM_B + j:SM_B + j + 1], (rows, LANES))

    def conv(j):
        acc = cw_ref[0, pl.ds(j, 1), :] * ext_ref[j, pl.ds(5, rows), :]
        for i in range(1, GDN_CONV):
            acc = acc + cw_ref[i, pl.ds(j, 1), :] * ext_ref[j, pl.ds(5 + i, rows), :]
        return acc

    def head_setup(h):
        ri = lax.broadcasted_iota(jnp.int32, (rows, rows), 0)
        ci = lax.broadcasted_iota(jnp.int32, (rows, rows), 1)
        same = (ri // c) == (ci // c)
        tril = same & (ri >= ci)
        strict = same & (ri > ci)
        l_tril = jnp.where(tril, 1.0, 0.0).astype(BF)

        qh = _silu(conv(h))
        kh = _silu(conv(GDN_HEADS + h))
        vv = _silu(conv(2 * GDN_HEADS + h))
        qn = qh * lax.rsqrt(jnp.sum(qh * qh, axis=-1, keepdims=True) + NORM_EPS) * (GDN_DK ** -0.5)
        kn = kh * lax.rsqrt(jnp.sum(kh * kh, axis=-1, keepdims=True) + NORM_EPS)

        g = -jnp.exp(alog_ref[pl.ds(h, 1), :]) * _softplus(gb_ref[h] + dtb_ref[pl.ds(h, 1), :])
        beta = jax.nn.sigmoid(bb_ref[h])
        g1, g2, g3 = _split3(g)
        gcum = (jnp.dot(l_tril, g1, preferred_element_type=F32)
                + jnp.dot(l_tril, g2, preferred_element_type=F32)
                + jnp.dot(l_tril, g3, preferred_element_type=F32))
        glast = jnp.concatenate(
            [jnp.broadcast_to(gcum[(n + 1) * c - 1:(n + 1) * c, :], (c, LANES)) for n in range(nchunk)],
            axis=0)
        gc2 = jnp.concatenate([gcum, gcum], axis=1)
        gct = gcum.T
        gr = jnp.concatenate([gct, gct], axis=0)
        decay = jnp.where(tril, jnp.exp(jnp.where(tril, gc2 - gr, 0.0)), 0.0)

        kb = kn * beta
        eg = jnp.exp(gcum)
        knb = kn.astype(BF)
        return dict(
            a=jnp.where(strict, _mm_nt(kb, knb) * decay, 0.0),
            intra=_mm_nt(qn, knb) * decay,
            rhs=jnp.concatenate([vv * beta, kb * eg], axis=1),
            qg=qn * eg, kdec=kn * jnp.exp(glast - gcum), cd=jnp.exp(glast))

    def head_group(hg, carry):
        hs = [GDN_GROUP * hg + u for u in range(GDN_GROUP)]
        w = [head_setup(h) for h in hs]
        p = [_mm(d["a"], d["a"]) for d in w]
        x = [-d["a"] for d in w]
        for j in range(1, 6):
            xp = [_mm(xi, pi) for xi, pi in zip(x, p)]
            if j < 5:
                p_next = [_mm(pi, pi) for pi in p]
            x = [xi + pi + xpi for xi, pi, xpi in zip(x, p, xp)]
            if j < 5:
                p = p_next
        sol = [d["rhs"] + _mm(xi, d["rhs"]) for d, xi in zip(w, x)]
        st = [st_ref[h] for h in hs]
        outs = [[] for _ in hs]
        for n in range(nchunk):
            sl = slice(n * c, (n + 1) * c)
            ks = [_mm(jnp.concatenate([s_[sl, GDN_DV:], d["qg"][sl]], axis=0), si)
                  for s_, d, si in zip(sol, w, st)]
            vn = [s_[sl, :GDN_DV] - k_[:c] for s_, k_ in zip(sol, ks)]
            for u, (d, k_, v_) in enumerate(zip(w, ks, vn)):
                outs[u].append(k_[c:] + _mm(d["intra"][sl, n * c:(n + 1) * c], v_))
            st = [si * jnp.concatenate([d["cd"][sl], d["cd"][sl]], axis=0) + _mm_tn(d["kdec"][sl], v_)
                  for si, d, v_ in zip(st, w, vn)]
        for u, h in enumerate(hs):
            st_ref[h] = st[u]
            o = jnp.concatenate(outs[u], axis=0)
            on = o * lax.rsqrt(jnp.mean(o * o, axis=-1, keepdims=True) + NORM_EPS) * nw_ref[...]
            o_ref[h] = on * _silu(z_ref[h])
        return carry

    lax.fori_loop(0, GDN_HEADS // GDN_GROUP, head_group, 0)

    for j in range(3 * GDN_HEADS):
        ext_ref[j, 0:8, :] = ext_ref[j, rows:rows + 8, :]


def _gdn(p3, conv_w3, alog_b, dtb_b, gdn_norm_w, b_, s_):
    rows = GDN_ROWS
    ns = s_ // rows
    hb = GDN_HEADS

    def cb(base):
        return pl.BlockSpec((hb, rows, LANES), lambda b, s, base=base: (base // hb, b * ns + s, 0))

    return pl.pallas_call(
        _gdn_body,
        grid=(b_, ns),
        in_specs=[cb(CB_GQ), cb(CB_GK), cb(CB_GV), cb(CB_GZ),
                  pl.BlockSpec((1, rows, LANES), lambda b, s: (CB_SMALL, b * ns + s, 0)),
                  pl.BlockSpec((GDN_CONV, 3 * hb, LANES), lambda b, s: (0, 0, 0)),
                  pl.BlockSpec((hb, LANES), lambda b, s: (0, 0)),
                  pl.BlockSpec((hb, LANES), lambda b, s: (0, 0)),
                  pl.BlockSpec((1, LANES), lambda b, s: (0, 0))],
        out_specs=pl.BlockSpec((hb, rows, LANES), lambda b, s: (0, b * ns + s, 0)),
        out_shape=jax.ShapeDtypeStruct((hb, b_ * s_, LANES), F32),
        scratch_shapes=[pltpu.VMEM((3 * hb, rows + 8, LANES), F32),
                        pltpu.VMEM((hb, GDN_DK, GDN_DV), F32),
                        pltpu.VMEM((hb, rows, LANES), F32),
                        pltpu.VMEM((hb, rows, LANES), F32)],
        compiler_params=pltpu.CompilerParams(dimension_semantics=("parallel", "arbitrary")),
        name="gdn",
    )(p3, p3, p3, p3, p3, conv_w3, alog_b, dtb_b, gdn_norm_w)


def _cmp_body(x_ref, pe_ref, w1_ref, w2_ref, nw_ref, o_ref, c_ref):
    kv = pl.program_id(1)
    nch = c_ref.shape[0]
    half = CMP_STRIDE * NSA_DH
    for p in range(CMP_STRIDE):
        c_ref[:, p * NSA_DH:(p + 1) * NSA_DH] = x_ref[0, pl.ds(p, nch, stride=CMP_STRIDE), :]
    cc = c_ref[...]
    u = _mm(cc + pe_ref[0, 0:1, :], w1_ref[0, 0:half, :])
    v = _mm(cc + pe_ref[0, 1:2, :], w1_ref[0, half:2 * half, :])
    v_next = jnp.concatenate([v[1:], v[:1]], axis=0)
    hid = _silu(u + v_next)
    out = _mm(hid, w2_ref[0])
    normed = out * lax.rsqrt(jnp.mean(out * out, axis=-1, keepdims=True) + NORM_EPS) * nw_ref[...]
    o_ref[0, 0, 0] = jnp.where(kv == 0, normed, out)


def _cmp(p3, pe2, w1, w2, knw0, b_, s_):
    nch = s_ // CMP_STRIDE
    g_ = NSA_GROUPS
    return pl.pallas_call(
        _cmp_body,
        grid=(b_, 2, g_),
        in_specs=[pl.BlockSpec((1, s_, LANES), lambda b, kv, g: (CB_KC + 2 * kv + g, b, 0)),
                  pl.BlockSpec((1, 2, CMP_STRIDE * NSA_DH), lambda b, kv, g: (kv, 0, 0)),
                  pl.BlockSpec((1, CMP_BLOCK * NSA_DH, CMP_HIDDEN), lambda b, kv, g: (kv, 0, 0)),
                  pl.BlockSpec((1, CMP_HIDDEN, NSA_DH), lambda b, kv, g: (kv, 0, 0)),
                  pl.BlockSpec((1, NSA_DH), lambda b, kv, g: (0, 0))],
        out_specs=pl.BlockSpec((1, 1, 1, nch, NSA_DH), lambda b, kv, g: (b, kv, g, 0, 0)),
        out_shape=jax.ShapeDtypeStruct((b_, 2, g_, nch, NSA_DH), F32),
        scratch_shapes=[pltpu.VMEM((nch, CMP_STRIDE * NSA_DH), F32)],
        compiler_params=pltpu.CompilerParams(dimension_semantics=("parallel", "arbitrary", "arbitrary")),
        name="cmp",
    )(p3, pe2, w1, w2, knw0)


def _bias_of(d, rb_ref, h):
    val = jnp.full(d.shape, rb_ref[0, h], F32)
    for k in range(1, REL_BUCKETS):
        val = jnp.where(d >= REL_THR[k], rb_ref[k, h], val)
    return val * LOG2E


def _bias_body(rb_ref, tc_ref, ts_ref, tw_ref):
    h = pl.program_id(0)
    _, s_, ncp = tc_ref.shape

    def row_tile(it, carry):
        r0 = pl.multiple_of(it * NSA_TQ, NSA_TQ)
        t = r0 + lax.broadcasted_iota(jnp.int32, (NSA_TQ, ncp), 0)
        n = lax.broadcasted_iota(jnp.int32, (NSA_TQ, ncp), 1)
        d = t - (n * CMP_STRIDE + CMP_BLOCK - 1)
        tc_ref[0, pl.ds(r0, NSA_TQ), :] = jnp.where(d >= 0, _bias_of(d, rb_ref, h), NEG)
        return carry

    lax.fori_loop(0, s_ // NSA_TQ, row_tile, 0)
    i = lax.broadcasted_iota(jnp.int32, (NSA_TB, NSA_TB), 0)
    j = lax.broadcasted_iota(jnp.int32, (NSA_TB, NSA_TB), 1)
    for e in range(ts_ref.shape[1]):
        d = (e - NSA_PAD) * NSA_TB + i - j
        ts_ref[0, e] = jnp.where(d >= 0, _bias_of(d, rb_ref, h), NEG)
    for e in range(tw_ref.shape[1]):
        d = (e - NSA_PAD) * NSA_TB + i - j
        tw_ref[0, e] = jnp.where((d >= 0) & (d < WINDOW), _bias_of(d, rb_ref, h), NEG)


def _sel_table_len():
    a = 0
    while a * NSA_TB - (NSA_TB - 1) < REL_THR[REL_BUCKETS - 1]:
        a += 1
    return a + 1 + NSA_PAD


def _win_table_len():
    return (WINDOW + NSA_TB - 1) // NSA_TB + 2 + NSA_PAD


def _bias_tables(rel_bias, s_):
    ncp = s_ // CMP_STRIDE
    nd = _sel_table_len()
    nw = _win_table_len()
    return pl.pallas_call(
        _bias_body,
        grid=(NSA_HEADS,),
        in_specs=[pl.BlockSpec(memory_space=pltpu.SMEM)],
        out_specs=[pl.BlockSpec((1, s_, ncp), lambda h: (h, 0, 0)),
                   pl.BlockSpec((1, nd, NSA_TB, NSA_TB), lambda h: (h, 0, 0, 0)),
                   pl.BlockSpec((1, nw, NSA_TB, NSA_TB), lambda h: (h, 0, 0, 0))],
        out_shape=[jax.ShapeDtypeStruct((NSA_HEADS, s_, ncp), F32),
                   jax.ShapeDtypeStruct((NSA_HEADS, nd, NSA_TB, NSA_TB), F32),
                   jax.ShapeDtypeStruct((NSA_HEADS, nw, NSA_TB, NSA_TB), F32)],
        compiler_params=pltpu.CompilerParams(dimension_semantics=("parallel",)),
        name="bias",
    )(rel_bias)


def _rms_rows(x, w):
    return x * lax.rsqrt(jnp.mean(x * x, axis=-1, keepdims=True) + NORM_EPS) * w


def _nsa_body(q_ref, sm_ref, kc_ref, vc_ref, ks_ref, vs_ref, kw_ref, vw_ref, tc_ref, ts_ref, tw_ref,
              ovl_ref, e_ref, qnw_ref, knw_ref, o_ref,
              ksb, vsb, kwb, vwb, qa_ref, m_ref, acc_ref, sa_ref, sb_ref):
    tq, r_, tb = NSA_TQ, NSA_REP, NSA_TB
    g = pl.program_id(1)
    qi = pl.program_id(2)
    nsel = ovl_ref.shape[0]
    topn = min(SEL_TOPN, nsel)

    @pl.when(qi == 0)
    def _():
        ones = jnp.ones(vs_ref.shape[1:], BF)
        ksb[:, :NSA_DH] = _rms_rows(ks_ref[0], knw_ref[1:2, :]).astype(BF)
        ksb[:, NSA_DH:] = e_ref[...]
        kwb[...] = _rms_rows(kw_ref[0], knw_ref[2:3, :]).astype(BF)
        vsb[:, :NSA_DH] = vs_ref[0].astype(BF)
        vsb[:, NSA_DH:] = ones
        vwb[:, :NSA_DH] = vw_ref[0].astype(BF)
        vwb[:, NSA_DH:] = ones

    qscale = NSA_DH ** -0.5 * LOG2E
    for r in range(r_):
        qa_ref[r * tq:(r + 1) * tq, :NSA_DH] = (_rms_rows(q_ref[r], qnw_ref[...]) * qscale).astype(BF)
    qs = qa_ref[:, :NSA_DH]

    lc = _mm_nt(qs, kc_ref[0, 0, 0]) + jnp.concatenate([tc_ref[r] for r in range(r_)], axis=0)
    mc = jnp.max(lc, axis=-1, keepdims=True)
    pc = jnp.exp2(lc - mc)
    lsum = jnp.sum(pc, axis=-1, keepdims=True)
    pc = pc * jnp.where(mc > 0.5 * NEG, 1.0 / jnp.maximum(lsum, 1e-30), 0.0)
    oc = _mm(pc, vc_ref[0, 0, 0])

    psum = pc[0:tq]
    for r in range(1, r_):
        psum = psum + pc[r * tq:(r + 1) * tq]
    s1, s2, s3 = _split3(psum)
    dn = (((1,), (1,)), ((), ()))
    ovl = ovl_ref[...]
    imp_t = (lax.dot_general(ovl, s1, dn, preferred_element_type=F32)
             + lax.dot_general(ovl, s2, dn, preferred_element_type=F32)
             + lax.dot_general(ovl, s3, dn, preferred_element_type=F32))
    jb = lax.broadcasted_iota(jnp.int32, (nsel, tq), 0)
    tpos = qi * tq + lax.broadcasted_iota(jnp.int32, (nsel, tq), 1)
    tblk = tpos // SEL_BLOCK
    forced = (jb == 0) | (jb == tblk) | (jb == tblk - 1)
    score = jnp.where(jb <= tblk, imp_t + jnp.where(forced, FORCE_BONUS, 0.0), NEG)
    unsel_t = jnp.full((nsel, tq), NEG, F32)
    for _ in range(topn):
        mx = jnp.max(score, axis=0, keepdims=True)
        idx = jnp.min(jnp.where(score == mx, jb, nsel), axis=0, keepdims=True)
        hit = jb == idx
        unsel_t = jnp.where(hit, 0.0, unsel_t)
        score = jnp.where(hit, M_INIT, score)
    unsel_pad = jnp.concatenate([unsel_t, jnp.zeros((LANES - nsel, tq), F32)], axis=0).astype(BF)
    ri = lax.broadcasted_iota(jnp.int32, (tq, tq), 0)
    ci = lax.broadcasted_iota(jnp.int32, (tq, tq), 1)
    eye = jnp.where(ri == ci, 1.0, 0.0).astype(BF)
    unsel_q = lax.dot_general(eye, unsel_pad, dn, preferred_element_type=F32).astype(BF)
    for r in range(r_):
        qa_ref[r * tq:(r + 1) * tq, NSA_DH:] = unsel_q

    def flash_init():
        m_ref[...] = jnp.full(m_ref.shape, M_INIT, F32)
        acc_ref[...] = jnp.zeros_like(acc_ref)

    def logits(kj, q, k_sc, tk, dst_ref):
        off = pl.multiple_of(kj * tk, tk)
        dst_ref[:, :tk] = _mm_nt(q, k_sc[pl.ds(off, tk), :])

    def flash_step(kj, s_ref, v_sc, tab_ref, tk):
        nct = tk // tb
        off = pl.multiple_of(kj * tk, tk)
        e0 = qi - kj * nct + NSA_PAD
        idx = [jnp.clip(e0 - c, 0, tab_ref.shape[1] - 1) for c in range(nct)]
        m_prev = m_ref[...]
        m_rows, p_rows = [], []
        for r in range(r_):
            rs = slice(r * tq, (r + 1) * tq)
            pieces = [s_ref[rs, c * tb:(c + 1) * tb] + tab_ref[r, idx[c]] for c in range(nct)]
            mx = pieces[0]
            for c in range(1, nct):
                mx = jnp.maximum(mx, pieces[c])
            m_next = jnp.maximum(m_prev[rs], jnp.max(mx, axis=-1, keepdims=True))
            m_rows.append(m_next)
            p_rows.append(jnp.concatenate([jnp.exp2(pc_ - m_next).astype(BF) for pc_ in pieces], axis=1))
        m_next = jnp.concatenate(m_rows, axis=0)
        p = jnp.concatenate(p_rows, axis=0)
        alpha = jnp.exp2(m_prev - m_next)
        acc_ref[...] = (jnp.concatenate([alpha, alpha], axis=1) * acc_ref[...]
                        + jnp.dot(p, v_sc[pl.ds(off, tk), :], preferred_element_type=F32))
        m_ref[...] = m_next

    def flash_result():
        acc = acc_ref[...]
        return acc[:, :NSA_DH] / jnp.maximum(acc[:, NSA_DH:], 1e-30)

    def sweep(first, last, q, k_sc, v_sc, tab_ref, tk):
        count = last - first + 1
        logits(first, q, k_sc, tk, sa_ref)

        def pair(i, carry):
            kj = first + 2 * i
            logits(kj + 1, q, k_sc, tk, sb_ref)
            flash_step(kj, sa_ref, v_sc, tab_ref, tk)
            logits(jnp.minimum(kj + 2, last), q, k_sc, tk, sa_ref)
            flash_step(kj + 1, sb_ref, v_sc, tab_ref, tk)
            return carry

        lax.fori_loop(0, count // 2, pair, 0)

        @pl.when(count % 2 == 1)
        def _():
            flash_step(last, sa_ref, v_sc, tab_ref, tk)

    flash_init()
    sweep(0, (qi * tq) // NSA_TK_SEL, qa_ref[...], ksb, vsb, ts_ref, NSA_TK_SEL)
    o_s = flash_result()

    flash_init()
    sweep(jnp.maximum(qi * tq - WINDOW, 0) // NSA_TK_WIN, (qi * tq) // NSA_TK_WIN,
          qs, kwb, vwb, tw_ref, NSA_TK_WIN)
    o_w = flash_result()

    sg = jax.nn.sigmoid(sm_ref[0])
    for r in range(r_):
        acc = None
        for br, ob in enumerate((oc, o_s, o_w)):
            c0 = SM_GATE + r * 3 + br
            c1 = SM_GATE + (r_ + r) * 3 + br
            gate = jnp.where(g == 0, sg[:, c0:c0 + 1], sg[:, c1:c1 + 1])
            term = gate * ob[r * tq:(r + 1) * tq]
            acc = term if acc is None else acc + term
        o_ref[r] = acc


def _nsa(p3, ckv, tab_c, tab_s, tab_w, ovl_t, e_sel, qnw, knw, b_, s_):
    tq, tk, r_ = NSA_TQ, NSA_TB, NSA_REP
    nq = s_ // tq
    ncp = s_ // CMP_STRIDE
    nsel = s_ // SEL_BLOCK
    nd = tab_s.shape[1]
    nw = tab_w.shape[1]
    assert s_ % NSA_TK_SEL == 0 and NSA_TK_SEL // NSA_TB - 1 <= NSA_PAD

    def kv_spec(base):
        return pl.BlockSpec((1, s_, LANES), lambda b, g, q, base=base: (base + g, b, 0))

    return pl.pallas_call(
        _nsa_body,
        grid=(b_, NSA_GROUPS, nq),
        in_specs=[pl.BlockSpec((r_, tq, LANES), lambda b, g, q: (CB_NQ // r_ + g, b * nq + q, 0)),
                  pl.BlockSpec((1, tq, LANES), lambda b, g, q: (CB_SMALL, b * nq + q, 0)),
                  pl.BlockSpec((1, 1, 1, ncp, NSA_DH), lambda b, g, q: (b, 0, g, 0, 0)),
                  pl.BlockSpec((1, 1, 1, ncp, NSA_DH), lambda b, g, q: (b, 1, g, 0, 0)),
                  kv_spec(CB_KS), kv_spec(CB_VS), kv_spec(CB_KW), kv_spec(CB_VW),
                  pl.BlockSpec((r_, tq, ncp), lambda b, g, q: (g, q, 0)),
                  pl.BlockSpec((r_, nd, tq, tk), lambda b, g, q: (g, 0, 0, 0)),
                  pl.BlockSpec((r_, nw, tq, tk), lambda b, g, q: (g, 0, 0, 0)),
                  pl.BlockSpec((nsel, ncp), lambda b, g, q: (0, 0)),
                  pl.BlockSpec((s_, LANES), lambda b, g, q: (0, 0)),
                  pl.BlockSpec((1, NSA_DH), lambda b, g, q: (0, 0)),
                  pl.BlockSpec((3, NSA_DH), lambda b, g, q: (0, 0))],
        out_specs=pl.BlockSpec((r_, tq, LANES), lambda b, g, q: (g, b * nq + q, 0)),
        out_shape=jax.ShapeDtypeStruct((NSA_HEADS, b_ * s_, LANES), F32),
        scratch_shapes=[pltpu.VMEM((s_, 2 * NSA_DH), BF), pltpu.VMEM((s_, 2 * NSA_DH), BF),
                        pltpu.VMEM((s_, NSA_DH), BF), pltpu.VMEM((s_, 2 * NSA_DH), BF),
                        pltpu.VMEM((r_ * tq, 2 * NSA_DH), BF),
                        pltpu.VMEM((r_ * tq, LANES), F32),
                        pltpu.VMEM((r_ * tq, 2 * NSA_DH), F32),
                        pltpu.VMEM((r_ * tq, NSA_TK_SEL), F32), pltpu.VMEM((r_ * tq, NSA_TK_SEL), F32)],
        compiler_params=pltpu.CompilerParams(dimension_semantics=("parallel", "arbitrary", "arbitrary")),
        name="nsa",
    )(p3, p3, ckv, ckv, p3, p3, p3, p3, tab_c, tab_s, tab_w, ovl_t, e_sel, qnw, knw)


def _merge_body(x_ref, ya_ref, yb_ref, ga_ref, gb_ref, wpa_ref, wpb_ref, wo_ref, o_ref):
    nh = ya_ref.shape[0]
    ya = jnp.concatenate([ya_ref[j] for j in range(nh)], axis=1).astype(BF)
    yb = jnp.concatenate([yb_ref[j] for j in range(nh)], axis=1).astype(BF)
    ga = jax.nn.sigmoid(jnp.concatenate([ga_ref[j] for j in range(nh)], axis=1))
    gb = jax.nn.sigmoid(jnp.concatenate([gb_ref[j] for j in range(nh)], axis=1))
    mixed = (ga * jnp.dot(ya, wpa_ref[...], preferred_element_type=F32)
             + gb * jnp.dot(yb, wpb_ref[...], preferred_element_type=F32))
    o_ref[...] = x_ref[...] + jnp.dot(mixed.astype(BF), wo_ref[...], preferred_element_type=F32)


def _merge(x2, ya, yb, p3, wpa, wpb, wo, tm):
    t, d = x2.shape
    nh = d // LANES
    hspec = pl.BlockSpec((nh, tm, LANES), lambda i: (0, i, 0))
    wspec = pl.BlockSpec((d, d), lambda i: (0, 0))
    return pl.pallas_call(
        _merge_body,
        grid=(t // tm,),
        in_specs=[pl.BlockSpec((tm, d), lambda i: (i, 0)), hspec, hspec,
                  pl.BlockSpec((nh, tm, LANES), lambda i: (CB_MGA // nh, i, 0)),
                  pl.BlockSpec((nh, tm, LANES), lambda i: (CB_MGB // nh, i, 0)),
                  wspec, wspec, wspec],
        out_specs=pl.BlockSpec((tm, d), lambda i: (i, 0)),
        out_shape=jax.ShapeDtypeStruct((t, d), F32),
        compiler_params=pltpu.CompilerParams(dimension_semantics=("parallel",)),
        name="merge",
    )(x2, ya, yb, p3, p3, wpa, wpb, wo)


def _ffn_body(x_ref, nw_ref, wg_ref, wu_ref, wd_ref, o_ref):
    x = x_ref[...]
    h = (x * lax.rsqrt(jnp.mean(x * x, axis=-1, keepdims=True) + NORM_EPS) * nw_ref[...]).astype(BF)
    gate = jnp.dot(h, wg_ref[...], preferred_element_type=F32)
    up = jnp.dot(h, wu_ref[...], preferred_element_type=F32)
    act = (_silu(gate) * up).astype(BF)
    o_ref[...] = x + jnp.dot(act, wd_ref[...], preferred_element_type=F32)


def _ffn(x2, norm_w, wg, wu, wd, tm):
    t, d = x2.shape
    f = wg.shape[1]
    return pl.pallas_call(
        _ffn_body,
        grid=(t // tm,),
        in_specs=[pl.BlockSpec((tm, d), lambda i: (i, 0)),
                  pl.BlockSpec((1, d), lambda i: (0, 0)),
                  pl.BlockSpec((d, f), lambda i: (0, 0)),
                  pl.BlockSpec((d, f), lambda i: (0, 0)),
                  pl.BlockSpec((f, d), lambda i: (0, 0))],
        out_specs=pl.BlockSpec((tm, d), lambda i: (i, 0)),
        out_shape=jax.ShapeDtypeStruct((t, d), F32),
        compiler_params=pltpu.CompilerParams(dimension_semantics=("parallel",)),
        name="ffn",
    )(x2, norm_w, wg, wu, wd)


def _arrange_w_in(w_in):
    o_ga = 4 * GDN_HEADS * GDN_DK
    o_gb = o_ga + GDN_HEADS
    o_nq = o_gb + GDN_HEADS
    o_nkv = o_nq + NSA_HEADS * NSA_DH
    o_ng = o_nkv + 6 * NSA_GROUPS * NSA_DH
    o_mg = o_ng + 3 * NSA_HEADS
    d = w_in.shape[0]
    small = jnp.concatenate([w_in[:, o_ga:o_nq], w_in[:, o_ng:o_mg],
                             jnp.zeros((d, 2 * LANES - (o_nq - o_ga) - (o_mg - o_ng)), w_in.dtype)], axis=1)
    return jnp.concatenate([w_in[:, :o_ga], w_in[:, o_nq:o_nkv], w_in[:, o_mg:], w_in[:, o_nkv:o_ng], small],
                           axis=1).astype(BF)


def _overlap_t(s_):
    ncp = s_ // CMP_STRIDE
    nsel = s_ // SEL_BLOCK
    cs = np.arange(ncp) * CMP_STRIDE
    ss = np.arange(nsel) * SEL_BLOCK
    ov = (cs[None, :] < ss[:, None] + SEL_BLOCK) & (cs[None, :] + CMP_BLOCK > ss[:, None])
    ov[:, ncp - 1] = False
    return jnp.asarray(ov.astype(np.float32), BF)


def _sel_expand(s_):
    assert s_ // SEL_BLOCK <= LANES
    pos = np.arange(s_)
    e = (pos[:, None] // SEL_BLOCK == np.arange(LANES)[None, :]).astype(np.float32)
    return jnp.asarray(e, BF)


def kernel(x, norm1_w, w_in, conv_w, a_log, dt_bias, gdn_norm_w, cmp_pe, cmp_w1, cmp_w2, q_norm_w, k_norm_w,
           rel_bias, w_proj_a, w_proj_b, w_out, norm2_w, w_gate, w_up, w_down):
    b_, s_, d = x.shape
    t = b_ * s_
    x2 = x.reshape(t, d)
    tab_c, tab_s, tab_w = _bias_tables(rel_bias, s_)
    ovl_t = _overlap_t(s_)
    e_sel = _sel_expand(s_)
    for l in range(norm1_w.shape[0]):
        p3 = _proj(x2, norm1_w[l][None, :], _arrange_w_in(w_in[l]), tm=min(1024, t), tn=10 * LANES)
        conv_w3 = conv_w[l].reshape(GDN_CONV, 3 * GDN_HEADS, LANES)
        alog_b = jnp.broadcast_to(a_log[l][:, None], (GDN_HEADS, LANES))
        dtb_b = jnp.broadcast_to(dt_bias[l][:, None], (GDN_HEADS, LANES))
        y_a = _gdn(p3, conv_w3, alog_b, dtb_b, gdn_norm_w[l][None, :], b_, s_)
        pe2 = cmp_pe[l].reshape(2, 2, CMP_STRIDE * NSA_DH)
        ckv = _cmp(p3, pe2, cmp_w1[l].astype(BF), cmp_w2[l].astype(BF), k_norm_w[l][0:1], b_, s_)
        y_b = _nsa(p3, ckv, tab_c, tab_s, tab_w, ovl_t, e_sel, q_norm_w[l][None, :], k_norm_w[l], b_, s_)
        x2 = _merge(x2, y_a, y_b, p3, w_proj_a[l].astype(BF), w_proj_b[l].astype(BF), w_out[l].astype(BF),
                    tm=min(512, t))
        x2 = _ffn(x2, norm2_w[l][None, :], w_gate[l].astype(BF), w_up[l].astype(BF), w_down[l].astype(BF),
                  tm=min(512, t))
    return x2.reshape(b_, s_, d)
```

```python
import functools
import math

import numpy as np
import jax
import jax.numpy as jnp
from jax import lax
from jax.experimental import pallas as pl
from jax.experimental.pallas import tpu as pltpu

F32 = jnp.float32
BF = jnp.bfloat16

LANES = 128
D_MODEL = 1024
GDN_HEADS = 8
GDN_DK = 128
GDN_DV = 128
GDN_CONV = 4
GDN_CHUNK = 64
NSA_HEADS = 8
NSA_GROUPS = 2
NSA_REP = NSA_HEADS // NSA_GROUPS
NSA_DH = 128
CMP_BLOCK = 32
CMP_STRIDE = 16
CMP_HIDDEN = 256
SEL_BLOCK = 64
SEL_TOPN = 16
WINDOW = 512
FORCE_BONUS = 1000.0
REL_BUCKETS = 32
REL_MAX_DIST = 1024
FFN_HIDDEN = 2816
NORM_EPS = 1e-6
NEG = -1e30
M_INIT = -3e38
LOG2E = 1.4426950408889634

CB_GQ, CB_GK, CB_GV, CB_GZ = 0, 8, 16, 24
CB_NQ = 32
CB_MGA, CB_MGB = 40, 48
CB_KC, CB_VC, CB_KS, CB_VS, CB_KW, CB_VW = 56, 58, 60, 62, 64, 66
CB_SMALL = 68
N_CB = 70
SM_A, SM_B, SM_GATE = 0, 8, 16

GDN_ROWS = 256
GDN_GROUP = 8
NSA_TQ = 256
NSA_TB = 128
NSA_TK_SEL = 512
NSA_TK_WIN = 256
NSA_WIN_TILES = (WINDOW + max(NSA_TQ, NSA_TK_WIN) - 2) // NSA_TK_WIN + 1
NSA_PAD = NSA_TK_SEL // NSA_TB - 1


def _mm(a, b):
    return jnp.dot(a.astype(BF), b.astype(BF), preferred_element_type=F32)


def _mm_nt(a, b):
    return lax.dot_general(a.astype(BF), b.astype(BF), (((1,), (1,)), ((), ())),
                           preferred_element_type=F32)


def _mm_tn(a, b):
    return lax.dot_general(a.astype(BF), b.astype(BF), (((0,), (0,)), ((), ())),
                           preferred_element_type=F32)


def _split3(x):
    x1 = x.astype(BF)
    r1 = x - x1.astype(F32)
    x2 = r1.astype(BF)
    x3 = (r1 - x2.astype(F32)).astype(BF)
    return x1, x2, x3


def _sigmoid(x):
    return 0.5 * jnp.tanh(0.5 * x) + 0.5


def _silu(x):
    return x * _sigmoid(x)


def _softplus(x):
    return jnp.maximum(x, 0.0) + jnp.log1p(jnp.exp(-jnp.abs(x)))


def _rel_thresholds():
    d = np.arange(0, 4 * REL_MAX_DIST, dtype=np.int64)
    max_exact = REL_BUCKETS // 2
    d_f = np.maximum(d, 1).astype(np.float32)
    large = max_exact + (np.log(d_f / np.float32(max_exact)) / np.float32(math.log(REL_MAX_DIST / max_exact))
                         * np.float32(REL_BUCKETS - max_exact)).astype(np.int32)
    large = np.minimum(large, REL_BUCKETS - 1)
    bucket = np.where(d < max_exact, d, large)
    assert np.all(np.diff(bucket) >= 0)
    return [int(np.argmax(bucket >= k)) for k in range(REL_BUCKETS)]


REL_THR = _rel_thresholds()


def _proj_body(x_ref, nw_ref, w_ref, o_ref, h_ref):
    @pl.when(pl.program_id(1) == 0)
    def _():
        x = x_ref[...]
        y = x * lax.rsqrt(jnp.mean(x * x, axis=-1, keepdims=True) + NORM_EPS)
        h_ref[...] = (y * nw_ref[...]).astype(BF)

    r = jnp.dot(h_ref[...], w_ref[...], preferred_element_type=F32)
    for j in range(o_ref.shape[0]):
        o_ref[j] = r[:, j * LANES:(j + 1) * LANES]


def _proj(x2, norm_w, w_all, tm, tn):
    t, d = x2.shape
    n = w_all.shape[1]
    nb = tn // LANES
    return pl.pallas_call(
        _proj_body,
        grid=(t // tm, n // tn),
        in_specs=[pl.BlockSpec((tm, d), lambda i, j: (i, 0)),
                  pl.BlockSpec((1, d), lambda i, j: (0, 0)),
                  pl.BlockSpec((d, tn), lambda i, j: (0, j))],
        out_specs=pl.BlockSpec((nb, tm, LANES), lambda i, j: (j, i, 0)),
        out_shape=jax.ShapeDtypeStruct((n // LANES, t, LANES), F32),
        scratch_shapes=[pltpu.VMEM((tm, d), BF)],
        compiler_params=pltpu.CompilerParams(dimension_semantics=("parallel", "arbitrary")),
        name="proj",
    )(x2, norm_w, w_all)


def _gdn_body(q_ref, k_ref, v_ref, z_ref, sm_ref, cw_ref, alog_ref, dtb_ref, nw_ref, o_ref,
              ext_ref, st_ref, gb_ref, bb_ref):
    rows = GDN_ROWS
    c = GDN_CHUNK
    nchunk = rows // c
    s = pl.program_id(1)

    @pl.when(s == 0)
    def _():
        ext_ref[:, 0:8, :] = jnp.zeros((3 * GDN_HEADS, 8, LANES), F32)
        st_ref[...] = jnp.zeros_like(st_ref)

    for j in range(GDN_HEADS):
        ext_ref[j, 8:8 + rows, :] = q_ref[j]
        ext_ref[GDN_HEADS + j, 8:8 + rows, :] = k_ref[j]
        ext_ref[2 * GDN_HEADS + j, 8:8 + rows, :] = v_ref[j]

    sm = sm_ref[0]
    for j in range(GDN_HEADS):
        gb_ref[j] = jnp.broadcast_to(sm[:, SM_A + j:SM_A + j + 1], (rows, LANES))
        bb_ref[j] = jnp.broadcast_to(sm[:, SM_B + j:SM_B + j + 1], (rows, LANES))

    def conv(j):
        acc = cw_ref[0, pl.ds(j, 1), :] * ext_ref[j, pl.ds(5, rows), :]
        for i in range(1, GDN_CONV):
            acc = acc + cw_ref[i, pl.ds(j, 1), :] * ext_ref[j, pl.ds(5 + i, rows), :]
        return acc

    def head_setup(h, tril, strict, l_tril):
        qh = _silu(conv(h))
        kh = _silu(conv(GDN_HEADS + h))
        vv = _silu(conv(2 * GDN_HEADS + h))
        qn = qh * lax.rsqrt(jnp.sum(qh * qh, axis=-1, keepdims=True) + NORM_EPS) * (GDN_DK ** -0.5)
        kn = kh * lax.rsqrt(jnp.sum(kh * kh, axis=-1, keepdims=True) + NORM_EPS)

        g = -jnp.exp(alog_ref[pl.ds(h, 1), :]) * _softplus(gb_ref[h] + dtb_ref[pl.ds(h, 1), :])
        beta = _sigmoid(bb_ref[h])
        g1, g2, g3 = _split3(g)
        gcum = (jnp.dot(l_tril, g1, preferred_element_type=F32)
                + jnp.dot(l_tril, g2, preferred_element_type=F32)
                + jnp.dot(l_tril, g3, preferred_element_type=F32))
        glast = jnp.concatenate(
            [jnp.broadcast_to(gcum[(n + 1) * c - 1:(n + 1) * c, :], (c, LANES)) for n in range(nchunk)],
            axis=0)
        gc2 = jnp.concatenate([gcum, gcum], axis=1)
        gct = gcum.T
        gr = jnp.concatenate([gct, gct], axis=0)
        decay = jnp.where(tril, jnp.exp(jnp.where(tril, gc2 - gr, 0.0)), 0.0)

        kb = kn * beta
        eg = jnp.exp(gcum)
        knb = kn.astype(BF)
        return dict(
            a=jnp.where(strict, _mm_nt(kb, knb) * decay, 0.0),
            intra=_mm_nt(qn, knb) * decay,
            rhs=jnp.concatenate([vv * beta, kb * eg], axis=1),
            qg=qn * eg, kdec=kn * jnp.exp(glast - gcum), cd=jnp.exp(glast))

    def head_group(hg, carry):
        hs = [GDN_GROUP * hg + u for u in range(GDN_GROUP)]
        ri = lax.broadcasted_iota(jnp.int32, (rows, rows), 0)
        ci = lax.broadcasted_iota(jnp.int32, (rows, rows), 1)
        same = (ri // c) == (ci // c)
        tril = same & (ri >= ci)
        strict = same & (ri > ci)
        l_tril = jnp.where(tril, 1.0, 0.0).astype(BF)
        eye = jnp.where(ri == ci, 1.0, 0.0)
        w = [head_setup(h, tril, strict, l_tril) for h in hs]
        p = [_mm(d["a"], d["a"]) for d in w]
        t = [eye - d["a"] for d in w]
        for j in range(1, 6):
            tp = [_mm(ti, pi) for ti, pi in zip(t, p)]
            if j < 5:
                p = [_mm(pi, pi) for pi in p]
            t = [ti + tpi for ti, tpi in zip(t, tp)]
        sol = [d["rhs"] + _mm(ti - eye, d["rhs"]) for d, ti in zip(w, t)]
        st = [st_ref[h] for h in hs]
        outs = [[] for _ in hs]
        for n in range(nchunk):
            sl = slice(n * c, (n + 1) * c)
            ks = [_mm(jnp.concatenate([s_[sl, GDN_DV:], d["qg"][sl]], axis=0), si)
                  for s_, d, si in zip(sol, w, st)]
            vn = [s_[sl, :GDN_DV] - k_[:c] for s_, k_ in zip(sol, ks)]
            for u, (d, k_, v_) in enumerate(zip(w, ks, vn)):
                outs[u].append(k_[c:] + _mm(d["intra"][sl, n * c:(n + 1) * c], v_))
            st = [si * jnp.concatenate([d["cd"][sl], d["cd"][sl]], axis=0) + _mm_tn(d["kdec"][sl], v_)
                  for si, d, v_ in zip(st, w, vn)]
        for u, h in enumerate(hs):
            st_ref[h] = st[u]
            o = jnp.concatenate(outs[u], axis=0)
            on = o * lax.rsqrt(jnp.mean(o * o, axis=-1, keepdims=True) + NORM_EPS) * nw_ref[...]
            o_ref[h] = on * _silu(z_ref[h])
        return carry

    lax.fori_loop(0, GDN_HEADS // GDN_GROUP, head_group, 0)

    for j in range(3 * GDN_HEADS):
        ext_ref[j, 0:8, :] = ext_ref[j, rows:rows + 8, :]


def _gdn(p3, conv_w3, alog_b, dtb_b, gdn_norm_w, b_, s_):
    rows = GDN_ROWS
    ns = s_ // rows
    hb = GDN_HEADS

    def cb(base):
        return pl.BlockSpec((hb, rows, LANES), lambda b, s, base=base: (base // hb, b * ns + s, 0))

    return pl.pallas_call(
        _gdn_body,
        grid=(b_, ns),
        in_specs=[cb(CB_GQ), cb(CB_GK), cb(CB_GV), cb(CB_GZ),
                  pl.BlockSpec((1, rows, LANES), lambda b, s: (CB_SMALL, b * ns + s, 0)),
                  pl.BlockSpec((GDN_CONV, 3 * hb, LANES), lambda b, s: (0, 0, 0)),
                  pl.BlockSpec((hb, LANES), lambda b, s: (0, 0)),
                  pl.BlockSpec((hb, LANES), lambda b, s: (0, 0)),
                  pl.BlockSpec((1, LANES), lambda b, s: (0, 0))],
        out_specs=pl.BlockSpec((hb, rows, LANES), lambda b, s: (0, b * ns + s, 0)),
        out_shape=jax.ShapeDtypeStruct((hb, b_ * s_, LANES), F32),
        scratch_shapes=[pltpu.VMEM((3 * hb, rows + 8, LANES), F32),
                        pltpu.VMEM((hb, GDN_DK, GDN_DV), F32),
                        pltpu.VMEM((hb, rows, LANES), F32),
                        pltpu.VMEM((hb, rows, LANES), F32)],
        compiler_params=pltpu.CompilerParams(dimension_semantics=("parallel", "arbitrary")),
        name="gdn",
    )(p3, p3, p3, p3, p3, conv_w3, alog_b, dtb_b, gdn_norm_w)


def _cmp_body(x_ref, pe_ref, w1_ref, w2_ref, nw_ref, o_ref, c_ref):
    kv = pl.program_id(1)
    nch = c_ref.shape[0]
    half = CMP_STRIDE * NSA_DH
    for p in range(CMP_STRIDE):
        c_ref[:, p * NSA_DH:(p + 1) * NSA_DH] = x_ref[0, pl.ds(p, nch, stride=CMP_STRIDE), :]
    cc = c_ref[...]
    u = _mm(cc + pe_ref[0, 0:1, :], w1_ref[0, 0:half, :])
    v = _mm(cc + pe_ref[0, 1:2, :], w1_ref[0, half:2 * half, :])
    v_next = jnp.concatenate([v[1:], v[:1]], axis=0)
    hid = _silu(u + v_next)
    out = _mm(hid, w2_ref[0])
    normed = out * lax.rsqrt(jnp.mean(out * out, axis=-1, keepdims=True) + NORM_EPS) * nw_ref[...]
    o_ref[0, 0, 0] = jnp.where(kv == 0, normed, out)


def _cmp(p3, pe2, w1, w2, knw0, b_, s_):
    nch = s_ // CMP_STRIDE
    g_ = NSA_GROUPS
    return pl.pallas_call(
        _cmp_body,
        grid=(b_, 2, g_),
        in_specs=[pl.BlockSpec((1, s_, LANES), lambda b, kv, g: (CB_KC + 2 * kv + g, b, 0)),
                  pl.BlockSpec((1, 2, CMP_STRIDE * NSA_DH), lambda b, kv, g: (kv, 0, 0)),
                  pl.BlockSpec((1, CMP_BLOCK * NSA_DH, CMP_HIDDEN), lambda b, kv, g: (kv, 0, 0)),
                  pl.BlockSpec((1, CMP_HIDDEN, NSA_DH), lambda b, kv, g: (kv, 0, 0)),
                  pl.BlockSpec((1, NSA_DH), lambda b, kv, g: (0, 0))],
        out_specs=pl.BlockSpec((1, 1, 1, nch, NSA_DH), lambda b, kv, g: (b, kv, g, 0, 0)),
        out_shape=jax.ShapeDtypeStruct((b_, 2, g_, nch, NSA_DH), F32),
        scratch_shapes=[pltpu.VMEM((nch, CMP_STRIDE * NSA_DH), F32)],
        compiler_params=pltpu.CompilerParams(dimension_semantics=("parallel", "arbitrary", "arbitrary")),
        name="cmp",
    )(p3, pe2, w1, w2, knw0)


def _bias_of(d, rb_ref, h):
    val = jnp.full(d.shape, rb_ref[0, h], F32)
    for k in range(1, REL_BUCKETS):
        val = jnp.where(d >= REL_THR[k], rb_ref[k, h], val)
    return val * LOG2E


def _bias_body(rb_ref, tc_ref, ts_ref, tw_ref):
    h = pl.program_id(0)
    _, s_, ncp = tc_ref.shape

    def row_tile(it, carry):
        r0 = pl.multiple_of(it * NSA_TQ, NSA_TQ)
        t = r0 + lax.broadcasted_iota(jnp.int32, (NSA_TQ, ncp), 0)
        n = lax.broadcasted_iota(jnp.int32, (NSA_TQ, ncp), 1)
        d = t - (n * CMP_STRIDE + CMP_BLOCK - 1)
        tc_ref[0, pl.ds(r0, NSA_TQ), :] = jnp.where(d >= 0, _bias_of(d, rb_ref, h), NEG)
        return carry

    lax.fori_loop(0, s_ // NSA_TQ, row_tile, 0)
    i = lax.broadcasted_iota(jnp.int32, (NSA_TB, NSA_TB), 0)
    j = lax.broadcasted_iota(jnp.int32, (NSA_TB, NSA_TB), 1)
    for e in range(ts_ref.shape[1]):
        d = (e - NSA_PAD) * NSA_TB + i - j
        ts_ref[0, e] = jnp.where(d >= 0, _bias_of(d, rb_ref, h), NEG)
    for e in range(tw_ref.shape[1]):
        d = (e - NSA_PAD) * NSA_TB + i - j
        tw_ref[0, e] = jnp.where((d >= 0) & (d < WINDOW), _bias_of(d, rb_ref, h), NEG)


def _sel_table_len():
    a = 0
    while a * NSA_TB - (NSA_TB - 1) < REL_THR[REL_BUCKETS - 1]:
        a += 1
    return a + 1 + NSA_PAD


def _win_table_len():
    return (WINDOW + NSA_TB - 1) // NSA_TB + 2 + NSA_PAD


def _bias_tables(rel_bias, s_):
    ncp = s_ // CMP_STRIDE
    nd = _sel_table_len()
    nw = _win_table_len()
    return pl.pallas_call(
        _bias_body,
        grid=(NSA_HEADS,),
        in_specs=[pl.BlockSpec(memory_space=pltpu.SMEM)],
        out_specs=[pl.BlockSpec((1, s_, ncp), lambda h: (h, 0, 0)),
                   pl.BlockSpec((1, nd, NSA_TB, NSA_TB), lambda h: (h, 0, 0, 0)),
                   pl.BlockSpec((1, nw, NSA_TB, NSA_TB), lambda h: (h, 0, 0, 0))],
        out_shape=[jax.ShapeDtypeStruct((NSA_HEADS, s_, ncp), F32),
                   jax.ShapeDtypeStruct((NSA_HEADS, nd, NSA_TB, NSA_TB), F32),
                   jax.ShapeDtypeStruct((NSA_HEADS, nw, NSA_TB, NSA_TB), F32)],
        compiler_params=pltpu.CompilerParams(dimension_semantics=("parallel",)),
        name="bias",
    )(rel_bias)


def _rms_rows(x, w):
    return x * lax.rsqrt(jnp.mean(x * x, axis=-1, keepdims=True) + NORM_EPS) * w


def _nsa_body(q_ref, sm_ref, kc_ref, vc_ref, ks_ref, vs_ref, kw_ref, vw_ref, tc_ref, ts_ref, tw_ref,
              ovl_ref, e_ref, qnw_ref, knw_ref, o_ref,
              ksb, vsb, kwb, vwb, qa_ref, ms_ref, accs_ref, mw_ref, accw_ref, sa_ref, sb_ref, wa_ref, wb_ref):
    tq, r_, tb = NSA_TQ, NSA_REP, NSA_TB
    g = pl.program_id(1)
    qi = pl.program_id(2)
    nsel = ovl_ref.shape[0]
    topn = min(SEL_TOPN, nsel)

    @pl.when(qi == 0)
    def _():
        ones = jnp.ones(vs_ref.shape[1:], BF)
        ksb[:, :NSA_DH] = _rms_rows(ks_ref[0], knw_ref[1:2, :]).astype(BF)
        ksb[:, NSA_DH:] = e_ref[...]
        kwb[...] = _rms_rows(kw_ref[0], knw_ref[2:3, :]).astype(BF)
        vsb[:, :NSA_DH] = vs_ref[0].astype(BF)
        vsb[:, NSA_DH:] = ones
        vwb[:, :NSA_DH] = vw_ref[0].astype(BF)
        vwb[:, NSA_DH:] = ones

    qscale = NSA_DH ** -0.5 * LOG2E
    for r in range(r_):
        qa_ref[r * tq:(r + 1) * tq, :NSA_DH] = (_rms_rows(q_ref[r], qnw_ref[...]) * qscale).astype(BF)
    qs = qa_ref[:, :NSA_DH]

    def flash_init(m_ref, acc_ref):
        m_ref[...] = jnp.full(m_ref.shape, M_INIT, F32)
        acc_ref[...] = jnp.zeros_like(acc_ref)

    def logits(kj, q, k_sc, tk, dst_ref):
        off = pl.multiple_of(jnp.maximum(kj, 0) * tk, tk)
        dst_ref[:, :tk] = _mm_nt(q, k_sc[pl.ds(off, tk), :])

    def flash_step(kj, s_ref, v_sc, tab_ref, tk, m_ref, acc_ref):
        nct = tk // tb
        nrt = tq // tb
        off = pl.multiple_of(jnp.maximum(kj, 0) * tk, tk)
        e0 = qi * nrt - kj * nct + NSA_PAD
        idx = {o: jnp.where(kj >= 0, jnp.clip(e0 + o, 0, tab_ref.shape[1] - 1), 0)
               for o in range(-(nct - 1), nrt)}
        m_prev = m_ref[...]
        m_rows, p_rows = [], []
        for rb in range(r_ * nrt):
            r, rho = divmod(rb, nrt)
            rs = slice(rb * tb, (rb + 1) * tb)
            pieces = [s_ref[rs, c * tb:(c + 1) * tb] + tab_ref[r, idx[rho - c]] for c in range(nct)]
            mx = pieces[0]
            for c in range(1, nct):
                mx = jnp.maximum(mx, pieces[c])
            m_next = jnp.maximum(m_prev[rs], jnp.max(mx, axis=-1, keepdims=True))
            m_rows.append(m_next)
            p_rows.append(jnp.concatenate([jnp.exp2(pc_ - m_next).astype(BF) for pc_ in pieces], axis=1))
        m_next = jnp.concatenate(m_rows, axis=0)
        p = jnp.concatenate(p_rows, axis=0)
        alpha = jnp.exp2(m_prev - m_next)
        acc_ref[...] = (jnp.concatenate([alpha, alpha], axis=1) * acc_ref[...]
                        + jnp.dot(p, v_sc[pl.ds(off, tk), :], preferred_element_type=F32))
        m_ref[...] = m_next

    def flash_result(acc_ref):
        acc = acc_ref[...]
        return acc[:, :NSA_DH] / jnp.maximum(acc[:, NSA_DH:], 1e-30)

    wlast = (qi * tq + tq - 1) // NSA_TK_WIN
    wtiles = [wlast - (NSA_WIN_TILES - 1) + u for u in range(NSA_WIN_TILES)]
    wbufs = (wa_ref, wb_ref)

    def win_step(u):
        flash_step(wtiles[u], wbufs[u % 2], vwb, tw_ref, NSA_TK_WIN, mw_ref, accw_ref)

    flash_init(mw_ref, accw_ref)
    logits(wtiles[0], qs, kwb, NSA_TK_WIN, wbufs[0])
    logits(wtiles[1], qs, kwb, NSA_TK_WIN, wbufs[1])

    lc = _mm_nt(qs, kc_ref[0, 0, 0]) + jnp.concatenate([tc_ref[r] for r in range(r_)], axis=0)
    mc = jnp.max(lc, axis=-1, keepdims=True)
    pc = jnp.exp2(lc - mc)
    lsum = jnp.sum(pc, axis=-1, keepdims=True)
    pc = pc * jnp.where(mc > 0.5 * NEG, 1.0 / jnp.maximum(lsum, 1e-30), 0.0)
    oc = _mm(pc, vc_ref[0, 0, 0])

    win_step(0)
    for u in range(2, NSA_WIN_TILES):
        logits(wtiles[u], qs, kwb, NSA_TK_WIN, wbufs[u % 2])
        win_step(u - 1)

    psum = pc[0:tq]
    for r in range(1, r_):
        psum = psum + pc[r * tq:(r + 1) * tq]
    s1, s2, s3 = _split3(psum)
    dn = (((1,), (1,)), ((), ()))
    ovl = ovl_ref[...]
    imp_t = (lax.dot_general(ovl, s1, dn, preferred_element_type=F32)
             + lax.dot_general(ovl, s2, dn, preferred_element_type=F32)
             + lax.dot_general(ovl, s3, dn, preferred_element_type=F32))
    jb = lax.broadcasted_iota(jnp.int32, (nsel, tq), 0)
    tpos = qi * tq + lax.broadcasted_iota(jnp.int32, (nsel, tq), 1)
    tblk = tpos // SEL_BLOCK
    forced = (jb == 0) | (jb == tblk) | (jb == tblk - 1)
    score = jnp.where(jb <= tblk, imp_t + jnp.where(forced, FORCE_BONUS, 0.0), NEG)
    unsel_t = jnp.full((nsel, tq), NEG, F32)
    for _ in range(topn):
        mx = jnp.max(score, axis=0, keepdims=True)
        idx = jnp.min(jnp.where(score == mx, jb, nsel), axis=0, keepdims=True)
        hit = jb == idx
        unsel_t = jnp.where(hit, 0.0, unsel_t)
        score = jnp.where(hit, M_INIT, score)
    unsel_pad = jnp.concatenate([unsel_t, jnp.zeros((LANES - nsel, tq), F32)], axis=0).astype(BF)
    ri = lax.broadcasted_iota(jnp.int32, (tq, tq), 0)
    ci = lax.broadcasted_iota(jnp.int32, (tq, tq), 1)
    eye = jnp.where(ri == ci, 1.0, 0.0).astype(BF)
    unsel_q = lax.dot_general(eye, unsel_pad, dn, preferred_element_type=F32).astype(BF)
    for r in range(r_):
        qa_ref[r * tq:(r + 1) * tq, NSA_DH:] = unsel_q

    win_step(NSA_WIN_TILES - 1)

    qa = qa_ref[...]
    slast = (qi * tq + tq - 1) // NSA_TK_SEL
    flash_init(ms_ref, accs_ref)
    logits(0, qa, ksb, NSA_TK_SEL, sa_ref)

    def sel_step(kj, s_ref):
        flash_step(kj, s_ref, vsb, ts_ref, NSA_TK_SEL, ms_ref, accs_ref)

    def pair(i, carry):
        kj = 2 * i
        logits(kj + 1, qa, ksb, NSA_TK_SEL, sb_ref)
        sel_step(kj, sa_ref)
        logits(jnp.minimum(kj + 2, slast), qa, ksb, NSA_TK_SEL, sa_ref)
        sel_step(kj + 1, sb_ref)
        return carry

    lax.fori_loop(0, (slast + 1) // 2, pair, 0)

    @pl.when(slast % 2 == 0)
    def _():
        sel_step(slast, sa_ref)

    o_s = flash_result(accs_ref)
    o_w = flash_result(accw_ref)

    sg = _sigmoid(sm_ref[0])
    for r in range(r_):
        acc = None
        for br, ob in enumerate((oc, o_s, o_w)):
            c0 = SM_GATE + r * 3 + br
            c1 = SM_GATE + (r_ + r) * 3 + br
            gate = jnp.where(g == 0, sg[:, c0:c0 + 1], sg[:, c1:c1 + 1])
            term = gate * ob[r * tq:(r + 1) * tq]
            acc = term if acc is None else acc + term
        o_ref[r] = acc


def _nsa(p3, ckv, tab_c, tab_s, tab_w, ovl_t, e_sel, qnw, knw, b_, s_):
    tq, tk, r_ = NSA_TQ, NSA_TB, NSA_REP
    nq = s_ // tq
    ncp = s_ // CMP_STRIDE
    nsel = s_ // SEL_BLOCK
    nd = tab_s.shape[1]
    nw = tab_w.shape[1]
    assert s_ % NSA_TK_SEL == 0 and NSA_TK_SEL // NSA_TB - 1 <= NSA_PAD
    assert NSA_TQ % NSA_TB == 0 and max(NSA_TQ, NSA_TK_WIN) % min(NSA_TQ, NSA_TK_WIN) == 0

    def kv_spec(base):
        return pl.BlockSpec((1, s_, LANES), lambda b, g, q, base=base: (base + g, b, 0))

    return pl.pallas_call(
        _nsa_body,
        grid=(b_, NSA_GROUPS, nq),
        in_specs=[pl.BlockSpec((r_, tq, LANES), lambda b, g, q: (CB_NQ // r_ + g, b * nq + q, 0)),
                  pl.BlockSpec((1, tq, LANES), lambda b, g, q: (CB_SMALL, b * nq + q, 0)),
                  pl.BlockSpec((1, 1, 1, ncp, NSA_DH), lambda b, g, q: (b, 0, g, 0, 0)),
                  pl.BlockSpec((1, 1, 1, ncp, NSA_DH), lambda b, g, q: (b, 1, g, 0, 0)),
                  kv_spec(CB_KS), kv_spec(CB_VS), kv_spec(CB_KW), kv_spec(CB_VW),
                  pl.BlockSpec((r_, tq, ncp), lambda b, g, q: (g, q, 0)),
                  pl.BlockSpec((r_, nd, tk, tk), lambda b, g, q: (g, 0, 0, 0)),
                  pl.BlockSpec((r_, nw, tk, tk), lambda b, g, q: (g, 0, 0, 0)),
                  pl.BlockSpec((nsel, ncp), lambda b, g, q: (0, 0)),
                  pl.BlockSpec((s_, LANES), lambda b, g, q: (0, 0)),
                  pl.BlockSpec((1, NSA_DH), lambda b, g, q: (0, 0)),
                  pl.BlockSpec((3, NSA_DH), lambda b, g, q: (0, 0))],
        out_specs=pl.BlockSpec((r_, tq, LANES), lambda b, g, q: (g, b * nq + q, 0)),
        out_shape=jax.ShapeDtypeStruct((NSA_HEADS, b_ * s_, LANES), F32),
        scratch_shapes=[pltpu.VMEM((s_, 2 * NSA_DH), BF), pltpu.VMEM((s_, 2 * NSA_DH), BF),
                        pltpu.VMEM((s_, NSA_DH), BF), pltpu.VMEM((s_, 2 * NSA_DH), BF),
                        pltpu.VMEM((r_ * tq, 2 * NSA_DH), BF),
                        pltpu.VMEM((r_ * tq, LANES), F32), pltpu.VMEM((r_ * tq, 2 * NSA_DH), F32),
                        pltpu.VMEM((r_ * tq, LANES), F32), pltpu.VMEM((r_ * tq, 2 * NSA_DH), F32),
                        pltpu.VMEM((r_ * tq, NSA_TK_SEL), F32), pltpu.VMEM((r_ * tq, NSA_TK_SEL), F32),
                        pltpu.VMEM((r_ * tq, NSA_TK_WIN), F32), pltpu.VMEM((r_ * tq, NSA_TK_WIN), F32)],
        compiler_params=pltpu.CompilerParams(dimension_semantics=("parallel", "arbitrary", "arbitrary")),
        name="nsa",
    )(p3, p3, ckv, ckv, p3, p3, p3, p3, tab_c, tab_s, tab_w, ovl_t, e_sel, qnw, knw)


def _merge_body(x_ref, ya_ref, yb_ref, ga_ref, gb_ref, wpa_ref, wpb_ref, wo_ref, o_ref):
    nh = ya_ref.shape[0]
    ya = jnp.concatenate([ya_ref[j] for j in range(nh)], axis=1).astype(BF)
    yb = jnp.concatenate([yb_ref[j] for j in range(nh)], axis=1).astype(BF)
    ga = _sigmoid(jnp.concatenate([ga_ref[j] for j in range(nh)], axis=1))
    gb = _sigmoid(jnp.concatenate([gb_ref[j] for j in range(nh)], axis=1))
    mixed = (ga * jnp.dot(ya, wpa_ref[...], preferred_element_type=F32)
             + gb * jnp.dot(yb, wpb_ref[...], preferred_element_type=F32))
    o_ref[...] = x_ref[...] + jnp.dot(mixed.astype(BF), wo_ref[...], preferred_element_type=F32)


def _merge(x2, ya, yb, p3, wpa, wpb, wo, tm):
    t, d = x2.shape
    nh = d // LANES
    hspec = pl.BlockSpec((nh, tm, LANES), lambda i: (0, i, 0))
    wspec = pl.BlockSpec((d, d), lambda i: (0, 0))
    return pl.pallas_call(
        _merge_body,
        grid=(t // tm,),
        in_specs=[pl.BlockSpec((tm, d), lambda i: (i, 0)), hspec, hspec,
                  pl.BlockSpec((nh, tm, LANES), lambda i: (CB_MGA // nh, i, 0)),
                  pl.BlockSpec((nh, tm, LANES), lambda i: (CB_MGB // nh, i, 0)),
                  wspec, wspec, wspec],
        out_specs=pl.BlockSpec((tm, d), lambda i: (i, 0)),
        out_shape=jax.ShapeDtypeStruct((t, d), F32),
        compiler_params=pltpu.CompilerParams(dimension_semantics=("parallel",)),
        name="merge",
    )(x2, ya, yb, p3, p3, wpa, wpb, wo)


def _ffn_body(x_ref, nw_ref, wg_ref, wu_ref, wd_ref, o_ref):
    x = x_ref[...]
    h = (x * lax.rsqrt(jnp.mean(x * x, axis=-1, keepdims=True) + NORM_EPS) * nw_ref[...]).astype(BF)
    gate = jnp.dot(h, wg_ref[...], preferred_element_type=F32)
    up = jnp.dot(h, wu_ref[...], preferred_element_type=F32)
    act = (_silu(gate) * up).astype(BF)
    o_ref[...] = x + jnp.dot(act, wd_ref[...], preferred_element_type=F32)


def _ffn(x2, norm_w, wg, wu, wd, tm):
    t, d = x2.shape
    f = wg.shape[1]
    return pl.pallas_call(
        _ffn_body,
        grid=(t // tm,),
        in_specs=[pl.BlockSpec((tm, d), lambda i: (i, 0)),
                  pl.BlockSpec((1, d), lambda i: (0, 0)),
                  pl.BlockSpec((d, f), lambda i: (0, 0)),
                  pl.BlockSpec((d, f), lambda i: (0, 0)),
                  pl.BlockSpec((f, d), lambda i: (0, 0))],
        out_specs=pl.BlockSpec((tm, d), lambda i: (i, 0)),
        out_shape=jax.ShapeDtypeStruct((t, d), F32),
        compiler_params=pltpu.CompilerParams(dimension_semantics=("parallel",)),
        name="ffn",
    )(x2, norm_w, wg, wu, wd)


def _arrange_w_in(w_in):
    o_ga = 4 * GDN_HEADS * GDN_DK
    o_gb = o_ga + GDN_HEADS
    o_nq = o_gb + GDN_HEADS
    o_nkv = o_nq + NSA_HEADS * NSA_DH
    o_ng = o_nkv + 6 * NSA_GROUPS * NSA_DH
    o_mg = o_ng + 3 * NSA_HEADS
    d = w_in.shape[0]
    small = jnp.concatenate([w_in[:, o_ga:o_nq], w_in[:, o_ng:o_mg],
                             jnp.zeros((d, 2 * LANES - (o_nq - o_ga) - (o_mg - o_ng)), w_in.dtype)], axis=1)
    return jnp.concatenate([w_in[:, :o_ga], w_in[:, o_nq:o_nkv], w_in[:, o_mg:], w_in[:, o_nkv:o_ng], small],
                           axis=1).astype(BF)


def _overlap_t(s_):
    ncp = s_ // CMP_STRIDE
    nsel = s_ // SEL_BLOCK
    cs = np.arange(ncp) * CMP_STRIDE
    ss = np.arange(nsel) * SEL_BLOCK
    ov = (cs[None, :] < ss[:, None] + SEL_BLOCK) & (cs[None, :] + CMP_BLOCK > ss[:, None])
    ov[:, ncp - 1] = False
    return jnp.asarray(ov.astype(np.float32), BF)


def _sel_expand(s_):
    assert s_ // SEL_BLOCK <= LANES
    pos = np.arange(s_)
    e = (pos[:, None] // SEL_BLOCK == np.arange(LANES)[None, :]).astype(np.float32)
    return jnp.asarray(e, BF)


def kernel(x, norm1_w, w_in, conv_w, a_log, dt_bias, gdn_norm_w, cmp_pe, cmp_w1, cmp_w2, q_norm_w, k_norm_w,
           rel_bias, w_proj_a, w_proj_b, w_out, norm2_w, w_gate, w_up, w_down):
    b_, s_, d = x.shape
    t = b_ * s_
    x2 = x.reshape(t, d)
    tab_c, tab_s, tab_w = _bias_tables(rel_bias, s_)
    ovl_t = _overlap_t(s_)
    e_sel = _sel_expand(s_)
    for l in range(norm1_w.shape[0]):
        p3 = _proj(x2, norm1_w[l][None, :], _arrange_w_in(w_in[l]), tm=min(1024, t), tn=10 * LANES)
        conv_w3 = conv_w[l].reshape(GDN_CONV, 3 * GDN_HEADS, LANES)
        alog_b = jnp.broadcast_to(a_log[l][:, None], (GDN_HEADS, LANES))
        dtb_b = jnp.broadcast_to(dt_bias[l][:, None], (GDN_HEADS, LANES))
        y_a = _gdn(p3, conv_w3, alog_b, dtb_b, gdn_norm_w[l][None, :], b_, s_)
        pe2 = cmp_pe[l].reshape(2, 2, CMP_STRIDE * NSA_DH)
        ckv = _cmp(p3, pe2, cmp_w1[l].astype(BF), cmp_w2[l].astype(BF), k_norm_w[l][0:1], b_, s_)
        y_b = _nsa(p3, ckv, tab_c, tab_s, tab_w, ovl_t, e_sel, q_norm_w[l][None, :], k_norm_w[l], b_, s_)
        x2 = _merge(x2, y_a, y_b, p3, w_proj_a[l].astype(BF), w_proj_b[l].astype(BF), w_out[l].astype(BF),
                    tm=min(512, t))
        x2 = _ffn(x2, norm2_w[l][None, :], w_gate[l].astype(BF), w_up[l].astype(BF), w_down[l].astype(BF),
                  tm=min(512, t))
    return x2.reshape(b_, s_, d)
```

```python
import functools
import math

import numpy as np
import jax
import jax.numpy as jnp
from jax import lax
from jax.experimental import pallas as pl
from jax.experimental.pallas import tpu as pltpu

F32 = jnp.float32
BF = jnp.bfloat16

LANES = 128
D_MODEL = 1024
GDN_HEADS = 8
GDN_DK = 128
GDN_DV = 128
GDN_CONV = 4
GDN_CHUNK = 64
NSA_HEADS = 8
NSA_GROUPS = 2
NSA_REP = NSA_HEADS // NSA_GROUPS
NSA_DH = 128
CMP_BLOCK = 32
CMP_STRIDE = 16
CMP_HIDDEN = 256
SEL_BLOCK = 64
SEL_TOPN = 16
WINDOW = 512
FORCE_BONUS = 1000.0
REL_BUCKETS = 32
REL_MAX_DIST = 1024
FFN_HIDDEN = 2816
NORM_EPS = 1e-6
NEG = -1e30
M_INIT = -3e38
LOG2E = 1.4426950408889634

CB_GQ, CB_GK, CB_GV, CB_GZ = 0, 8, 16, 24
CB_NQ = 32
CB_MGA, CB_MGB = 40, 48
CB_KC, CB_VC, CB_KS, CB_VS, CB_KW, CB_VW = 56, 58, 60, 62, 64, 66
CB_SMALL = 68
N_CB = 70
SM_A, SM_B, SM_GATE = 0, 8, 16

GDN_ROWS = 256
GDN_GROUP = 8
NSA_TQ = 256
NSA_TB = 128
NSA_TK_SEL = 512
NSA_TK_WIN = 256
NSA_WIN_TILES = (WINDOW + max(NSA_TQ, NSA_TK_WIN) - 2) // NSA_TK_WIN + 1
NSA_PAD = NSA_TK_SEL // NSA_TB - 1


def _mm(a, b):
    return jnp.dot(a.astype(BF), b.astype(BF), preferred_element_type=F32)


def _mm_nt(a, b):
    return lax.dot_general(a.astype(BF), b.astype(BF), (((1,), (1,)), ((), ())),
                           preferred_element_type=F32)


def _mm_tn(a, b):
    return lax.dot_general(a.astype(BF), b.astype(BF), (((0,), (0,)), ((), ())),
                           preferred_element_type=F32)


def _split3(x):
    x1 = x.astype(BF)
    r1 = x - x1.astype(F32)
    x2 = r1.astype(BF)
    x3 = (r1 - x2.astype(F32)).astype(BF)
    return x1, x2, x3


def _sigmoid(x):
    return 0.5 * jnp.tanh(0.5 * x) + 0.5


def _silu(x):
    return x * _sigmoid(x)


def _softplus(x):
    return jnp.maximum(x, 0.0) + jnp.log1p(jnp.exp(-jnp.abs(x)))


def _rel_thresholds():
    d = np.arange(0, 4 * REL_MAX_DIST, dtype=np.int64)
    max_exact = REL_BUCKETS // 2
    d_f = np.maximum(d, 1).astype(np.float32)
    large = max_exact + (np.log(d_f / np.float32(max_exact)) / np.float32(math.log(REL_MAX_DIST / max_exact))
                         * np.float32(REL_BUCKETS - max_exact)).astype(np.int32)
    large = np.minimum(large, REL_BUCKETS - 1)
    bucket = np.where(d < max_exact, d, large)
    assert np.all(np.diff(bucket) >= 0)
    return [int(np.argmax(bucket >= k)) for k in range(REL_BUCKETS)]


REL_THR = _rel_thresholds()


def _proj_body(x_ref, nw_ref, w_ref, o_ref, sm_ref, h_ref):
    @pl.when(pl.program_id(1) == 0)
    def _():
        x = x_ref[...]
        y = x * lax.rsqrt(jnp.mean(x * x, axis=-1, keepdims=True) + NORM_EPS)
        h_ref[...] = (y * nw_ref[...]).astype(BF)

    nb = o_ref.shape[0]
    r = jnp.dot(h_ref[...], w_ref[...], preferred_element_type=F32)
    for j in range(nb):
        o_ref[j] = r[:, j * LANES:(j + 1) * LANES].astype(BF)

    @pl.when(pl.program_id(1) == CB_SMALL // nb)
    def _():
        sm_ref[0] = r[:, (CB_SMALL % nb) * LANES:(CB_SMALL % nb + 1) * LANES]


def _proj(x2, norm_w, w_all, tm, tn):
    t, d = x2.shape
    n = w_all.shape[1]
    nb = tn // LANES
    return pl.pallas_call(
        _proj_body,
        grid=(t // tm, n // tn),
        in_specs=[pl.BlockSpec((tm, d), lambda i, j: (i, 0)),
                  pl.BlockSpec((1, d), lambda i, j: (0, 0)),
                  pl.BlockSpec((d, tn), lambda i, j: (0, j))],
        out_specs=[pl.BlockSpec((nb, tm, LANES), lambda i, j: (j, i, 0)),
                   pl.BlockSpec((1, tm, LANES), lambda i, j: (0, i, 0))],
        out_shape=[jax.ShapeDtypeStruct((n // LANES, t, LANES), BF),
                   jax.ShapeDtypeStruct((1, t, LANES), F32)],
        scratch_shapes=[pltpu.VMEM((tm, d), BF)],
        compiler_params=pltpu.CompilerParams(dimension_semantics=("parallel", "arbitrary")),
        name="proj",
    )(x2, norm_w, w_all)


def _gdn_body(q_ref, k_ref, v_ref, z_ref, sm_ref, cw_ref, alog_ref, dtb_ref, nw_ref, o_ref,
              ext_ref, st_ref, gb_ref, bb_ref):
    rows = GDN_ROWS
    c = GDN_CHUNK
    nchunk = rows // c
    s = pl.program_id(1)

    @pl.when(s == 0)
    def _():
        ext_ref[:, 0:8, :] = jnp.zeros((3 * GDN_HEADS, 8, LANES), F32)
        st_ref[...] = jnp.zeros_like(st_ref)

    for j in range(GDN_HEADS):
        ext_ref[j, 8:8 + rows, :] = q_ref[j].astype(F32)
        ext_ref[GDN_HEADS + j, 8:8 + rows, :] = k_ref[j].astype(F32)
        ext_ref[2 * GDN_HEADS + j, 8:8 + rows, :] = v_ref[j].astype(F32)

    sm = sm_ref[0]
    for j in range(GDN_HEADS):
        gb_ref[j] = jnp.broadcast_to(sm[:, SM_A + j:SM_A + j + 1], (rows, LANES))
        bb_ref[j] = jnp.broadcast_to(sm[:, SM_B + j:SM_B + j + 1], (rows, LANES))

    def conv(j):
        acc = cw_ref[0, pl.ds(j, 1), :] * ext_ref[j, pl.ds(5, rows), :]
        for i in range(1, GDN_CONV):
            acc = acc + cw_ref[i, pl.ds(j, 1), :] * ext_ref[j, pl.ds(5 + i, rows), :]
        return acc

    def head_setup(h, tril, strict, l_tril):
        qh = _silu(conv(h))
        kh = _silu(conv(GDN_HEADS + h))
        vv = _silu(conv(2 * GDN_HEADS + h))
        qn = qh * lax.rsqrt(jnp.sum(qh * qh, axis=-1, keepdims=True) + NORM_EPS) * (GDN_DK ** -0.5)
        kn = kh * lax.rsqrt(jnp.sum(kh * kh, axis=-1, keepdims=True) + NORM_EPS)

        g = -jnp.exp(alog_ref[pl.ds(h, 1), :]) * _softplus(gb_ref[h] + dtb_ref[pl.ds(h, 1), :])
        beta = _sigmoid(bb_ref[h])
        g1, g2, g3 = _split3(g)
        gcum = (jnp.dot(l_tril, g1, preferred_element_type=F32)
                + jnp.dot(l_tril, g2, preferred_element_type=F32)
                + jnp.dot(l_tril, g3, preferred_element_type=F32))
        glast = jnp.concatenate(
            [jnp.broadcast_to(gcum[(n + 1) * c - 1:(n + 1) * c, :], (c, LANES)) for n in range(nchunk)],
            axis=0)
        gc2 = jnp.concatenate([gcum, gcum], axis=1)
        gct = gcum.T
        gr = jnp.concatenate([gct, gct], axis=0)
        decay = jnp.where(tril, jnp.exp(jnp.where(tril, gc2 - gr, 0.0)), 0.0)

        kb = kn * beta
        eg = jnp.exp(gcum)
        knb = kn.astype(BF)
        return dict(
            a=jnp.where(strict, _mm_nt(kb, knb) * decay, 0.0),
            intra=_mm_nt(qn, knb) * decay,
            rhs=jnp.concatenate([vv * beta, kb * eg], axis=1),
            qg=qn * eg, kdec=kn * jnp.exp(glast - gcum), cd=jnp.exp(glast))

    def head_group(hg, carry):
        hs = [GDN_GROUP * hg + u for u in range(GDN_GROUP)]
        ri = lax.broadcasted_iota(jnp.int32, (rows, rows), 0)
        ci = lax.broadcasted_iota(jnp.int32, (rows, rows), 1)
        same = (ri // c) == (ci // c)
        tril = same & (ri >= ci)
        strict = same & (ri > ci)
        l_tril = jnp.where(tril, 1.0, 0.0).astype(BF)
        eye = jnp.where(ri == ci, 1.0, 0.0)
        w = [head_setup(h, tril, strict, l_tril) for h in hs]
        p = [_mm(d["a"], d["a"]) for d in w]
        t = [eye - d["a"] for d in w]
        for j in range(1, 6):
            tp = [_mm(ti, pi) for ti, pi in zip(t, p)]
            if j < 5:
                p = [_mm(pi, pi) for pi in p]
            t = [ti + tpi for ti, tpi in zip(t, tp)]
        sol = [d["rhs"] + _mm(ti - eye, d["rhs"]) for d, ti in zip(w, t)]
        st = [st_ref[h] for h in hs]
        outs = [[] for _ in hs]
        for n in range(nchunk):
            sl = slice(n * c, (n + 1) * c)
            ks = [_mm(jnp.concatenate([s_[sl, GDN_DV:], d["qg"][sl]], axis=0), si)
                  for s_, d, si in zip(sol, w, st)]
            vn = [s_[sl, :GDN_DV] - k_[:c] for s_, k_ in zip(sol, ks)]
            for u, (d, k_, v_) in enumerate(zip(w, ks, vn)):
                outs[u].append(k_[c:] + _mm(d["intra"][sl, n * c:(n + 1) * c], v_))
            st = [si * jnp.concatenate([d["cd"][sl], d["cd"][sl]], axis=0) + _mm_tn(d["kdec"][sl], v_)
                  for si, d, v_ in zip(st, w, vn)]
        for u, h in enumerate(hs):
            st_ref[h] = st[u]
            o = jnp.concatenate(outs[u], axis=0)
            on = o * lax.rsqrt(jnp.mean(o * o, axis=-1, keepdims=True) + NORM_EPS) * nw_ref[...]
            o_ref[h] = (on * _silu(z_ref[h].astype(F32))).astype(o_ref.dtype)
        return carry

    lax.fori_loop(0, GDN_HEADS // GDN_GROUP, head_group, 0)

    for j in range(3 * GDN_HEADS):
        ext_ref[j, 0:8, :] = ext_ref[j, rows:rows + 8, :]


def _gdn(p3, sm, conv_w3, alog_b, dtb_b, gdn_norm_w, b_, s_):
    rows = GDN_ROWS
    ns = s_ // rows
    hb = GDN_HEADS

    def cb(base):
        return pl.BlockSpec((hb, rows, LANES), lambda b, s, base=base: (base // hb, b * ns + s, 0))

    return pl.pallas_call(
        _gdn_body,
        grid=(b_, ns),
        in_specs=[cb(CB_GQ), cb(CB_GK), cb(CB_GV), cb(CB_GZ),
                  pl.BlockSpec((1, rows, LANES), lambda b, s: (0, b * ns + s, 0)),
                  pl.BlockSpec((GDN_CONV, 3 * hb, LANES), lambda b, s: (0, 0, 0)),
                  pl.BlockSpec((hb, LANES), lambda b, s: (0, 0)),
                  pl.BlockSpec((hb, LANES), lambda b, s: (0, 0)),
                  pl.BlockSpec((1, LANES), lambda b, s: (0, 0))],
        out_specs=pl.BlockSpec((hb, rows, LANES), lambda b, s: (0, b * ns + s, 0)),
        out_shape=jax.ShapeDtypeStruct((hb, b_ * s_, LANES), BF),
        scratch_shapes=[pltpu.VMEM((3 * hb, rows + 8, LANES), F32),
                        pltpu.VMEM((hb, GDN_DK, GDN_DV), F32),
                        pltpu.VMEM((hb, rows, LANES), F32),
                        pltpu.VMEM((hb, rows, LANES), F32)],
        compiler_params=pltpu.CompilerParams(dimension_semantics=("parallel", "arbitrary")),
        name="gdn",
    )(p3, p3, p3, p3, sm, conv_w3, alog_b, dtb_b, gdn_norm_w)


def _cmp_body(x_ref, pe_ref, w1_ref, w2_ref, nw_ref, o_ref, c_ref, xf_ref):
    kv = pl.program_id(1)
    nch = c_ref.shape[0]
    half = CMP_STRIDE * NSA_DH
    xf_ref[...] = x_ref[0].astype(F32)
    for p in range(CMP_STRIDE):
        c_ref[:, p * NSA_DH:(p + 1) * NSA_DH] = xf_ref[pl.ds(p, nch, stride=CMP_STRIDE), :]
    cc = c_ref[...]
    u = _mm(cc + pe_ref[0, 0:1, :], w1_ref[0, 0:half, :])
    v = _mm(cc + pe_ref[0, 1:2, :], w1_ref[0, half:2 * half, :])
    v_next = jnp.concatenate([v[1:], v[:1]], axis=0)
    hid = _silu(u + v_next)
    out = _mm(hid, w2_ref[0])
    normed = out * lax.rsqrt(jnp.mean(out * out, axis=-1, keepdims=True) + NORM_EPS) * nw_ref[...]
    o_ref[0, 0, 0] = jnp.where(kv == 0, normed, out)


def _cmp(p3, pe2, w1, w2, knw0, b_, s_):
    nch = s_ // CMP_STRIDE
    g_ = NSA_GROUPS
    return pl.pallas_call(
        _cmp_body,
        grid=(b_, 2, g_),
        in_specs=[pl.BlockSpec((1, s_, LANES), lambda b, kv, g: (CB_KC + 2 * kv + g, b, 0)),
                  pl.BlockSpec((1, 2, CMP_STRIDE * NSA_DH), lambda b, kv, g: (kv, 0, 0)),
                  pl.BlockSpec((1, CMP_BLOCK * NSA_DH, CMP_HIDDEN), lambda b, kv, g: (kv, 0, 0)),
                  pl.BlockSpec((1, CMP_HIDDEN, NSA_DH), lambda b, kv, g: (kv, 0, 0)),
                  pl.BlockSpec((1, NSA_DH), lambda b, kv, g: (0, 0))],
        out_specs=pl.BlockSpec((1, 1, 1, nch, NSA_DH), lambda b, kv, g: (b, kv, g, 0, 0)),
        out_shape=jax.ShapeDtypeStruct((b_, 2, g_, nch, NSA_DH), F32),
        scratch_shapes=[pltpu.VMEM((nch, CMP_STRIDE * NSA_DH), F32), pltpu.VMEM((s_, LANES), F32)],
        compiler_params=pltpu.CompilerParams(dimension_semantics=("parallel", "arbitrary", "arbitrary")),
        name="cmp",
    )(p3, pe2, w1, w2, knw0)


def _bias_of(d, rb_ref, h):
    val = jnp.full(d.shape, rb_ref[0, h], F32)
    for k in range(1, REL_BUCKETS):
        val = jnp.where(d >= REL_THR[k], rb_ref[k, h], val)
    return val * LOG2E


def _bias_body(rb_ref, tc_ref, ts_ref, tw_ref):
    h = pl.program_id(0)
    _, s_, ncp = tc_ref.shape

    def row_tile(it, carry):
        r0 = pl.multiple_of(it * NSA_TQ, NSA_TQ)
        t = r0 + lax.broadcasted_iota(jnp.int32, (NSA_TQ, ncp), 0)
        n = lax.broadcasted_iota(jnp.int32, (NSA_TQ, ncp), 1)
        d = t - (n * CMP_STRIDE + CMP_BLOCK - 1)
        tc_ref[0, pl.ds(r0, NSA_TQ), :] = jnp.where(d >= 0, _bias_of(d, rb_ref, h), NEG)
        return carry

    lax.fori_loop(0, s_ // NSA_TQ, row_tile, 0)
    i = lax.broadcasted_iota(jnp.int32, (NSA_TB, NSA_TB), 0)
    j = lax.broadcasted_iota(jnp.int32, (NSA_TB, NSA_TB), 1)
    for e in range(ts_ref.shape[1]):
        d = (e - NSA_PAD) * NSA_TB + i - j
        ts_ref[0, e] = jnp.where(d >= 0, _bias_of(d, rb_ref, h), NEG)
    for e in range(tw_ref.shape[1]):
        d = (e - NSA_PAD) * NSA_TB + i - j
        tw_ref[0, e] = jnp.where((d >= 0) & (d < WINDOW), _bias_of(d, rb_ref, h), NEG)


def _sel_table_len():
    a = 0
    while a * NSA_TB - (NSA_TB - 1) < REL_THR[REL_BUCKETS - 1]:
        a += 1
    return a + 1 + NSA_PAD


def _win_table_len():
    return (WINDOW + NSA_TB - 1) // NSA_TB + 2 + NSA_PAD


def _bias_tables(rel_bias, s_):
    ncp = s_ // CMP_STRIDE
    nd = _sel_table_len()
    nw = _win_table_len()
    return pl.pallas_call(
        _bias_body,
        grid=(NSA_HEADS,),
        in_specs=[pl.BlockSpec(memory_space=pltpu.SMEM)],
        out_specs=[pl.BlockSpec((1, s_, ncp), lambda h: (h, 0, 0)),
                   pl.BlockSpec((1, nd, NSA_TB, NSA_TB), lambda h: (h, 0, 0, 0)),
                   pl.BlockSpec((1, nw, NSA_TB, NSA_TB), lambda h: (h, 0, 0, 0))],
        out_shape=[jax.ShapeDtypeStruct((NSA_HEADS, s_, ncp), F32),
                   jax.ShapeDtypeStruct((NSA_HEADS, nd, NSA_TB, NSA_TB), F32),
                   jax.ShapeDtypeStruct((NSA_HEADS, nw, NSA_TB, NSA_TB), F32)],
        compiler_params=pltpu.CompilerParams(dimension_semantics=("parallel",)),
        name="bias",
    )(rel_bias)


def _rms_rows(x, w):
    return x * lax.rsqrt(jnp.mean(x * x, axis=-1, keepdims=True) + NORM_EPS) * w


def _nsa_body(q_ref, sm_ref, kc_ref, vc_ref, ks_ref, vs_ref, kw_ref, vw_ref, tc_ref, ts_ref, tw_ref,
              ovl_ref, e_ref, qnw_ref, knw_ref, o_ref,
              ksb, vsb, kwb, vwb, qa_ref, ms_ref, accs_ref, mw_ref, accw_ref, sa_ref, sb_ref, wa_ref, wb_ref):
    tq, r_, tb = NSA_TQ, NSA_REP, NSA_TB
    g = pl.program_id(1)
    qi = pl.program_id(2)
    nsel = ovl_ref.shape[0]
    topn = min(SEL_TOPN, nsel)

    @pl.when(qi == 0)
    def _():
        ones = jnp.ones(vs_ref.shape[1:], BF)
        ksb[:, :NSA_DH] = _rms_rows(ks_ref[0].astype(F32), knw_ref[1:2, :]).astype(BF)
        ksb[:, NSA_DH:] = e_ref[...]
        kwb[...] = _rms_rows(kw_ref[0].astype(F32), knw_ref[2:3, :]).astype(BF)
        vsb[:, :NSA_DH] = vs_ref[0]
        vsb[:, NSA_DH:] = ones
        vwb[:, :NSA_DH] = vw_ref[0]
        vwb[:, NSA_DH:] = ones

    qscale = NSA_DH ** -0.5 * LOG2E
    for r in range(r_):
        qa_ref[r * tq:(r + 1) * tq, :NSA_DH] = (
            _rms_rows(q_ref[r].astype(F32), qnw_ref[...]) * qscale).astype(BF)
    qs = qa_ref[:, :NSA_DH]

    def flash_init(m_ref, acc_ref):
        m_ref[...] = jnp.full(m_ref.shape, M_INIT, F32)
        acc_ref[...] = jnp.zeros_like(acc_ref)

    def logits(kj, q, k_sc, tk, dst_ref):
        off = pl.multiple_of(jnp.maximum(kj, 0) * tk, tk)
        dst_ref[:, :tk] = _mm_nt(q, k_sc[pl.ds(off, tk), :])

    def flash_step(kj, s_ref, v_sc, tab_ref, tk, m_ref, acc_ref):
        nct = tk // tb
        nrt = tq // tb
        off = pl.multiple_of(jnp.maximum(kj, 0) * tk, tk)
        e0 = qi * nrt - kj * nct + NSA_PAD
        idx = {o: jnp.where(kj >= 0, jnp.clip(e0 + o, 0, tab_ref.shape[1] - 1), 0)
               for o in range(-(nct - 1), nrt)}
        m_prev = m_ref[...]
        m_rows, p_rows = [], []
        for rb in range(r_ * nrt):
            r, rho = divmod(rb, nrt)
            rs = slice(rb * tb, (rb + 1) * tb)
            pieces = [s_ref[rs, c * tb:(c + 1) * tb] + tab_ref[r, idx[rho - c]] for c in range(nct)]
            mx = pieces[0]
            for c in range(1, nct):
                mx = jnp.maximum(mx, pieces[c])
            m_next = jnp.maximum(m_prev[rs], jnp.max(mx, axis=-1, keepdims=True))
            m_rows.append(m_next)
            p_rows.append(jnp.concatenate([jnp.exp2(pc_ - m_next).astype(BF) for pc_ in pieces], axis=1))
        m_next = jnp.concatenate(m_rows, axis=0)
        p = jnp.concatenate(p_rows, axis=0)
        alpha = jnp.exp2(m_prev - m_next)
        acc_ref[...] = (jnp.concatenate([alpha, alpha], axis=1) * acc_ref[...]
                        + jnp.dot(p, v_sc[pl.ds(off, tk), :], preferred_element_type=F32))
        m_ref[...] = m_next

    def flash_result(acc_ref):
        acc = acc_ref[...]
        return acc[:, :NSA_DH] / jnp.maximum(acc[:, NSA_DH:], 1e-30)

    wlast = (qi * tq + tq - 1) // NSA_TK_WIN
    wtiles = [wlast - (NSA_WIN_TILES - 1) + u for u in range(NSA_WIN_TILES)]
    wbufs = (wa_ref, wb_ref)

    def win_step(u):
        flash_step(wtiles[u], wbufs[u % 2], vwb, tw_ref, NSA_TK_WIN, mw_ref, accw_ref)

    flash_init(mw_ref, accw_ref)
    logits(wtiles[0], qs, kwb, NSA_TK_WIN, wbufs[0])
    logits(wtiles[1], qs, kwb, NSA_TK_WIN, wbufs[1])

    lc = _mm_nt(qs, kc_ref[0, 0, 0]) + jnp.concatenate([tc_ref[r] for r in range(r_)], axis=0)
    mc = jnp.max(lc, axis=-1, keepdims=True)
    pc = jnp.exp2(lc - mc)
    lsum = jnp.sum(pc, axis=-1, keepdims=True)
    pc = pc * jnp.where(mc > 0.5 * NEG, 1.0 / jnp.maximum(lsum, 1e-30), 0.0)
    oc = _mm(pc, vc_ref[0, 0, 0])

    win_step(0)
    for u in range(2, NSA_WIN_TILES):
        logits(wtiles[u], qs, kwb, NSA_TK_WIN, wbufs[u % 2])
        win_step(u - 1)

    psum = pc[0:tq]
    for r in range(1, r_):
        psum = psum + pc[r * tq:(r + 1) * tq]
    s1, s2, s3 = _split3(psum)
    dn = (((1,), (1,)), ((), ()))
    ovl = ovl_ref[...]
    imp_t = (lax.dot_general(ovl, s1, dn, preferred_element_type=F32)
             + lax.dot_general(ovl, s2, dn, preferred_element_type=F32)
             + lax.dot_general(ovl, s3, dn, preferred_element_type=F32))
    jb = lax.broadcasted_iota(jnp.int32, (nsel, tq), 0)
    tpos = qi * tq + lax.broadcasted_iota(jnp.int32, (nsel, tq), 1)
    tblk = tpos // SEL_BLOCK
    forced = (jb == 0) | (jb == tblk) | (jb == tblk - 1)
    score = jnp.where(jb <= tblk, imp_t + jnp.where(forced, FORCE_BONUS, 0.0), NEG)
    unsel_t = jnp.full((nsel, tq), NEG, F32)
    for _ in range(topn):
        mx = jnp.max(score, axis=0, keepdims=True)
        idx = jnp.min(jnp.where(score == mx, jb, nsel), axis=0, keepdims=True)
        hit = jb == idx
        unsel_t = jnp.where(hit, 0.0, unsel_t)
        score = jnp.where(hit, M_INIT, score)
    unsel_pad = jnp.concatenate([unsel_t, jnp.zeros((LANES - nsel, tq), F32)], axis=0).astype(BF)
    ri = lax.broadcasted_iota(jnp.int32, (tq, tq), 0)
    ci = lax.broadcasted_iota(jnp.int32, (tq, tq), 1)
    eye = jnp.where(ri == ci, 1.0, 0.0).astype(BF)
    unsel_q = lax.dot_general(eye, unsel_pad, dn, preferred_element_type=F32).astype(BF)
    for r in range(r_):
        qa_ref[r * tq:(r + 1) * tq, NSA_DH:] = unsel_q

    win_step(NSA_WIN_TILES - 1)

    qa = qa_ref[...]
    slast = (qi * tq + tq - 1) // NSA_TK_SEL
    flash_init(ms_ref, accs_ref)
    logits(0, qa, ksb, NSA_TK_SEL, sa_ref)

    def sel_step(kj, s_ref):
        flash_step(kj, s_ref, vsb, ts_ref, NSA_TK_SEL, ms_ref, accs_ref)

    def pair(i, carry):
        kj = 2 * i
        logits(kj + 1, qa, ksb, NSA_TK_SEL, sb_ref)
        sel_step(kj, sa_ref)
        logits(jnp.minimum(kj + 2, slast), qa, ksb, NSA_TK_SEL, sa_ref)
        sel_step(kj + 1, sb_ref)
        return carry

    lax.fori_loop(0, (slast + 1) // 2, pair, 0)

    @pl.when(slast % 2 == 0)
    def _():
        sel_step(slast, sa_ref)

    o_s = flash_result(accs_ref)
    o_w = flash_result(accw_ref)

    sg = _sigmoid(sm_ref[0])
    for r in range(r_):
        acc = None
        for br, ob in enumerate((oc, o_s, o_w)):
            c0 = SM_GATE + r * 3 + br
            c1 = SM_GATE + (r_ + r) * 3 + br
            gate = jnp.where(g == 0, sg[:, c0:c0 + 1], sg[:, c1:c1 + 1])
            term = gate * ob[r * tq:(r + 1) * tq]
            acc = term if acc is None else acc + term
        o_ref[r] = acc.astype(o_ref.dtype)


def _nsa(p3, sm, ckv, tab_c, tab_s, tab_w, ovl_t, e_sel, qnw, knw, b_, s_):
    tq, tk, r_ = NSA_TQ, NSA_TB, NSA_REP
    nq = s_ // tq
    ncp = s_ // CMP_STRIDE
    nsel = s_ // SEL_BLOCK
    nd = tab_s.shape[1]
    nw = tab_w.shape[1]
    assert s_ % NSA_TK_SEL == 0 and NSA_TK_SEL // NSA_TB - 1 <= NSA_PAD
    assert NSA_TQ % NSA_TB == 0 and max(NSA_TQ, NSA_TK_WIN) % min(NSA_TQ, NSA_TK_WIN) == 0

    def kv_spec(base):
        return pl.BlockSpec((1, s_, LANES), lambda b, g, q, base=base: (base + g, b, 0))

    return pl.pallas_call(
        _nsa_body,
        grid=(b_, NSA_GROUPS, nq),
        in_specs=[pl.BlockSpec((r_, tq, LANES), lambda b, g, q: (CB_NQ // r_ + g, b * nq + q, 0)),
                  pl.BlockSpec((1, tq, LANES), lambda b, g, q: (0, b * nq + q, 0)),
                  pl.BlockSpec((1, 1, 1, ncp, NSA_DH), lambda b, g, q: (b, 0, g, 0, 0)),
                  pl.BlockSpec((1, 1, 1, ncp, NSA_DH), lambda b, g, q: (b, 1, g, 0, 0)),
                  kv_spec(CB_KS), kv_spec(CB_VS), kv_spec(CB_KW), kv_spec(CB_VW),
                  pl.BlockSpec((r_, tq, ncp), lambda b, g, q: (g, q, 0)),
                  pl.BlockSpec((r_, nd, tk, tk), lambda b, g, q: (g, 0, 0, 0)),
                  pl.BlockSpec((r_, nw, tk, tk), lambda b, g, q: (g, 0, 0, 0)),
                  pl.BlockSpec((nsel, ncp), lambda b, g, q: (0, 0)),
                  pl.BlockSpec((s_, LANES), lambda b, g, q: (0, 0)),
                  pl.BlockSpec((1, NSA_DH), lambda b, g, q: (0, 0)),
                  pl.BlockSpec((3, NSA_DH), lambda b, g, q: (0, 0))],
        out_specs=pl.BlockSpec((r_, tq, LANES), lambda b, g, q: (g, b * nq + q, 0)),
        out_shape=jax.ShapeDtypeStruct((NSA_HEADS, b_ * s_, LANES), BF),
        scratch_shapes=[pltpu.VMEM((s_, 2 * NSA_DH), BF), pltpu.VMEM((s_, 2 * NSA_DH), BF),
                        pltpu.VMEM((s_, NSA_DH), BF), pltpu.VMEM((s_, 2 * NSA_DH), BF),
                        pltpu.VMEM((r_ * tq, 2 * NSA_DH), BF),
                        pltpu.VMEM((r_ * tq, LANES), F32), pltpu.VMEM((r_ * tq, 2 * NSA_DH), F32),
                        pltpu.VMEM((r_ * tq, LANES), F32), pltpu.VMEM((r_ * tq, 2 * NSA_DH), F32),
                        pltpu.VMEM((r_ * tq, NSA_TK_SEL), F32), pltpu.VMEM((r_ * tq, NSA_TK_SEL), F32),
                        pltpu.VMEM((r_ * tq, NSA_TK_WIN), F32), pltpu.VMEM((r_ * tq, NSA_TK_WIN), F32)],
        compiler_params=pltpu.CompilerParams(dimension_semantics=("parallel", "arbitrary", "arbitrary")),
        name="nsa",
    )(p3, sm, ckv, ckv, p3, p3, p3, p3, tab_c, tab_s, tab_w, ovl_t, e_sel, qnw, knw)


def _merge_body(x_ref, ya_ref, yb_ref, ga_ref, gb_ref, wpa_ref, wpb_ref, wo_ref, o_ref):
    nh = ya_ref.shape[0]
    ya = jnp.concatenate([ya_ref[j] for j in range(nh)], axis=1)
    yb = jnp.concatenate([yb_ref[j] for j in range(nh)], axis=1)
    ga = _sigmoid(jnp.concatenate([ga_ref[j].astype(F32) for j in range(nh)], axis=1))
    gb = _sigmoid(jnp.concatenate([gb_ref[j].astype(F32) for j in range(nh)], axis=1))
    mixed = (ga * jnp.dot(ya, wpa_ref[...], preferred_element_type=F32)
             + gb * jnp.dot(yb, wpb_ref[...], preferred_element_type=F32))
    o_ref[...] = x_ref[...] + jnp.dot(mixed.astype(BF), wo_ref[...], preferred_element_type=F32)


def _merge(x2, ya, yb, p3, wpa, wpb, wo, tm):
    t, d = x2.shape
    nh = d // LANES
    hspec = pl.BlockSpec((nh, tm, LANES), lambda i: (0, i, 0))
    wspec = pl.BlockSpec((d, d), lambda i: (0, 0))
    return pl.pallas_call(
        _merge_body,
        grid=(t // tm,),
        in_specs=[pl.BlockSpec((tm, d), lambda i: (i, 0)), hspec, hspec,
                  pl.BlockSpec((nh, tm, LANES), lambda i: (CB_MGA // nh, i, 0)),
                  pl.BlockSpec((nh, tm, LANES), lambda i: (CB_MGB // nh, i, 0)),
                  wspec, wspec, wspec],
        out_specs=pl.BlockSpec((tm, d), lambda i: (i, 0)),
        out_shape=jax.ShapeDtypeStruct((t, d), F32),
        compiler_params=pltpu.CompilerParams(dimension_semantics=("parallel",)),
        name="merge",
    )(x2, ya, yb, p3, p3, wpa, wpb, wo)


def _ffn_body(x_ref, nw_ref, wg_ref, wu_ref, wd_ref, o_ref):
    x = x_ref[...]
    h = (x * lax.rsqrt(jnp.mean(x * x, axis=-1, keepdims=True) + NORM_EPS) * nw_ref[...]).astype(BF)
    gate = jnp.dot(h, wg_ref[...], preferred_element_type=F32)
    up = jnp.dot(h, wu_ref[...], preferred_element_type=F32)
    act = (_silu(gate) * up).astype(BF)
    o_ref[...] = x + jnp.dot(act, wd_ref[...], preferred_element_type=F32)


def _ffn(x2, norm_w, wg, wu, wd, tm):
    t, d = x2.shape
    f = wg.shape[1]
    return pl.pallas_call(
        _ffn_body,
        grid=(t // tm,),
        in_specs=[pl.BlockSpec((tm, d), lambda i: (i, 0)),
                  pl.BlockSpec((1, d), lambda i: (0, 0)),
                  pl.BlockSpec((d, f), lambda i: (0, 0)),
                  pl.BlockSpec((d, f), lambda i: (0, 0)),
                  pl.BlockSpec((f, d), lambda i: (0, 0))],
        out_specs=pl.BlockSpec((tm, d), lambda i: (i, 0)),
        out_shape=jax.ShapeDtypeStruct((t, d), F32),
        compiler_params=pltpu.CompilerParams(dimension_semantics=("parallel",)),
        name="ffn",
    )(x2, norm_w, wg, wu, wd)


def _arrange_w_in(w_in):
    o_ga = 4 * GDN_HEADS * GDN_DK
    o_gb = o_ga + GDN_HEADS
    o_nq = o_gb + GDN_HEADS
    o_nkv = o_nq + NSA_HEADS * NSA_DH
    o_ng = o_nkv + 6 * NSA_GROUPS * NSA_DH
    o_mg = o_ng + 3 * NSA_HEADS
    d = w_in.shape[0]
    small = jnp.concatenate([w_in[:, o_ga:o_nq], w_in[:, o_ng:o_mg],
                             jnp.zeros((d, 2 * LANES - (o_nq - o_ga) - (o_mg - o_ng)), w_in.dtype)], axis=1)
    return jnp.concatenate([w_in[:, :o_ga], w_in[:, o_nq:o_nkv], w_in[:, o_mg:], w_in[:, o_nkv:o_ng], small],
                           axis=1).astype(BF)


def _overlap_t(s_):
    ncp = s_ // CMP_STRIDE
    nsel = s_ // SEL_BLOCK
    cs = np.arange(ncp) * CMP_STRIDE
    ss = np.arange(nsel) * SEL_BLOCK
    ov = (cs[None, :] < ss[:, None] + SEL_BLOCK) & (cs[None, :] + CMP_BLOCK > ss[:, None])
    ov[:, ncp - 1] = False
    return jnp.asarray(ov.astype(np.float32), BF)


def _sel_expand(s_):
    assert s_ // SEL_BLOCK <= LANES
    pos = np.arange(s_)
    e = (pos[:, None] // SEL_BLOCK == np.arange(LANES)[None, :]).astype(np.float32)
    return jnp.asarray(e, BF)


def kernel(x, norm1_w, w_in, conv_w, a_log, dt_bias, gdn_norm_w, cmp_pe, cmp_w1, cmp_w2, q_norm_w, k_norm_w,
           rel_bias, w_proj_a, w_proj_b, w_out, norm2_w, w_gate, w_up, w_down):
    b_, s_, d = x.shape
    t = b_ * s_
    x2 = x.reshape(t, d)
    tab_c, tab_s, tab_w = _bias_tables(rel_bias, s_)
    ovl_t = _overlap_t(s_)
    e_sel = _sel_expand(s_)
    for l in range(norm1_w.shape[0]):
        p3, sm = _proj(x2, norm1_w[l][None, :], _arrange_w_in(w_in[l]), tm=min(1024, t), tn=10 * LANES)
        conv_w3 = conv_w[l].reshape(GDN_CONV, 3 * GDN_HEADS, LANES)
        alog_b = jnp.broadcast_to(a_log[l][:, None], (GDN_HEADS, LANES))
        dtb_b = jnp.broadcast_to(dt_bias[l][:, None], (GDN_HEADS, LANES))
        y_a = _gdn(p3, sm, conv_w3, alog_b, dtb_b, gdn_norm_w[l][None, :], b_, s_)
        pe2 = cmp_pe[l].reshape(2, 2, CMP_STRIDE * NSA_DH)
        ckv = _cmp(p3, pe2, cmp_w1[l].astype(BF), cmp_w2[l].astype(BF), k_norm_w[l][0:1], b_, s_)
        y_b = _nsa(p3, sm, ckv, tab_c, tab_s, tab_w, ovl_t, e_sel, q_norm_w[l][None, :], k_norm_w[l], b_, s_)
        x2 = _merge(x2, y_a, y_b, p3, w_proj_a[l].astype(BF), w_proj_b[l].astype(BF), w_out[l].astype(BF),
                    tm=min(512, t))
        x2 = _ffn(x2, norm2_w[l][None, :], w_gate[l].astype(BF), w_up[l].astype(BF), w_down[l].astype(BF),
                  tm=min(512, t))
    return x2.reshape(b_, s_, d)
```

```python
import functools
import math

import numpy as np
import jax
import jax.numpy as jnp
from jax import lax
from jax.experimental import pallas as pl
from jax.experimental.pallas import tpu as pltpu

F32 = jnp.float32
BF = jnp.bfloat16

LANES = 128
D_MODEL = 1024
GDN_HEADS = 8
GDN_DK = 128
GDN_DV = 128
GDN_CONV = 4
GDN_CHUNK = 64
NSA_HEADS = 8
NSA_GROUPS = 2
NSA_REP = NSA_HEADS // NSA_GROUPS
NSA_DH = 128
CMP_BLOCK = 32
CMP_STRIDE = 16
CMP_HIDDEN = 256
SEL_BLOCK = 64
SEL_TOPN = 16
WINDOW = 512
FORCE_BONUS = 1000.0
REL_BUCKETS = 32
REL_MAX_DIST = 1024
FFN_HIDDEN = 2816
NORM_EPS = 1e-6
NEG = -1e30
M_INIT = -3e38
LOG2E = 1.4426950408889634

CB_GQ, CB_GK, CB_GV, CB_GZ = 0, 8, 16, 24
CB_NQ = 32
CB_MGA, CB_MGB = 40, 48
CB_KC, CB_VC, CB_KS, CB_VS, CB_KW, CB_VW = 56, 58, 60, 62, 64, 66
CB_SMALL = 68
N_CB = 70
SM_A, SM_B, SM_GATE = 0, 8, 16

GDN_ROWS = 256
NSA_TQ = 256
NSA_TB = 128
NSA_TK_SEL = 512
NSA_TK_WIN = 256
NSA_WIN_TILES = (WINDOW + max(NSA_TQ, NSA_TK_WIN) - 2) // NSA_TK_WIN + 1
NSA_PAD = NSA_TK_SEL // NSA_TB - 1


def _mm(a, b):
    return jnp.dot(a.astype(BF), b.astype(BF), preferred_element_type=F32)


def _mm_nt(a, b):
    return lax.dot_general(a.astype(BF), b.astype(BF), (((1,), (1,)), ((), ())),
                           preferred_element_type=F32)


def _mm_tn(a, b):
    return lax.dot_general(a.astype(BF), b.astype(BF), (((0,), (0,)), ((), ())),
                           preferred_element_type=F32)


def _split3(x):
    x1 = x.astype(BF)
    r1 = x - x1.astype(F32)
    x2 = r1.astype(BF)
    x3 = (r1 - x2.astype(F32)).astype(BF)
    return x1, x2, x3


def _sigmoid(x):
    return 0.5 * jnp.tanh(0.5 * x) + 0.5


def _silu(x):
    return x * _sigmoid(x)


def _softplus(x):
    return jnp.maximum(x, 0.0) + jnp.log1p(jnp.exp(-jnp.abs(x)))


def _rel_thresholds():
    d = np.arange(0, 4 * REL_MAX_DIST, dtype=np.int64)
    max_exact = REL_BUCKETS // 2
    d_f = np.maximum(d, 1).astype(np.float32)
    large = max_exact + (np.log(d_f / np.float32(max_exact)) / np.float32(math.log(REL_MAX_DIST / max_exact))
                         * np.float32(REL_BUCKETS - max_exact)).astype(np.int32)
    large = np.minimum(large, REL_BUCKETS - 1)
    bucket = np.where(d < max_exact, d, large)
    assert np.all(np.diff(bucket) >= 0)
    return [int(np.argmax(bucket >= k)) for k in range(REL_BUCKETS)]


REL_THR = _rel_thresholds()


def _proj_body(x_ref, nw_ref, w_ref, o_ref, sm_ref, h_ref):
    @pl.when(pl.program_id(1) == 0)
    def _():
        x = x_ref[...]
        y = x * lax.rsqrt(jnp.mean(x * x, axis=-1, keepdims=True) + NORM_EPS)
        h_ref[...] = (y * nw_ref[...]).astype(BF)

    nb = o_ref.shape[0]
    r = jnp.dot(h_ref[...], w_ref[...], preferred_element_type=F32)
    for j in range(nb):
        o_ref[j] = r[:, j * LANES:(j + 1) * LANES].astype(BF)

    @pl.when(pl.program_id(1) == CB_SMALL // nb)
    def _():
        sm_ref[0] = r[:, (CB_SMALL % nb) * LANES:(CB_SMALL % nb + 1) * LANES]


def _proj(x2, norm_w, w_all, tm, tn):
    t, d = x2.shape
    n = w_all.shape[1]
    nb = tn // LANES
    return pl.pallas_call(
        _proj_body,
        grid=(t // tm, n // tn),
        in_specs=[pl.BlockSpec((tm, d), lambda i, j: (i, 0)),
                  pl.BlockSpec((1, d), lambda i, j: (0, 0)),
                  pl.BlockSpec((d, tn), lambda i, j: (0, j))],
        out_specs=[pl.BlockSpec((nb, tm, LANES), lambda i, j: (j, i, 0)),
                   pl.BlockSpec((1, tm, LANES), lambda i, j: (0, i, 0))],
        out_shape=[jax.ShapeDtypeStruct((n // LANES, t, LANES), BF),
                   jax.ShapeDtypeStruct((1, t, LANES), F32)],
        scratch_shapes=[pltpu.VMEM((tm, d), BF)],
        compiler_params=pltpu.CompilerParams(dimension_semantics=("parallel", "arbitrary")),
        name="proj",
    )(x2, norm_w, w_all)


def _gdn_body(q_ref, k_ref, v_ref, z_ref, sm_ref, cw_ref, alog_ref, dtb_ref, nw_ref, o_ref,
              ext_ref, st_ref):
    rows = GDN_ROWS
    c = GDN_CHUNK
    nchunk = rows // c
    s = pl.program_id(1)

    @pl.when(s == 0)
    def _():
        ext_ref[:, 0:8, :] = jnp.zeros((3 * GDN_HEADS, 8, LANES), F32)
        st_ref[...] = jnp.zeros_like(st_ref)

    for j in range(GDN_HEADS):
        ext_ref[j, 8:8 + rows, :] = q_ref[j].astype(F32)
        ext_ref[GDN_HEADS + j, 8:8 + rows, :] = k_ref[j].astype(F32)
        ext_ref[2 * GDN_HEADS + j, 8:8 + rows, :] = v_ref[j].astype(F32)

    ri = lax.broadcasted_iota(jnp.int32, (rows, rows), 0)
    ci = lax.broadcasted_iota(jnp.int32, (rows, rows), 1)
    same = (ri // c) == (ci // c)
    tril = same & (ri >= ci)
    strict = same & (ri > ci)
    l_tril = jnp.where(tril, 1.0, 0.0).astype(BF)
    eye = jnp.where(ri == ci, 1.0, 0.0)

    sm = sm_ref[0]
    lane = lax.broadcasted_iota(jnp.int32, (rows, LANES), 1)
    gall = jnp.where(lane < GDN_HEADS, -jnp.exp(alog_ref[...]) * _softplus(sm + dtb_ref[...]), 0.0)
    g1 = gall.astype(BF).astype(F32)
    r1 = gall - g1
    g2 = r1.astype(BF).astype(F32)
    packed = g1 + pltpu.roll(g2, GDN_HEADS, 1) + pltpu.roll(r1 - g2, 2 * GDN_HEADS, 1)
    gc = jnp.dot(l_tril, packed.astype(BF), preferred_element_type=F32)
    gcum_all = gc + pltpu.roll(gc, LANES - GDN_HEADS, 1) + pltpu.roll(gc, LANES - 2 * GDN_HEADS, 1)

    def conv(j):
        acc = cw_ref[0, pl.ds(j, 1), :] * ext_ref[j, pl.ds(5, rows), :]
        for i in range(1, GDN_CONV):
            acc = acc + cw_ref[i, pl.ds(j, 1), :] * ext_ref[j, pl.ds(5 + i, rows), :]
        return acc

    def head_setup(h):
        qh = _silu(conv(h))
        kh = _silu(conv(GDN_HEADS + h))
        vv = _silu(conv(2 * GDN_HEADS + h))
        qn = qh * lax.rsqrt(jnp.sum(qh * qh, axis=-1, keepdims=True) + NORM_EPS) * (GDN_DK ** -0.5)
        kn = kh * lax.rsqrt(jnp.sum(kh * kh, axis=-1, keepdims=True) + NORM_EPS)

        beta = _sigmoid(jnp.broadcast_to(sm[:, SM_B + h:SM_B + h + 1], (rows, LANES)))
        gcum = jnp.broadcast_to(gcum_all[:, h:h + 1], (rows, LANES))
        glast = jnp.concatenate(
            [jnp.broadcast_to(gcum[(n + 1) * c - 1:(n + 1) * c, :], (c, LANES)) for n in range(nchunk)],
            axis=0)
        gc2 = jnp.concatenate([gcum, gcum], axis=1)
        gct = gcum.T
        gr = jnp.concatenate([gct, gct], axis=0)
        decay = jnp.where(tril, jnp.exp(jnp.where(tril, gc2 - gr, 0.0)), 0.0)

        kb = kn * beta
        eg = jnp.exp(gcum)
        knb = kn.astype(BF)
        return dict(
            a=jnp.where(strict, _mm_nt(kb, knb) * decay, 0.0),
            intra=_mm_nt(qn, knb) * decay,
            rhs=jnp.concatenate([vv * beta, kb * eg], axis=1),
            qg=qn * eg, kdec=kn * jnp.exp(glast - gcum), cd=jnp.exp(glast))

    def all_heads():
        hs = list(range(GDN_HEADS))
        w = [head_setup(h) for h in hs]
        p = [_mm(d["a"], d["a"]) for d in w]
        t = [eye - d["a"] for d in w]
        for j in range(1, 6):
            tp = [_mm(ti, pi) for ti, pi in zip(t, p)]
            if j < 5:
                p = [_mm(pi, pi) for pi in p]
            t = [ti + tpi for ti, tpi in zip(t, tp)]
        sol = [d["rhs"] + _mm(ti - eye, d["rhs"]) for d, ti in zip(w, t)]
        st = [st_ref[h] for h in hs]
        outs = [[] for _ in hs]
        for n in range(nchunk):
            sl = slice(n * c, (n + 1) * c)
            ks = [_mm(jnp.concatenate([s_[sl, GDN_DV:], d["qg"][sl]], axis=0), si)
                  for s_, d, si in zip(sol, w, st)]
            vn = [s_[sl, :GDN_DV] - k_[:c] for s_, k_ in zip(sol, ks)]
            for u, (d, k_, v_) in enumerate(zip(w, ks, vn)):
                outs[u].append(k_[c:] + _mm(d["intra"][sl, n * c:(n + 1) * c], v_))
            st = [si * jnp.concatenate([d["cd"][sl], d["cd"][sl]], axis=0) + _mm_tn(d["kdec"][sl], v_)
                  for si, d, v_ in zip(st, w, vn)]
        for u, h in enumerate(hs):
            st_ref[h] = st[u]
            o = jnp.concatenate(outs[u], axis=0)
            on = o * lax.rsqrt(jnp.mean(o * o, axis=-1, keepdims=True) + NORM_EPS) * nw_ref[...]
            o_ref[h] = (on * _silu(z_ref[h].astype(F32))).astype(o_ref.dtype)

    all_heads()

    for j in range(3 * GDN_HEADS):
        ext_ref[j, 0:8, :] = ext_ref[j, rows:rows + 8, :]


def _gdn(p3, sm, conv_w3, alog_b, dtb_b, gdn_norm_w, b_, s_):
    rows = GDN_ROWS
    ns = s_ // rows
    hb = GDN_HEADS

    def cb(base):
        return pl.BlockSpec((hb, rows, LANES), lambda b, s, base=base: (base // hb, b * ns + s, 0))

    return pl.pallas_call(
        _gdn_body,
        grid=(b_, ns),
        in_specs=[cb(CB_GQ), cb(CB_GK), cb(CB_GV), cb(CB_GZ),
                  pl.BlockSpec((1, rows, LANES), lambda b, s: (0, b * ns + s, 0)),
                  pl.BlockSpec((GDN_CONV, 3 * hb, LANES), lambda b, s: (0, 0, 0)),
                  pl.BlockSpec((1, LANES), lambda b, s: (0, 0)),
                  pl.BlockSpec((1, LANES), lambda b, s: (0, 0)),
                  pl.BlockSpec((1, LANES), lambda b, s: (0, 0))],
        out_specs=pl.BlockSpec((hb, rows, LANES), lambda b, s: (0, b * ns + s, 0)),
        out_shape=jax.ShapeDtypeStruct((hb, b_ * s_, LANES), BF),
        scratch_shapes=[pltpu.VMEM((3 * hb, rows + 8, LANES), F32),
                        pltpu.VMEM((hb, GDN_DK, GDN_DV), F32)],
        compiler_params=pltpu.CompilerParams(dimension_semantics=("parallel", "arbitrary")),
        name="gdn",
    )(p3, p3, p3, p3, sm, conv_w3, alog_b, dtb_b, gdn_norm_w)


def _cmp_body(x_ref, pe_ref, w1_ref, w2_ref, nw_ref, o_ref, c_ref, xf_ref):
    kv = pl.program_id(1)
    nch = c_ref.shape[0]
    half = CMP_STRIDE * NSA_DH
    xf_ref[...] = x_ref[0].astype(F32)
    for p in range(CMP_STRIDE):
        c_ref[:, p * NSA_DH:(p + 1) * NSA_DH] = xf_ref[pl.ds(p, nch, stride=CMP_STRIDE), :]
    cc = c_ref[...]
    u = _mm(cc + pe_ref[0, 0:1, :], w1_ref[0, 0:half, :])
    v = _mm(cc + pe_ref[0, 1:2, :], w1_ref[0, half:2 * half, :])
    v_next = jnp.concatenate([v[1:], v[:1]], axis=0)
    hid = _silu(u + v_next)
    out = _mm(hid, w2_ref[0])
    normed = out * lax.rsqrt(jnp.mean(out * out, axis=-1, keepdims=True) + NORM_EPS) * nw_ref[...]
    o_ref[0, 0, 0] = jnp.where(kv == 0, normed, out)


def _cmp(p3, pe2, w1, w2, knw0, b_, s_):
    nch = s_ // CMP_STRIDE
    g_ = NSA_GROUPS
    return pl.pallas_call(
        _cmp_body,
        grid=(b_, 2, g_),
        in_specs=[pl.BlockSpec((1, s_, LANES), lambda b, kv, g: (CB_KC + 2 * kv + g, b, 0)),
                  pl.BlockSpec((1, 2, CMP_STRIDE * NSA_DH), lambda b, kv, g: (kv, 0, 0)),
                  pl.BlockSpec((1, CMP_BLOCK * NSA_DH, CMP_HIDDEN), lambda b, kv, g: (kv, 0, 0)),
                  pl.BlockSpec((1, CMP_HIDDEN, NSA_DH), lambda b, kv, g: (kv, 0, 0)),
                  pl.BlockSpec((1, NSA_DH), lambda b, kv, g: (0, 0))],
        out_specs=pl.BlockSpec((1, 1, 1, nch, NSA_DH), lambda b, kv, g: (b, kv, g, 0, 0)),
        out_shape=jax.ShapeDtypeStruct((b_, 2, g_, nch, NSA_DH), F32),
        scratch_shapes=[pltpu.VMEM((nch, CMP_STRIDE * NSA_DH), F32), pltpu.VMEM((s_, LANES), F32)],
        compiler_params=pltpu.CompilerParams(dimension_semantics=("parallel", "arbitrary", "arbitrary")),
        name="cmp",
    )(p3, pe2, w1, w2, knw0)


def _bias_of(d, rb_ref, h):
    val = jnp.full(d.shape, rb_ref[0, h], F32)
    for k in range(1, REL_BUCKETS):
        val = jnp.where(d >= REL_THR[k], rb_ref[k, h], val)
    return val * LOG2E


def _bias_body(rb_ref, tc_ref, ts_ref, tw_ref):
    h = pl.program_id(0)
    _, s_, ncp = tc_ref.shape

    def row_tile(it, carry):
        r0 = pl.multiple_of(it * NSA_TQ, NSA_TQ)
        t = r0 + lax.broadcasted_iota(jnp.int32, (NSA_TQ, ncp), 0)
        n = lax.broadcasted_iota(jnp.int32, (NSA_TQ, ncp), 1)
        d = t - (n * CMP_STRIDE + CMP_BLOCK - 1)
        tc_ref[0, pl.ds(r0, NSA_TQ), :] = jnp.where(d >= 0, _bias_of(d, rb_ref, h), NEG)
        return carry

    lax.fori_loop(0, s_ // NSA_TQ, row_tile, 0)
    i = lax.broadcasted_iota(jnp.int32, (NSA_TB, NSA_TB), 0)
    j = lax.broadcasted_iota(jnp.int32, (NSA_TB, NSA_TB), 1)
    for e in range(ts_ref.shape[1]):
        d = (e - NSA_PAD) * NSA_TB + i - j
        ts_ref[0, e] = jnp.where(d >= 0, _bias_of(d, rb_ref, h), NEG)
    for e in range(tw_ref.shape[1]):
        d = (e - NSA_PAD) * NSA_TB + i - j
        tw_ref[0, e] = jnp.where((d >= 0) & (d < WINDOW), _bias_of(d, rb_ref, h), NEG)


def _sel_table_len():
    a = 0
    while a * NSA_TB - (NSA_TB - 1) < REL_THR[REL_BUCKETS - 1]:
        a += 1
    return a + 1 + NSA_PAD


def _win_table_len():
    return (WINDOW + NSA_TB - 1) // NSA_TB + 2 + NSA_PAD


def _bias_tables(rel_bias, s_):
    ncp = s_ // CMP_STRIDE
    nd = _sel_table_len()
    nw = _win_table_len()
    return pl.pallas_call(
        _bias_body,
        grid=(NSA_HEADS,),
        in_specs=[pl.BlockSpec(memory_space=pltpu.SMEM)],
        out_specs=[pl.BlockSpec((1, s_, ncp), lambda h: (h, 0, 0)),
                   pl.BlockSpec((1, nd, NSA_TB, NSA_TB), lambda h: (h, 0, 0, 0)),
                   pl.BlockSpec((1, nw, NSA_TB, NSA_TB), lambda h: (h, 0, 0, 0))],
        out_shape=[jax.ShapeDtypeStruct((NSA_HEADS, s_, ncp), F32),
                   jax.ShapeDtypeStruct((NSA_HEADS, nd, NSA_TB, NSA_TB), F32),
                   jax.ShapeDtypeStruct((NSA_HEADS, nw, NSA_TB, NSA_TB), F32)],
        compiler_params=pltpu.CompilerParams(dimension_semantics=("parallel",)),
        name="bias",
    )(rel_bias)


def _rms_rows(x, w):
    return x * lax.rsqrt(jnp.mean(x * x, axis=-1, keepdims=True) + NORM_EPS) * w


def _nsa_body(q_ref, sm_ref, kc_ref, vc_ref, ks_ref, vs_ref, kw_ref, vw_ref, tc_ref, ts_ref, tw_ref,
              ovl_ref, e_ref, qnw_ref, knw_ref, o_ref,
              ksb, vsb, kwb, vwb, qa_ref, ms_ref, accs_ref, mw_ref, accw_ref, sa_ref, sb_ref, wa_ref, wb_ref):
    tq, r_, tb = NSA_TQ, NSA_REP, NSA_TB
    g = pl.program_id(1)
    qi = pl.program_id(2)
    nsel = ovl_ref.shape[0]
    topn = min(SEL_TOPN, nsel)

    @pl.when(qi == 0)
    def _():
        ones = jnp.ones(vs_ref.shape[1:], BF)
        ksb[:, :NSA_DH] = _rms_rows(ks_ref[0].astype(F32), knw_ref[1:2, :]).astype(BF)
        ksb[:, NSA_DH:] = e_ref[...]
        kwb[...] = _rms_rows(kw_ref[0].astype(F32), knw_ref[2:3, :]).astype(BF)
        vsb[:, :NSA_DH] = vs_ref[0]
        vsb[:, NSA_DH:] = ones
        vwb[:, :NSA_DH] = vw_ref[0]
        vwb[:, NSA_DH:] = ones

    qscale = NSA_DH ** -0.5 * LOG2E
    for r in range(r_):
        qa_ref[r * tq:(r + 1) * tq, :NSA_DH] = (
            _rms_rows(q_ref[r].astype(F32), qnw_ref[...]) * qscale).astype(BF)
    qs = qa_ref[:, :NSA_DH]

    def flash_init(m_ref, acc_ref):
        m_ref[...] = jnp.full(m_ref.shape, M_INIT, F32)
        acc_ref[...] = jnp.zeros_like(acc_ref)

    def logits(kj, q, k_sc, tk, dst_ref):
        off = pl.multiple_of(jnp.maximum(kj, 0) * tk, tk)
        dst_ref[:, :tk] = _mm_nt(q, k_sc[pl.ds(off, tk), :])

    def flash_step(kj, s_ref, v_sc, tab_ref, tk, m_ref, acc_ref):
        nct = tk // tb
        nrt = tq // tb
        off = pl.multiple_of(jnp.maximum(kj, 0) * tk, tk)
        e0 = qi * nrt - kj * nct + NSA_PAD
        idx = {o: jnp.where(kj >= 0, jnp.clip(e0 + o, 0, tab_ref.shape[1] - 1), 0)
               for o in range(-(nct - 1), nrt)}
        m_prev = m_ref[...]
        m_rows, p_rows = [], []
        for rb in range(r_ * nrt):
            r, rho = divmod(rb, nrt)
            rs = slice(rb * tb, (rb + 1) * tb)
            pieces = [s_ref[rs, c * tb:(c + 1) * tb] + tab_ref[r, idx[rho - c]] for c in range(nct)]
            mx = pieces[0]
            for c in range(1, nct):
                mx = jnp.maximum(mx, pieces[c])
            m_next = jnp.maximum(m_prev[rs], jnp.max(mx, axis=-1, keepdims=True))
            m_rows.append(m_next)
            p_rows.append(jnp.concatenate([jnp.exp2(pc_ - m_next).astype(BF) for pc_ in pieces], axis=1))
        m_next = jnp.concatenate(m_rows, axis=0)
        p = jnp.concatenate(p_rows, axis=0)
        alpha = jnp.exp2(m_prev - m_next)
        acc_ref[...] = (jnp.concatenate([alpha, alpha], axis=1) * acc_ref[...]
                        + jnp.dot(p, v_sc[pl.ds(off, tk), :], preferred_element_type=F32))
        m_ref[...] = m_next

    def flash_result(acc_ref):
        acc = acc_ref[...]
        return acc[:, :NSA_DH] / jnp.maximum(acc[:, NSA_DH:], 1e-30)

    wlast = (qi * tq + tq - 1) // NSA_TK_WIN
    wtiles = [wlast - (NSA_WIN_TILES - 1) + u for u in range(NSA_WIN_TILES)]
    wbufs = (wa_ref, wb_ref)

    def win_step(u):
        flash_step(wtiles[u], wbufs[u % 2], vwb, tw_ref, NSA_TK_WIN, mw_ref, accw_ref)

    lc = _mm_nt(qs, kc_ref[0, 0, 0]) + jnp.concatenate([tc_ref[r] for r in range(r_)], axis=0)
    mc = jnp.max(lc, axis=-1, keepdims=True)
    pc = jnp.exp2(lc - mc)
    lsum = jnp.sum(pc, axis=-1, keepdims=True)
    pc = pc * jnp.where(mc > 0.5 * NEG, 1.0 / jnp.maximum(lsum, 1e-30), 0.0)

    psum = pc[0:tq]
    for r in range(1, r_):
        psum = psum + pc[r * tq:(r + 1) * tq]
    s1, s2, s3 = _split3(psum)
    dn = (((1,), (1,)), ((), ()))
    ovl = ovl_ref[...]
    imp_t = (lax.dot_general(ovl, s1, dn, preferred_element_type=F32)
             + lax.dot_general(ovl, s2, dn, preferred_element_type=F32)
             + lax.dot_general(ovl, s3, dn, preferred_element_type=F32))

    flash_init(mw_ref, accw_ref)
    logits(wtiles[0], qs, kwb, NSA_TK_WIN, wbufs[0])
    logits(wtiles[1], qs, kwb, NSA_TK_WIN, wbufs[1])
    oc = _mm(pc, vc_ref[0, 0, 0])
    win_step(0)
    for u in range(2, NSA_WIN_TILES):
        logits(wtiles[u], qs, kwb, NSA_TK_WIN, wbufs[u % 2])
        win_step(u - 1)

    jb = lax.broadcasted_iota(jnp.int32, (nsel, tq), 0)
    tpos = qi * tq + lax.broadcasted_iota(jnp.int32, (nsel, tq), 1)
    tblk = tpos // SEL_BLOCK
    forced = (jb == 0) | (jb == tblk) | (jb == tblk - 1)
    score = jnp.where(jb <= tblk, imp_t + jnp.where(forced, FORCE_BONUS, 0.0), NEG)
    sub = lax.broadcasted_iota(jnp.int32, (8, tq), 0)
    groups = [score[8 * v:8 * v + 8] for v in range(nsel // 8)]
    counts = [jnp.zeros((8, tq), F32) for _ in groups]
    for jp in range(nsel):
        row = jnp.broadcast_to(score[jp:jp + 1, :], (8, tq))
        for v, grp in enumerate(groups):
            if 8 * v > jp:
                beats = jnp.where(row >= grp, 1.0, 0.0)
            elif 8 * v + 8 <= jp:
                beats = jnp.where(row > grp, 1.0, 0.0)
            else:
                tie = jnp.where(sub + 8 * v > jp, 1.0, 0.0)
                beats = jnp.where(row > grp, 1.0, jnp.where(row == grp, tie, 0.0))
            counts[v] = counts[v] + beats
    unsel_t = jnp.where(jnp.concatenate(counts, axis=0) < topn, 0.0, NEG)
    unsel_pad = jnp.concatenate([unsel_t, jnp.zeros((LANES - nsel, tq), F32)], axis=0).astype(BF)
    ri = lax.broadcasted_iota(jnp.int32, (tq, tq), 0)
    ci = lax.broadcasted_iota(jnp.int32, (tq, tq), 1)
    eye = jnp.where(ri == ci, 1.0, 0.0).astype(BF)
    unsel_q = lax.dot_general(eye, unsel_pad, dn, preferred_element_type=F32).astype(BF)
    for r in range(r_):
        qa_ref[r * tq:(r + 1) * tq, NSA_DH:] = unsel_q

    qa = qa_ref[...]
    slast = (qi * tq + tq - 1) // NSA_TK_SEL
    flash_init(ms_ref, accs_ref)
    logits(0, qa, ksb, NSA_TK_SEL, sa_ref)
    win_step(NSA_WIN_TILES - 1)

    sg = _sigmoid(sm_ref[0])

    def gate(r, br):
        c0 = SM_GATE + r * 3 + br
        c1 = SM_GATE + (r_ + r) * 3 + br
        return jnp.where(g == 0, sg[:, c0:c0 + 1], sg[:, c1:c1 + 1])

    o_w = flash_result(accw_ref)
    for r in range(r_):
        rs = slice(r * tq, (r + 1) * tq)
        accw_ref[rs, :NSA_DH] = gate(r, 0) * oc[rs] + gate(r, 2) * o_w[rs]

    def sel_step(kj, s_ref):
        flash_step(kj, s_ref, vsb, ts_ref, NSA_TK_SEL, ms_ref, accs_ref)

    def pair(i, carry):
        kj = 2 * i
        logits(kj + 1, qa, ksb, NSA_TK_SEL, sb_ref)
        sel_step(kj, sa_ref)
        logits(jnp.minimum(kj + 2, slast), qa, ksb, NSA_TK_SEL, sa_ref)
        sel_step(kj + 1, sb_ref)
        return carry

    lax.fori_loop(0, (slast + 1) // 2, pair, 0)

    @pl.when(slast % 2 == 0)
    def _():
        sel_step(slast, sa_ref)

    o_s = flash_result(accs_ref)
    sg = _sigmoid(sm_ref[0])
    for r in range(r_):
        rs = slice(r * tq, (r + 1) * tq)
        o_ref[r] = (accw_ref[rs, :NSA_DH] + gate(r, 1) * o_s[rs]).astype(o_ref.dtype)


def _nsa(p3, sm, ckv, tab_c, tab_s, tab_w, ovl_t, e_sel, qnw, knw, b_, s_):
    tq, tk, r_ = NSA_TQ, NSA_TB, NSA_REP
    nq = s_ // tq
    ncp = s_ // CMP_STRIDE
    nsel = s_ // SEL_BLOCK
    nd = tab_s.shape[1]
    nw = tab_w.shape[1]
    assert s_ % NSA_TK_SEL == 0 and NSA_TK_SEL // NSA_TB - 1 <= NSA_PAD
    assert NSA_TQ % NSA_TB == 0 and max(NSA_TQ, NSA_TK_WIN) % min(NSA_TQ, NSA_TK_WIN) == 0

    def kv_spec(base):
        return pl.BlockSpec((1, s_, LANES), lambda b, g, q, base=base: (base + g, b, 0))

    return pl.pallas_call(
        _nsa_body,
        grid=(b_, NSA_GROUPS, nq),
        in_specs=[pl.BlockSpec((r_, tq, LANES), lambda b, g, q: (CB_NQ // r_ + g, b * nq + q, 0)),
                  pl.BlockSpec((1, tq, LANES), lambda b, g, q: (0, b * nq + q, 0)),
                  pl.BlockSpec((1, 1, 1, ncp, NSA_DH), lambda b, g, q: (b, 0, g, 0, 0)),
                  pl.BlockSpec((1, 1, 1, ncp, NSA_DH), lambda b, g, q: (b, 1, g, 0, 0)),
                  kv_spec(CB_KS), kv_spec(CB_VS), kv_spec(CB_KW), kv_spec(CB_VW),
                  pl.BlockSpec((r_, tq, ncp), lambda b, g, q: (g, q, 0)),
                  pl.BlockSpec((r_, nd, tk, tk), lambda b, g, q: (g, 0, 0, 0)),
                  pl.BlockSpec((r_, nw, tk, tk), lambda b, g, q: (g, 0, 0, 0)),
                  pl.BlockSpec((nsel, ncp), lambda b, g, q: (0, 0)),
                  pl.BlockSpec((s_, LANES), lambda b, g, q: (0, 0)),
                  pl.BlockSpec((1, NSA_DH), lambda b, g, q: (0, 0)),
                  pl.BlockSpec((3, NSA_DH), lambda b, g, q: (0, 0))],
        out_specs=pl.BlockSpec((r_, tq, LANES), lambda b, g, q: (g, b * nq + q, 0)),
        out_shape=jax.ShapeDtypeStruct((NSA_HEADS, b_ * s_, LANES), BF),
        scratch_shapes=[pltpu.VMEM((s_, 2 * NSA_DH), BF), pltpu.VMEM((s_, 2 * NSA_DH), BF),
                        pltpu.VMEM((s_, NSA_DH), BF), pltpu.VMEM((s_, 2 * NSA_DH), BF),
                        pltpu.VMEM((r_ * tq, 2 * NSA_DH), BF),
                        pltpu.VMEM((r_ * tq, LANES), F32), pltpu.VMEM((r_ * tq, 2 * NSA_DH), F32),
                        pltpu.VMEM((r_ * tq, LANES), F32), pltpu.VMEM((r_ * tq, 2 * NSA_DH), F32),
                        pltpu.VMEM((r_ * tq, NSA_TK_SEL), F32), pltpu.VMEM((r_ * tq, NSA_TK_SEL), F32),
                        pltpu.VMEM((r_ * tq, NSA_TK_WIN), F32), pltpu.VMEM((r_ * tq, NSA_TK_WIN), F32)],
        compiler_params=pltpu.CompilerParams(dimension_semantics=("parallel", "arbitrary", "arbitrary")),
        name="nsa",
    )(p3, sm, ckv, ckv, p3, p3, p3, p3, tab_c, tab_s, tab_w, ovl_t, e_sel, qnw, knw)


def _merge_body(x_ref, ya_ref, yb_ref, ga_ref, gb_ref, wpa_ref, wpb_ref, wo_ref, o_ref):
    nh = ya_ref.shape[0]
    ya = jnp.concatenate([ya_ref[j] for j in range(nh)], axis=1)
    yb = jnp.concatenate([yb_ref[j] for j in range(nh)], axis=1)
    ga = _sigmoid(jnp.concatenate([ga_ref[j].astype(F32) for j in range(nh)], axis=1))
    gb = _sigmoid(jnp.concatenate([gb_ref[j].astype(F32) for j in range(nh)], axis=1))
    mixed = (ga * jnp.dot(ya, wpa_ref[...], preferred_element_type=F32)
             + gb * jnp.dot(yb, wpb_ref[...], preferred_element_type=F32))
    o_ref[...] = x_ref[...] + jnp.dot(mixed.astype(BF), wo_ref[...], preferred_element_type=F32)


def _merge(x2, ya, yb, p3, wpa, wpb, wo, tm):
    t, d = x2.shape
    nh = d // LANES
    hspec = pl.BlockSpec((nh, tm, LANES), lambda i: (0, i, 0))
    wspec = pl.BlockSpec((d, d), lambda i: (0, 0))
    return pl.pallas_call(
        _merge_body,
        grid=(t // tm,),
        in_specs=[pl.BlockSpec((tm, d), lambda i: (i, 0)), hspec, hspec,
                  pl.BlockSpec((nh, tm, LANES), lambda i: (CB_MGA // nh, i, 0)),
                  pl.BlockSpec((nh, tm, LANES), lambda i: (CB_MGB // nh, i, 0)),
                  wspec, wspec, wspec],
        out_specs=pl.BlockSpec((tm, d), lambda i: (i, 0)),
        out_shape=jax.ShapeDtypeStruct((t, d), F32),
        compiler_params=pltpu.CompilerParams(dimension_semantics=("parallel",)),
        name="merge",
    )(x2, ya, yb, p3, p3, wpa, wpb, wo)


def _ffn_body(x_ref, nw_ref, wg_ref, wu_ref, wd_ref, o_ref):
    x = x_ref[...]
    h = (x * lax.rsqrt(jnp.mean(x * x, axis=-1, keepdims=True) + NORM_EPS) * nw_ref[...]).astype(BF)
    gate = jnp.dot(h, wg_ref[...], preferred_element_type=F32)
    up = jnp.dot(h, wu_ref[...], preferred_element_type=F32)
    act = (_silu(gate) * up).astype(BF)
    o_ref[...] = x + jnp.dot(act, wd_ref[...], preferred_element_type=F32)


def _ffn(x2, norm_w, wg, wu, wd, tm):
    t, d = x2.shape
    f = wg.shape[1]
    return pl.pallas_call(
        _ffn_body,
        grid=(t // tm,),
        in_specs=[pl.BlockSpec((tm, d), lambda i: (i, 0)),
                  pl.BlockSpec((1, d), lambda i: (0, 0)),
                  pl.BlockSpec((d, f), lambda i: (0, 0)),
                  pl.BlockSpec((d, f), lambda i: (0, 0)),
                  pl.BlockSpec((f, d), lambda i: (0, 0))],
        out_specs=pl.BlockSpec((tm, d), lambda i: (i, 0)),
        out_shape=jax.ShapeDtypeStruct((t, d), F32),
        compiler_params=pltpu.CompilerParams(dimension_semantics=("parallel",)),
        name="ffn",
    )(x2, norm_w, wg, wu, wd)


def _arrange_w_in(w_in):
    o_ga = 4 * GDN_HEADS * GDN_DK
    o_gb = o_ga + GDN_HEADS
    o_nq = o_gb + GDN_HEADS
    o_nkv = o_nq + NSA_HEADS * NSA_DH
    o_ng = o_nkv + 6 * NSA_GROUPS * NSA_DH
    o_mg = o_ng + 3 * NSA_HEADS
    d = w_in.shape[0]
    small = jnp.concatenate([w_in[:, o_ga:o_nq], w_in[:, o_ng:o_mg],
                             jnp.zeros((d, 2 * LANES - (o_nq - o_ga) - (o_mg - o_ng)), w_in.dtype)], axis=1)
    return jnp.concatenate([w_in[:, :o_ga], w_in[:, o_nq:o_nkv], w_in[:, o_mg:], w_in[:, o_nkv:o_ng], small],
                           axis=1).astype(BF)


def _overlap_t(s_):
    ncp = s_ // CMP_STRIDE
    nsel = s_ // SEL_BLOCK
    cs = np.arange(ncp) * CMP_STRIDE
    ss = np.arange(nsel) * SEL_BLOCK
    ov = (cs[None, :] < ss[:, None] + SEL_BLOCK) & (cs[None, :] + CMP_BLOCK > ss[:, None])
    ov[:, ncp - 1] = False
    return jnp.asarray(ov.astype(np.float32), BF)


def _sel_expand(s_):
    assert s_ // SEL_BLOCK <= LANES
    pos = np.arange(s_)
    e = (pos[:, None] // SEL_BLOCK == np.arange(LANES)[None, :]).astype(np.float32)
    return jnp.asarray(e, BF)


def kernel(x, norm1_w, w_in, conv_w, a_log, dt_bias, gdn_norm_w, cmp_pe, cmp_w1, cmp_w2, q_norm_w, k_norm_w,
           rel_bias, w_proj_a, w_proj_b, w_out, norm2_w, w_gate, w_up, w_down):
    b_, s_, d = x.shape
    t = b_ * s_
    x2 = x.reshape(t, d)
    tab_c, tab_s, tab_w = _bias_tables(rel_bias, s_)
    ovl_t = _overlap_t(s_)
    e_sel = _sel_expand(s_)
    for l in range(norm1_w.shape[0]):
        p3, sm = _proj(x2, norm1_w[l][None, :], _arrange_w_in(w_in[l]), tm=min(1024, t), tn=10 * LANES)
        conv_w3 = conv_w[l].reshape(GDN_CONV, 3 * GDN_HEADS, LANES)
        alog_b = jnp.pad(a_log[l], (0, LANES - GDN_HEADS))[None, :]
        dtb_b = jnp.pad(dt_bias[l], (0, LANES - GDN_HEADS))[None, :]
        y_a = _gdn(p3, sm, conv_w3, alog_b, dtb_b, gdn_norm_w[l][None, :], b_, s_)
        pe2 = cmp_pe[l].reshape(2, 2, CMP_STRIDE * NSA_DH)
        ckv = _cmp(p3, pe2, cmp_w1[l].astype(BF), cmp_w2[l].astype(BF), k_norm_w[l][0:1], b_, s_)
        y_b = _nsa(p3, sm, ckv, tab_c, tab_s, tab_w, ovl_t, e_sel, q_norm_w[l][None, :], k_norm_w[l], b_, s_)
        x2 = _merge(x2, y_a, y_b, p3, w_proj_a[l].astype(BF), w_proj_b[l].astype(BF), w_out[l].astype(BF),
                    tm=min(512, t))
        x2 = _ffn(x2, norm2_w[l][None, :], w_gate[l].astype(BF), w_up[l].astype(BF), w_down[l].astype(BF),
                  tm=min(512, t))
    return x2.reshape(b_, s_, d)
```

```python
import functools
import math

import numpy as np
import jax
import jax.numpy as jnp
from jax import lax
from jax.experimental import pallas as pl
from jax.experimental.pallas import tpu as pltpu

F32 = jnp.float32
BF = jnp.bfloat16

LANES = 128
D_MODEL = 1024
GDN_HEADS = 8
GDN_DK = 128
GDN_DV = 128
GDN_CONV = 4
GDN_CHUNK = 64
NSA_HEADS = 8
NSA_GROUPS = 2
NSA_REP = NSA_HEADS // NSA_GROUPS
NSA_DH = 128
CMP_BLOCK = 32
CMP_STRIDE = 16
CMP_HIDDEN = 256
SEL_BLOCK = 64
SEL_TOPN = 16
WINDOW = 512
FORCE_BONUS = 1000.0
REL_BUCKETS = 32
REL_MAX_DIST = 1024
FFN_HIDDEN = 2816
NORM_EPS = 1e-6
NEG = -1e30
M_INIT = -3e38
LOG2E = 1.4426950408889634

CB_GQ, CB_GK, CB_GV, CB_GZ = 0, 8, 16, 24
CB_NQ = 32
CB_MGA, CB_MGB = 40, 48
CB_KC, CB_VC, CB_KS, CB_VS, CB_KW, CB_VW = 56, 58, 60, 62, 64, 66
CB_SMALL = 68
N_CB = 70
SM_A, SM_B, SM_GATE = 0, 8, 16

GDN_ROWS = 256
NSA_TQ = 256
NSA_TB = 128
NSA_TK_SEL = 512
NSA_TK_WIN = 256
NSA_WIN_TILES = (WINDOW + max(NSA_TQ, NSA_TK_WIN) - 2) // NSA_TK_WIN + 1
NSA_PAD = NSA_TK_SEL // NSA_TB - 1


def _mm(a, b):
    return jnp.dot(a.astype(BF), b.astype(BF), preferred_element_type=F32)


def _mm_nt(a, b):
    return lax.dot_general(a.astype(BF), b.astype(BF), (((1,), (1,)), ((), ())),
                           preferred_element_type=F32)


def _mm_tn(a, b):
    return lax.dot_general(a.astype(BF), b.astype(BF), (((0,), (0,)), ((), ())),
                           preferred_element_type=F32)


def _split3(x):
    x1 = x.astype(BF)
    r1 = x - x1.astype(F32)
    x2 = r1.astype(BF)
    x3 = (r1 - x2.astype(F32)).astype(BF)
    return x1, x2, x3


def _sigmoid(x):
    return 0.5 * jnp.tanh(0.5 * x) + 0.5


def _silu(x):
    return x * _sigmoid(x)


def _softplus(x):
    return jnp.maximum(x, 0.0) + jnp.log1p(jnp.exp(-jnp.abs(x)))


def _rel_thresholds():
    d = np.arange(0, 4 * REL_MAX_DIST, dtype=np.int64)
    max_exact = REL_BUCKETS // 2
    d_f = np.maximum(d, 1).astype(np.float32)
    large = max_exact + (np.log(d_f / np.float32(max_exact)) / np.float32(math.log(REL_MAX_DIST / max_exact))
                         * np.float32(REL_BUCKETS - max_exact)).astype(np.int32)
    large = np.minimum(large, REL_BUCKETS - 1)
    bucket = np.where(d < max_exact, d, large)
    assert np.all(np.diff(bucket) >= 0)
    return [int(np.argmax(bucket >= k)) for k in range(REL_BUCKETS)]


REL_THR = _rel_thresholds()


def _proj_body(x_ref, nw_ref, w_ref, o_ref, sm_ref, h_ref):
    @pl.when(pl.program_id(1) == 0)
    def _():
        x = x_ref[...]
        y = x * lax.rsqrt(jnp.mean(x * x, axis=-1, keepdims=True) + NORM_EPS)
        h_ref[...] = (y * nw_ref[...]).astype(BF)

    nb = o_ref.shape[0]
    r = jnp.dot(h_ref[...], w_ref[...], preferred_element_type=F32)
    for j in range(nb):
        o_ref[j] = r[:, j * LANES:(j + 1) * LANES].astype(BF)

    @pl.when(pl.program_id(1) == CB_SMALL // nb)
    def _():
        sm_ref[0] = r[:, (CB_SMALL % nb) * LANES:(CB_SMALL % nb + 1) * LANES]


def _proj(x2, norm_w, w_all, tm, tn):
    t, d = x2.shape
    n = w_all.shape[1]
    nb = tn // LANES
    return pl.pallas_call(
        _proj_body,
        grid=(t // tm, n // tn),
        in_specs=[pl.BlockSpec((tm, d), lambda i, j: (i, 0)),
                  pl.BlockSpec((1, d), lambda i, j: (0, 0)),
                  pl.BlockSpec((d, tn), lambda i, j: (0, j))],
        out_specs=[pl.BlockSpec((nb, tm, LANES), lambda i, j: (j, i, 0)),
                   pl.BlockSpec((1, tm, LANES), lambda i, j: (0, i, 0))],
        out_shape=[jax.ShapeDtypeStruct((n // LANES, t, LANES), BF),
                   jax.ShapeDtypeStruct((1, t, LANES), F32)],
        scratch_shapes=[pltpu.VMEM((tm, d), BF)],
        compiler_params=pltpu.CompilerParams(dimension_semantics=("parallel", "arbitrary")),
        name="proj",
    )(x2, norm_w, w_all)


def _gdn_body(q_ref, k_ref, v_ref, z_ref, sm_ref, cw_ref, alog_ref, dtb_ref, nw_ref, o_ref,
              ext_ref, st_ref):
    rows = GDN_ROWS
    c = GDN_CHUNK
    nchunk = rows // c
    s = pl.program_id(1)

    @pl.when(s == 0)
    def _():
        ext_ref[:, 0:8, :] = jnp.zeros((3 * GDN_HEADS, 8, LANES), F32)
        st_ref[...] = jnp.zeros_like(st_ref)

    for j in range(GDN_HEADS):
        ext_ref[j, 8:8 + rows, :] = q_ref[j].astype(F32)
        ext_ref[GDN_HEADS + j, 8:8 + rows, :] = k_ref[j].astype(F32)
        ext_ref[2 * GDN_HEADS + j, 8:8 + rows, :] = v_ref[j].astype(F32)

    ri = lax.broadcasted_iota(jnp.int32, (rows, rows), 0)
    ci = lax.broadcasted_iota(jnp.int32, (rows, rows), 1)
    l_tril = jnp.where(((ri // c) == (ci // c)) & (ri >= ci), 1.0, 0.0).astype(BF)
    rt = lax.broadcasted_iota(jnp.int32, (LANES, LANES), 0)
    ct = lax.broadcasted_iota(jnp.int32, (LANES, LANES), 1)
    same_t = (rt // c) == (ct // c)
    tril_t = same_t & (rt >= ct)
    strict_t = same_t & (rt > ct)
    eye = jnp.where(rt == ct, 1.0, 0.0)

    sm = sm_ref[0]
    lane = lax.broadcasted_iota(jnp.int32, (rows, LANES), 1)
    gall = jnp.where(lane < GDN_HEADS, -jnp.exp(alog_ref[...]) * _softplus(sm + dtb_ref[...]), 0.0)
    g1 = gall.astype(BF).astype(F32)
    r1 = gall - g1
    g2 = r1.astype(BF).astype(F32)
    packed = g1 + pltpu.roll(g2, GDN_HEADS, 1) + pltpu.roll(r1 - g2, 2 * GDN_HEADS, 1)
    gc = jnp.dot(l_tril, packed.astype(BF), preferred_element_type=F32)
    gcum_all = gc + pltpu.roll(gc, LANES - GDN_HEADS, 1) + pltpu.roll(gc, LANES - 2 * GDN_HEADS, 1)

    def conv(j):
        acc = cw_ref[0, pl.ds(j, 1), :] * ext_ref[j, pl.ds(5, rows), :]
        for i in range(1, GDN_CONV):
            acc = acc + cw_ref[i, pl.ds(j, 1), :] * ext_ref[j, pl.ds(5 + i, rows), :]
        return acc

    def head_setup(h):
        qh = _silu(conv(h))
        kh = _silu(conv(GDN_HEADS + h))
        vv = _silu(conv(2 * GDN_HEADS + h))
        qn = qh * lax.rsqrt(jnp.sum(qh * qh, axis=-1, keepdims=True) + NORM_EPS) * (GDN_DK ** -0.5)
        kn = kh * lax.rsqrt(jnp.sum(kh * kh, axis=-1, keepdims=True) + NORM_EPS)

        beta = _sigmoid(jnp.broadcast_to(sm[:, SM_B + h:SM_B + h + 1], (rows, LANES)))
        gcum = jnp.broadcast_to(gcum_all[:, h:h + 1], (rows, LANES))
        glast = jnp.concatenate(
            [jnp.broadcast_to(gcum[(n + 1) * c - 1:(n + 1) * c, :], (c, LANES)) for n in range(nchunk)],
            axis=0)
        gct = gcum.T
        kb = kn * beta
        eg = jnp.exp(gcum)
        knb = kn.astype(BF)
        a_t, intra_t = [], []
        for u in range(rows // LANES):
            rs = slice(u * LANES, (u + 1) * LANES)
            diff = gcum[rs] - gct[:, rs]
            decay = jnp.where(tril_t, jnp.exp(jnp.where(tril_t, diff, 0.0)), 0.0)
            a_t.append(jnp.where(strict_t, _mm_nt(kb[rs], knb[rs]) * decay, 0.0))
            intra_t.append(_mm_nt(qn[rs], knb[rs]) * decay)
        return dict(
            a=a_t, intra=intra_t,
            rhs=jnp.concatenate([vv * beta, kb * eg], axis=1),
            qg=qn * eg, kdec=kn * jnp.exp(glast - gcum), cd=jnp.exp(glast))

    def all_heads():
        hs = list(range(GDN_HEADS))
        w, t = [], []
        nt = rows // LANES
        for grp in (hs[:GDN_HEADS // 2], hs[GDN_HEADS // 2:]):
            wg = [head_setup(h) for h in grp]
            a_all = [a_ for d in wg for a_ in d["a"]]
            p = [_mm(a_, a_) for a_ in a_all]
            tg = [eye - a_ for a_ in a_all]
            for j in range(1, 6):
                tp = [_mm(ti, pi) for ti, pi in zip(tg, p)]
                if j < 5:
                    p = [_mm(pi, pi) for pi in p]
                tg = [ti + tpi for ti, tpi in zip(tg, tp)]
            w += wg
            t += [tg[i * nt:(i + 1) * nt] for i in range(len(wg))]
        sol = [[d["rhs"][u * LANES:(u + 1) * LANES] + _mm(tu - eye, d["rhs"][u * LANES:(u + 1) * LANES])
                for u, tu in enumerate(ti)] for d, ti in zip(w, t)]
        st = [st_ref[h] for h in hs]
        outs = [[] for _ in hs]
        cpt = LANES // c
        for n in range(nchunk):
            sl = slice(n * c, (n + 1) * c)
            lo = slice((n % cpt) * c, (n % cpt + 1) * c)
            ks = [_mm(jnp.concatenate([s_[n // cpt][lo, GDN_DV:], d["qg"][sl]], axis=0), si)
                  for s_, d, si in zip(sol, w, st)]
            vn = [s_[n // cpt][lo, :GDN_DV] - k_[:c] for s_, k_ in zip(sol, ks)]
            for u, (d, k_, v_) in enumerate(zip(w, ks, vn)):
                outs[u].append(k_[c:] + _mm(d["intra"][n // cpt][lo, lo], v_))
            st = [si * jnp.concatenate([d["cd"][sl], d["cd"][sl]], axis=0) + _mm_tn(d["kdec"][sl], v_)
                  for si, d, v_ in zip(st, w, vn)]
        for u, h in enumerate(hs):
            st_ref[h] = st[u]
            o = jnp.concatenate(outs[u], axis=0)
            on = o * lax.rsqrt(jnp.mean(o * o, axis=-1, keepdims=True) + NORM_EPS) * nw_ref[...]
            o_ref[h] = (on * _silu(z_ref[h].astype(F32))).astype(o_ref.dtype)

    all_heads()

    for j in range(3 * GDN_HEADS):
        ext_ref[j, 0:8, :] = ext_ref[j, rows:rows + 8, :]


def _gdn(p3, sm, conv_w3, alog_b, dtb_b, gdn_norm_w, b_, s_):
    rows = GDN_ROWS
    ns = s_ // rows
    hb = GDN_HEADS

    def cb(base):
        return pl.BlockSpec((hb, rows, LANES), lambda b, s, base=base: (base // hb, b * ns + s, 0))

    return pl.pallas_call(
        _gdn_body,
        grid=(b_, ns),
        in_specs=[cb(CB_GQ), cb(CB_GK), cb(CB_GV), cb(CB_GZ),
                  pl.BlockSpec((1, rows, LANES), lambda b, s: (0, b * ns + s, 0)),
                  pl.BlockSpec((GDN_CONV, 3 * hb, LANES), lambda b, s: (0, 0, 0)),
                  pl.BlockSpec((1, LANES), lambda b, s: (0, 0)),
                  pl.BlockSpec((1, LANES), lambda b, s: (0, 0)),
                  pl.BlockSpec((1, LANES), lambda b, s: (0, 0))],
        out_specs=pl.BlockSpec((hb, rows, LANES), lambda b, s: (0, b * ns + s, 0)),
        out_shape=jax.ShapeDtypeStruct((hb, b_ * s_, LANES), BF),
        scratch_shapes=[pltpu.VMEM((3 * hb, rows + 8, LANES), F32),
                        pltpu.VMEM((hb, GDN_DK, GDN_DV), F32)],
        compiler_params=pltpu.CompilerParams(dimension_semantics=("parallel", "arbitrary")),
        name="gdn",
    )(p3, p3, p3, p3, sm, conv_w3, alog_b, dtb_b, gdn_norm_w)


def _cmp_body(x_ref, pe_ref, w1_ref, w2_ref, nw_ref, o_ref, c_ref, xf_ref):
    kv = pl.program_id(1)
    nch = c_ref.shape[0]
    half = CMP_STRIDE * NSA_DH
    xf_ref[...] = x_ref[0].astype(F32)
    for p in range(CMP_STRIDE):
        c_ref[:, p * NSA_DH:(p + 1) * NSA_DH] = xf_ref[pl.ds(p, nch, stride=CMP_STRIDE), :]
    cc = c_ref[...]
    u = _mm(cc + pe_ref[0, 0:1, :], w1_ref[0, 0:half, :])
    v = _mm(cc + pe_ref[0, 1:2, :], w1_ref[0, half:2 * half, :])
    v_next = jnp.concatenate([v[1:], v[:1]], axis=0)
    hid = _silu(u + v_next)
    out = _mm(hid, w2_ref[0])
    normed = out * lax.rsqrt(jnp.mean(out * out, axis=-1, keepdims=True) + NORM_EPS) * nw_ref[...]
    o_ref[0, 0, 0] = jnp.where(kv == 0, normed, out)


def _cmp(p3, pe2, w1, w2, knw0, b_, s_):
    nch = s_ // CMP_STRIDE
    g_ = NSA_GROUPS
    return pl.pallas_call(
        _cmp_body,
        grid=(b_, 2, g_),
        in_specs=[pl.BlockSpec((1, s_, LANES), lambda b, kv, g: (CB_KC + 2 * kv + g, b, 0)),
                  pl.BlockSpec((1, 2, CMP_STRIDE * NSA_DH), lambda b, kv, g: (kv, 0, 0)),
                  pl.BlockSpec((1, CMP_BLOCK * NSA_DH, CMP_HIDDEN), lambda b, kv, g: (kv, 0, 0)),
                  pl.BlockSpec((1, CMP_HIDDEN, NSA_DH), lambda b, kv, g: (kv, 0, 0)),
                  pl.BlockSpec((1, NSA_DH), lambda b, kv, g: (0, 0))],
        out_specs=pl.BlockSpec((1, 1, 1, nch, NSA_DH), lambda b, kv, g: (b, kv, g, 0, 0)),
        out_shape=jax.ShapeDtypeStruct((b_, 2, g_, nch, NSA_DH), F32),
        scratch_shapes=[pltpu.VMEM((nch, CMP_STRIDE * NSA_DH), F32), pltpu.VMEM((s_, LANES), F32)],
        compiler_params=pltpu.CompilerParams(dimension_semantics=("parallel", "arbitrary", "arbitrary")),
        name="cmp",
    )(p3, pe2, w1, w2, knw0)


def _bias_of(d, rb_ref, h):
    val = jnp.full(d.shape, rb_ref[0, h], F32)
    for k in range(1, REL_BUCKETS):
        val = jnp.where(d >= REL_THR[k], rb_ref[k, h], val)
    return val * LOG2E


def _bias_body(rb_ref, tc_ref, ts_ref, tw_ref):
    h = pl.program_id(0)
    _, s_, ncp = tc_ref.shape

    def row_tile(it, carry):
        r0 = pl.multiple_of(it * NSA_TQ, NSA_TQ)
        t = r0 + lax.broadcasted_iota(jnp.int32, (NSA_TQ, ncp), 0)
        n = lax.broadcasted_iota(jnp.int32, (NSA_TQ, ncp), 1)
        d = t - (n * CMP_STRIDE + CMP_BLOCK - 1)
        tc_ref[0, pl.ds(r0, NSA_TQ), :] = jnp.where(d >= 0, _bias_of(d, rb_ref, h), NEG)
        return carry

    lax.fori_loop(0, s_ // NSA_TQ, row_tile, 0)
    i = lax.broadcasted_iota(jnp.int32, (NSA_TB, NSA_TB), 0)
    j = lax.broadcasted_iota(jnp.int32, (NSA_TB, NSA_TB), 1)
    for e in range(ts_ref.shape[1]):
        d = (e - NSA_PAD) * NSA_TB + i - j
        ts_ref[0, e] = jnp.where(d >= 0, _bias_of(d, rb_ref, h), NEG)
    for e in range(tw_ref.shape[1]):
        d = (e - NSA_PAD) * NSA_TB + i - j
        tw_ref[0, e] = jnp.where((d >= 0) & (d < WINDOW), _bias_of(d, rb_ref, h), NEG)


def _sel_table_len():
    a = 0
    while a * NSA_TB - (NSA_TB - 1) < REL_THR[REL_BUCKETS - 1]:
        a += 1
    return a + 1 + NSA_PAD


def _win_table_len():
    return (WINDOW + NSA_TB - 1) // NSA_TB + 2 + NSA_PAD


def _bias_tables(rel_bias, s_):
    ncp = s_ // CMP_STRIDE
    nd = _sel_table_len()
    nw = _win_table_len()
    return pl.pallas_call(
        _bias_body,
        grid=(NSA_HEADS,),
        in_specs=[pl.BlockSpec(memory_space=pltpu.SMEM)],
        out_specs=[pl.BlockSpec((1, s_, ncp), lambda h: (h, 0, 0)),
                   pl.BlockSpec((1, nd, NSA_TB, NSA_TB), lambda h: (h, 0, 0, 0)),
                   pl.BlockSpec((1, nw, NSA_TB, NSA_TB), lambda h: (h, 0, 0, 0))],
        out_shape=[jax.ShapeDtypeStruct((NSA_HEADS, s_, ncp), F32),
                   jax.ShapeDtypeStruct((NSA_HEADS, nd, NSA_TB, NSA_TB), F32),
                   jax.ShapeDtypeStruct((NSA_HEADS, nw, NSA_TB, NSA_TB), F32)],
        compiler_params=pltpu.CompilerParams(dimension_semantics=("parallel",)),
        name="bias",
    )(rel_bias)


def _rms_rows(x, w):
    return x * lax.rsqrt(jnp.mean(x * x, axis=-1, keepdims=True) + NORM_EPS) * w


def _nsa_body(q_ref, sm_ref, kc_ref, vc_ref, ks_ref, vs_ref, kw_ref, vw_ref, tc_ref, ts_ref, tw_ref,
              ovl_ref, e_ref, qnw_ref, knw_ref, o_ref,
              ksb, vsb, kwb, vwb, qa_ref, ms_ref, accs_ref, mw_ref, accw_ref, sa_ref, sb_ref, wa_ref, wb_ref):
    tq, r_, tb = NSA_TQ, NSA_REP, NSA_TB
    g = pl.program_id(1)
    qi = pl.program_id(2)
    nsel = ovl_ref.shape[0]
    topn = min(SEL_TOPN, nsel)

    @pl.when(qi == 0)
    def _():
        ones = jnp.ones(vs_ref.shape[1:], BF)
        ksb[:, :NSA_DH] = _rms_rows(ks_ref[0].astype(F32), knw_ref[1:2, :]).astype(BF)
        ksb[:, NSA_DH:] = e_ref[...]
        kwb[...] = _rms_rows(kw_ref[0].astype(F32), knw_ref[2:3, :]).astype(BF)
        vsb[:, :NSA_DH] = vs_ref[0]
        vsb[:, NSA_DH:] = ones
        vwb[:, :NSA_DH] = vw_ref[0]
        vwb[:, NSA_DH:] = ones

    qscale = NSA_DH ** -0.5 * LOG2E
    for r in range(r_):
        qa_ref[r * tq:(r + 1) * tq, :NSA_DH] = (
            _rms_rows(q_ref[r].astype(F32), qnw_ref[...]) * qscale).astype(BF)
    qs = qa_ref[:, :NSA_DH]

    def flash_init(m_ref, acc_ref):
        m_ref[...] = jnp.full(m_ref.shape, M_INIT, F32)
        acc_ref[...] = jnp.zeros_like(acc_ref)

    def logits(kj, q, k_sc, tk, dst_ref):
        off = pl.multiple_of(jnp.maximum(kj, 0) * tk, tk)
        dst_ref[:, :tk] = _mm_nt(q, k_sc[pl.ds(off, tk), :])

    def flash_step(kj, s_ref, v_sc, tab_ref, tk, m_ref, acc_ref):
        nct = tk // tb
        nrt = tq // tb
        off = pl.multiple_of(jnp.maximum(kj, 0) * tk, tk)
        e0 = qi * nrt - kj * nct + NSA_PAD
        idx = {o: jnp.where(kj >= 0, jnp.clip(e0 + o, 0, tab_ref.shape[1] - 1), 0)
               for o in range(-(nct - 1), nrt)}
        m_prev = m_ref[...]
        m_rows, p_rows = [], []
        for rb in range(r_ * nrt):
            r, rho = divmod(rb, nrt)
            rs = slice(rb * tb, (rb + 1) * tb)
            pieces = [s_ref[rs, c * tb:(c + 1) * tb] + tab_ref[r, idx[rho - c]] for c in range(nct)]
            mx = pieces[0]
            for c in range(1, nct):
                mx = jnp.maximum(mx, pieces[c])
            m_next = jnp.maximum(m_prev[rs], jnp.max(mx, axis=-1, keepdims=True))
            m_rows.append(m_next)
            p_rows.append(jnp.concatenate([jnp.exp2(pc_ - m_next).astype(BF) for pc_ in pieces], axis=1))
        m_next = jnp.concatenate(m_rows, axis=0)
        p = jnp.concatenate(p_rows, axis=0)
        alpha = jnp.exp2(m_prev - m_next)
        acc_ref[...] = (jnp.concatenate([alpha, alpha], axis=1) * acc_ref[...]
                        + jnp.dot(p, v_sc[pl.ds(off, tk), :], preferred_element_type=F32))
        m_ref[...] = m_next

    def flash_result(acc_ref):
        acc = acc_ref[...]
        return acc[:, :NSA_DH] / jnp.maximum(acc[:, NSA_DH:], 1e-30)

    wlast = (qi * tq + tq - 1) // NSA_TK_WIN
    wtiles = [wlast - (NSA_WIN_TILES - 1) + u for u in range(NSA_WIN_TILES)]
    wbufs = (wa_ref, wb_ref)

    def win_step(u):
        flash_step(wtiles[u], wbufs[u % 2], vwb, tw_ref, NSA_TK_WIN, mw_ref, accw_ref)

    lc = _mm_nt(qs, kc_ref[0, 0, 0]) + jnp.concatenate([tc_ref[r] for r in range(r_)], axis=0)
    mc = jnp.max(lc, axis=-1, keepdims=True)
    pc = jnp.exp2(lc - mc)
    lsum = jnp.sum(pc, axis=-1, keepdims=True)
    pc = pc * jnp.where(mc > 0.5 * NEG, 1.0 / jnp.maximum(lsum, 1e-30), 0.0)

    psum = pc[0:tq]
    for r in range(1, r_):
        psum = psum + pc[r * tq:(r + 1) * tq]
    s1, s2, s3 = _split3(psum)
    dn = (((1,), (1,)), ((), ()))
    ovl = ovl_ref[...]
    imp_t = (lax.dot_general(ovl, s1, dn, preferred_element_type=F32)
             + lax.dot_general(ovl, s2, dn, preferred_element_type=F32)
             + lax.dot_general(ovl, s3, dn, preferred_element_type=F32))

    flash_init(mw_ref, accw_ref)
    logits(wtiles[0], qs, kwb, NSA_TK_WIN, wbufs[0])
    logits(wtiles[1], qs, kwb, NSA_TK_WIN, wbufs[1])
    oc = _mm(pc, vc_ref[0, 0, 0])
    win_step(0)
    for u in range(2, NSA_WIN_TILES):
        logits(wtiles[u], qs, kwb, NSA_TK_WIN, wbufs[u % 2])
        win_step(u - 1)

    jb = lax.broadcasted_iota(jnp.int32, (nsel, tq), 0)
    tpos = qi * tq + lax.broadcasted_iota(jnp.int32, (nsel, tq), 1)
    tblk = tpos // SEL_BLOCK
    forced = (jb == 0) | (jb == tblk) | (jb == tblk - 1)
    score = jnp.where(jb <= tblk, imp_t + jnp.where(forced, FORCE_BONUS, 0.0), NEG)
    sub = lax.broadcasted_iota(jnp.int32, (8, tq), 0)
    groups = [score[8 * v:8 * v + 8] for v in range(nsel // 8)]
    counts = [jnp.zeros((8, tq), F32) for _ in groups]
    for jp in range(nsel):
        row = jnp.broadcast_to(score[jp:jp + 1, :], (8, tq))
        for v, grp in enumerate(groups):
            if 8 * v > jp:
                beats = jnp.where(row >= grp, 1.0, 0.0)
            elif 8 * v + 8 <= jp:
                beats = jnp.where(row > grp, 1.0, 0.0)
            else:
                tie = jnp.where(sub + 8 * v > jp, 1.0, 0.0)
                beats = jnp.where(row > grp, 1.0, jnp.where(row == grp, tie, 0.0))
            counts[v] = counts[v] + beats
    unsel_t = jnp.where(jnp.concatenate(counts, axis=0) < topn, 0.0, NEG)
    unsel_pad = jnp.concatenate([unsel_t, jnp.zeros((LANES - nsel, tq), F32)], axis=0).astype(BF)
    ri = lax.broadcasted_iota(jnp.int32, (tq, tq), 0)
    ci = lax.broadcasted_iota(jnp.int32, (tq, tq), 1)
    eye = jnp.where(ri == ci, 1.0, 0.0).astype(BF)
    unsel_q = lax.dot_general(eye, unsel_pad, dn, preferred_element_type=F32).astype(BF)
    for r in range(r_):
        qa_ref[r * tq:(r + 1) * tq, NSA_DH:] = unsel_q

    qa = qa_ref[...]
    slast = (qi * tq + tq - 1) // NSA_TK_SEL
    flash_init(ms_ref, accs_ref)
    logits(0, qa, ksb, NSA_TK_SEL, sa_ref)
    win_step(NSA_WIN_TILES - 1)

    sg = _sigmoid(sm_ref[0])

    def gate(r, br):
        c0 = SM_GATE + r * 3 + br
        c1 = SM_GATE + (r_ + r) * 3 + br
        return jnp.where(g == 0, sg[:, c0:c0 + 1], sg[:, c1:c1 + 1])

    o_w = flash_result(accw_ref)
    for r in range(r_):
        rs = slice(r * tq, (r + 1) * tq)
        accw_ref[rs, :NSA_DH] = gate(r, 0) * oc[rs] + gate(r, 2) * o_w[rs]

    def sel_step(kj, s_ref):
        flash_step(kj, s_ref, vsb, ts_ref, NSA_TK_SEL, ms_ref, accs_ref)

    def pair(i, carry):
        kj = 2 * i
        logits(kj + 1, qa, ksb, NSA_TK_SEL, sb_ref)
        sel_step(kj, sa_ref)
        logits(jnp.minimum(kj + 2, slast), qa, ksb, NSA_TK_SEL, sa_ref)
        sel_step(kj + 1, sb_ref)
        return carry

    lax.fori_loop(0, (slast + 1) // 2, pair, 0)

    @pl.when(slast % 2 == 0)
    def _():
        sel_step(slast, sa_ref)

    o_s = flash_result(accs_ref)
    sg = _sigmoid(sm_ref[0])
    for r in range(r_):
        rs = slice(r * tq, (r + 1) * tq)
        o_ref[r] = (accw_ref[rs, :NSA_DH] + gate(r, 1) * o_s[rs]).astype(o_ref.dtype)


def _nsa(p3, sm, ckv, tab_c, tab_s, tab_w, ovl_t, e_sel, qnw, knw, b_, s_):
    tq, tk, r_ = NSA_TQ, NSA_TB, NSA_REP
    nq = s_ // tq
    ncp = s_ // CMP_STRIDE
    nsel = s_ // SEL_BLOCK
    nd = tab_s.shape[1]
    nw = tab_w.shape[1]
    assert s_ % NSA_TK_SEL == 0 and NSA_TK_SEL // NSA_TB - 1 <= NSA_PAD
    assert NSA_TQ % NSA_TB == 0 and max(NSA_TQ, NSA_TK_WIN) % min(NSA_TQ, NSA_TK_WIN) == 0

    def kv_spec(base):
        return pl.BlockSpec((1, s_, LANES), lambda b, g, q, base=base: (base + g, b, 0))

    return pl.pallas_call(
        _nsa_body,
        grid=(b_, NSA_GROUPS, nq),
        in_specs=[pl.BlockSpec((r_, tq, LANES), lambda b, g, q: (CB_NQ // r_ + g, b * nq + q, 0)),
                  pl.BlockSpec((1, tq, LANES), lambda b, g, q: (0, b * nq + q, 0)),
                  pl.BlockSpec((1, 1, 1, ncp, NSA_DH), lambda b, g, q: (b, 0, g, 0, 0)),
                  pl.BlockSpec((1, 1, 1, ncp, NSA_DH), lambda b, g, q: (b, 1, g, 0, 0)),
                  kv_spec(CB_KS), kv_spec(CB_VS), kv_spec(CB_KW), kv_spec(CB_VW),
                  pl.BlockSpec((r_, tq, ncp), lambda b, g, q: (g, q, 0)),
                  pl.BlockSpec((r_, nd, tk, tk), lambda b, g, q: (g, 0, 0, 0)),
                  pl.BlockSpec((r_, nw, tk, tk), lambda b, g, q: (g, 0, 0, 0)),
                  pl.BlockSpec((nsel, ncp), lambda b, g, q: (0, 0)),
                  pl.BlockSpec((s_, LANES), lambda b, g, q: (0, 0)),
                  pl.BlockSpec((1, NSA_DH), lambda b, g, q: (0, 0)),
                  pl.BlockSpec((3, NSA_DH), lambda b, g, q: (0, 0))],
        out_specs=pl.BlockSpec((r_, tq, LANES), lambda b, g, q: (g, b * nq + q, 0)),
        out_shape=jax.ShapeDtypeStruct((NSA_HEADS, b_ * s_, LANES), BF),
        scratch_shapes=[pltpu.VMEM((s_, 2 * NSA_DH), BF), pltpu.VMEM((s_, 2 * NSA_DH), BF),
                        pltpu.VMEM((s_, NSA_DH), BF), pltpu.VMEM((s_, 2 * NSA_DH), BF),
                        pltpu.VMEM((r_ * tq, 2 * NSA_DH), BF),
                        pltpu.VMEM((r_ * tq, LANES), F32), pltpu.VMEM((r_ * tq, 2 * NSA_DH), F32),
                        pltpu.VMEM((r_ * tq, LANES), F32), pltpu.VMEM((r_ * tq, 2 * NSA_DH), F32),
                        pltpu.VMEM((r_ * tq, NSA_TK_SEL), F32), pltpu.VMEM((r_ * tq, NSA_TK_SEL), F32),
                        pltpu.VMEM((r_ * tq, NSA_TK_WIN), F32), pltpu.VMEM((r_ * tq, NSA_TK_WIN), F32)],
        compiler_params=pltpu.CompilerParams(dimension_semantics=("parallel", "arbitrary", "arbitrary")),
        name="nsa",
    )(p3, sm, ckv, ckv, p3, p3, p3, p3, tab_c, tab_s, tab_w, ovl_t, e_sel, qnw, knw)


def _merge_body(x_ref, ya_ref, yb_ref, ga_ref, gb_ref, wpa_ref, wpb_ref, wo_ref, o_ref):
    nh = ya_ref.shape[0]
    ya = jnp.concatenate([ya_ref[j] for j in range(nh)], axis=1)
    yb = jnp.concatenate([yb_ref[j] for j in range(nh)], axis=1)
    ga = _sigmoid(jnp.concatenate([ga_ref[j].astype(F32) for j in range(nh)], axis=1))
    gb = _sigmoid(jnp.concatenate([gb_ref[j].astype(F32) for j in range(nh)], axis=1))
    mixed = (ga * jnp.dot(ya, wpa_ref[...], preferred_element_type=F32)
             + gb * jnp.dot(yb, wpb_ref[...], preferred_element_type=F32))
    o_ref[...] = x_ref[...] + jnp.dot(mixed.astype(BF), wo_ref[...], preferred_element_type=F32)


def _merge(x2, ya, yb, p3, wpa, wpb, wo, tm):
    t, d = x2.shape
    nh = d // LANES
    hspec = pl.BlockSpec((nh, tm, LANES), lambda i: (0, i, 0))
    wspec = pl.BlockSpec((d, d), lambda i: (0, 0))
    return pl.pallas_call(
        _merge_body,
        grid=(t // tm,),
        in_specs=[pl.BlockSpec((tm, d), lambda i: (i, 0)), hspec, hspec,
                  pl.BlockSpec((nh, tm, LANES), lambda i: (CB_MGA // nh, i, 0)),
                  pl.BlockSpec((nh, tm, LANES), lambda i: (CB_MGB // nh, i, 0)),
                  wspec, wspec, wspec],
        out_specs=pl.BlockSpec((tm, d), lambda i: (i, 0)),
        out_shape=jax.ShapeDtypeStruct((t, d), F32),
        compiler_params=pltpu.CompilerParams(dimension_semantics=("parallel",)),
        name="merge",
    )(x2, ya, yb, p3, p3, wpa, wpb, wo)


def _ffn_body(x_ref, nw_ref, wg_ref, wu_ref, wd_ref, o_ref):
    x = x_ref[...]
    h = (x * lax.rsqrt(jnp.mean(x * x, axis=-1, keepdims=True) + NORM_EPS) * nw_ref[...]).astype(BF)
    gate = jnp.dot(h, wg_ref[...], preferred_element_type=F32)
    up = jnp.dot(h, wu_ref[...], preferred_element_type=F32)
    act = (_silu(gate) * up).astype(BF)
    o_ref[...] = x + jnp.dot(act, wd_ref[...], preferred_element_type=F32)


def _ffn(x2, norm_w, wg, wu, wd, tm):
    t, d = x2.shape
    f = wg.shape[1]
    return pl.pallas_call(
        _ffn_body,
        grid=(t // tm,),
        in_specs=[pl.BlockSpec((tm, d), lambda i: (i, 0)),
                  pl.BlockSpec((1, d), lambda i: (0, 0)),
                  pl.BlockSpec((d, f), lambda i: (0, 0)),
                  pl.BlockSpec((d, f), lambda i: (0, 0)),
                  pl.BlockSpec((f, d), lambda i: (0, 0))],
        out_specs=pl.BlockSpec((tm, d), lambda i: (i, 0)),
        out_shape=jax.ShapeDtypeStruct((t, d), F32),
        compiler_params=pltpu.CompilerParams(dimension_semantics=("parallel",)),
        name="ffn",
    )(x2, norm_w, wg, wu, wd)


def _arrange_w_in(w_in):
    o_ga = 4 * GDN_HEADS * GDN_DK
    o_gb = o_ga + GDN_HEADS
    o_nq = o_gb + GDN_HEADS
    o_nkv = o_nq + NSA_HEADS * NSA_DH
    o_ng = o_nkv + 6 * NSA_GROUPS * NSA_DH
    o_mg = o_ng + 3 * NSA_HEADS
    d = w_in.shape[0]
    small = jnp.concatenate([w_in[:, o_ga:o_nq], w_in[:, o_ng:o_mg],
                             jnp.zeros((d, 2 * LANES - (o_nq - o_ga) - (o_mg - o_ng)), w_in.dtype)], axis=1)
    return jnp.concatenate([w_in[:, :o_ga], w_in[:, o_nq:o_nkv], w_in[:, o_mg:], w_in[:, o_nkv:o_ng], small],
                           axis=1).astype(BF)


def _overlap_t(s_):
    ncp = s_ // CMP_STRIDE
    nsel = s_ // SEL_BLOCK
    cs = np.arange(ncp) * CMP_STRIDE
    ss = np.arange(nsel) * SEL_BLOCK
    ov = (cs[None, :] < ss[:, None] + SEL_BLOCK) & (cs[None, :] + CMP_BLOCK > ss[:, None])
    ov[:, ncp - 1] = False
    return jnp.asarray(ov.astype(np.float32), BF)


def _sel_expand(s_):
    assert s_ // SEL_BLOCK <= LANES
    pos = np.arange(s_)
    e = (pos[:, None] // SEL_BLOCK == np.arange(LANES)[None, :]).astype(np.float32)
    return jnp.asarray(e, BF)


def kernel(x, norm1_w, w_in, conv_w, a_log, dt_bias, gdn_norm_w, cmp_pe, cmp_w1, cmp_w2, q_norm_w, k_norm_w,
           rel_bias, w_proj_a, w_proj_b, w_out, norm2_w, w_gate, w_up, w_down):
    b_, s_, d = x.shape
    t = b_ * s_
    x2 = x.reshape(t, d)
    tab_c, tab_s, tab_w = _bias_tables(rel_bias, s_)
    ovl_t = _overlap_t(s_)
    e_sel = _sel_expand(s_)
    for l in range(norm1_w.shape[0]):
        p3, sm = _proj(x2, norm1_w[l][None, :], _arrange_w_in(w_in[l]), tm=min(1024, t), tn=10 * LANES)
        conv_w3 = conv_w[l].reshape(GDN_CONV, 3 * GDN_HEADS, LANES)
        alog_b = jnp.pad(a_log[l], (0, LANES - GDN_HEADS))[None, :]
        dtb_b = jnp.pad(dt_bias[l], (0, LANES - GDN_HEADS))[None, :]
        y_a = _gdn(p3, sm, conv_w3, alog_b, dtb_b, gdn_norm_w[l][None, :], b_, s_)
        pe2 = cmp_pe[l].reshape(2, 2, CMP_STRIDE * NSA_DH)
        ckv = _cmp(p3, pe2, cmp_w1[l].astype(BF), cmp_w2[l].astype(BF), k_norm_w[l][0:1], b_, s_)
        y_b = _nsa(p3, sm, ckv, tab_c, tab_s, tab_w, ovl_t, e_sel, q_norm_w[l][None, :], k_norm_w[l], b_, s_)
        x2 = _merge(x2, y_a, y_b, p3, w_proj_a[l].astype(BF), w_proj_b[l].astype(BF), w_out[l].astype(BF),
                    tm=min(512, t))
        x2 = _ffn(x2, norm2_w[l][None, :], w_gate[l].astype(BF), w_up[l].astype(BF), w_down[l].astype(BF),
                  tm=min(512, t))
    return x2.reshape(b_, s_, d)
```

```python
import functools
import math

import numpy as np
import jax
import jax.numpy as jnp
from jax import lax
from jax.experimental import pallas as pl
from jax.experimental.pallas import tpu as pltpu

F32 = jnp.float32
BF = jnp.bfloat16

LANES = 128
D_MODEL = 1024
GDN_HEADS = 8
GDN_DK = 128
GDN_DV = 128
GDN_CONV = 4
GDN_CHUNK = 64
NSA_HEADS = 8
NSA_GROUPS = 2
NSA_REP = NSA_HEADS // NSA_GROUPS
NSA_DH = 128
CMP_BLOCK = 32
CMP_STRIDE = 16
CMP_HIDDEN = 256
SEL_BLOCK = 64
SEL_TOPN = 16
WINDOW = 512
FORCE_BONUS = 1000.0
REL_BUCKETS = 32
REL_MAX_DIST = 1024
FFN_HIDDEN = 2816
NORM_EPS = 1e-6
NEG = -1e30
M_INIT = -3e38
LOG2E = 1.4426950408889634

CB_GQ, CB_GK, CB_GV, CB_GZ = 0, 8, 16, 24
CB_NQ = 32
CB_MGA, CB_MGB = 40, 48
CB_KC, CB_VC, CB_KS, CB_VS, CB_KW, CB_VW = 56, 58, 60, 62, 64, 66
CB_SMALL = 68
N_CB = 70
SM_A, SM_B, SM_GATE = 0, 8, 16

GDN_ROWS = 256
NSA_TQ = 256
NSA_TB = 128
NSA_TK_SEL = 512
NSA_TK_WIN = 256
NSA_WIN_TILES = (WINDOW + max(NSA_TQ, NSA_TK_WIN) - 2) // NSA_TK_WIN + 1
NSA_PAD = NSA_TK_SEL // NSA_TB - 1


def _mm(a, b):
    return jnp.dot(a.astype(BF), b.astype(BF), preferred_element_type=F32)


def _mm_nt(a, b):
    return lax.dot_general(a.astype(BF), b.astype(BF), (((1,), (1,)), ((), ())),
                           preferred_element_type=F32)


def _mm_tn(a, b):
    return lax.dot_general(a.astype(BF), b.astype(BF), (((0,), (0,)), ((), ())),
                           preferred_element_type=F32)


def _split3(x):
    x1 = x.astype(BF)
    r1 = x - x1.astype(F32)
    x2 = r1.astype(BF)
    x3 = (r1 - x2.astype(F32)).astype(BF)
    return x1, x2, x3


def _sigmoid(x):
    return 0.5 * jnp.tanh(0.5 * x) + 0.5


def _silu_of_half(h):
    return h + h * jnp.tanh(h)


def _silu(x):
    return _silu_of_half(0.5 * x)


def _softplus(x):
    return jnp.maximum(x, 0.0) + jnp.log(1.0 + jnp.exp(-jnp.abs(x)))


def _rel_thresholds():
    d = np.arange(0, 4 * REL_MAX_DIST, dtype=np.int64)
    max_exact = REL_BUCKETS // 2
    d_f = np.maximum(d, 1).astype(np.float32)
    large = max_exact + (np.log(d_f / np.float32(max_exact)) / np.float32(math.log(REL_MAX_DIST / max_exact))
                         * np.float32(REL_BUCKETS - max_exact)).astype(np.int32)
    large = np.minimum(large, REL_BUCKETS - 1)
    bucket = np.where(d < max_exact, d, large)
    assert np.all(np.diff(bucket) >= 0)
    return [int(np.argmax(bucket >= k)) for k in range(REL_BUCKETS)]


REL_THR = _rel_thresholds()


def _proj_body(x_ref, nw_ref, w_ref, o_ref, sm_ref, h_ref):
    @pl.when(pl.program_id(1) == 0)
    def _():
        x = x_ref[...]
        y = x * lax.rsqrt(jnp.mean(x * x, axis=-1, keepdims=True) + NORM_EPS)
        h_ref[...] = (y * nw_ref[...]).astype(BF)

    nb = o_ref.shape[0]
    r = jnp.dot(h_ref[...], w_ref[...], preferred_element_type=F32)
    for j in range(nb):
        o_ref[j] = r[:, j * LANES:(j + 1) * LANES].astype(BF)

    @pl.when(pl.program_id(1) == CB_SMALL // nb)
    def _():
        sm_ref[0] = r[:, (CB_SMALL % nb) * LANES:(CB_SMALL % nb + 1) * LANES]


def _proj(x2, norm_w, w_all, tm, tn):
    t, d = x2.shape
    n = w_all.shape[1]
    nb = tn // LANES
    return pl.pallas_call(
        _proj_body,
        grid=(t // tm, n // tn),
        in_specs=[pl.BlockSpec((tm, d), lambda i, j: (i, 0)),
                  pl.BlockSpec((1, d), lambda i, j: (0, 0)),
                  pl.BlockSpec((d, tn), lambda i, j: (0, j))],
        out_specs=[pl.BlockSpec((nb, tm, LANES), lambda i, j: (j, i, 0)),
                   pl.BlockSpec((1, tm, LANES), lambda i, j: (0, i, 0))],
        out_shape=[jax.ShapeDtypeStruct((n // LANES, t, LANES), BF),
                   jax.ShapeDtypeStruct((1, t, LANES), F32)],
        scratch_shapes=[pltpu.VMEM((tm, d), BF)],
        compiler_params=pltpu.CompilerParams(dimension_semantics=("parallel", "arbitrary")),
        name="proj",
    )(x2, norm_w, w_all)


def _gdn_body(q_ref, k_ref, v_ref, z_ref, sm_ref, cw_ref, alog_ref, dtb_ref, nw_ref, o_ref,
              ext_ref, st_ref):
    rows = GDN_ROWS
    c = GDN_CHUNK
    nchunk = rows // c
    s = pl.program_id(1)

    @pl.when(s == 0)
    def _():
        ext_ref[:, 0:8, :] = jnp.zeros((3 * GDN_HEADS, 8, LANES), F32)
        st_ref[...] = jnp.zeros_like(st_ref)

    for j in range(GDN_HEADS):
        ext_ref[j, 8:8 + rows, :] = q_ref[j].astype(F32)
        ext_ref[GDN_HEADS + j, 8:8 + rows, :] = k_ref[j].astype(F32)
        ext_ref[2 * GDN_HEADS + j, 8:8 + rows, :] = v_ref[j].astype(F32)

    ri = lax.broadcasted_iota(jnp.int32, (rows, rows), 0)
    ci = lax.broadcasted_iota(jnp.int32, (rows, rows), 1)
    l_tril = jnp.where(((ri // c) == (ci // c)) & (ri >= ci), 1.0, 0.0).astype(BF)
    rt = lax.broadcasted_iota(jnp.int32, (LANES, LANES), 0)
    ct = lax.broadcasted_iota(jnp.int32, (LANES, LANES), 1)
    same_t = (rt // c) == (ct // c)
    tril_t = same_t & (rt >= ct)
    strict_t = same_t & (rt > ct)
    eye = jnp.where(rt == ct, 1.0, 0.0)

    sm = sm_ref[0]
    lane = lax.broadcasted_iota(jnp.int32, (rows, LANES), 1)
    gall = jnp.where(lane < GDN_HEADS, -jnp.exp(alog_ref[...]) * _softplus(sm + dtb_ref[...]), 0.0)
    g1 = gall.astype(BF).astype(F32)
    r1 = gall - g1
    g2 = r1.astype(BF).astype(F32)
    packed = g1 + pltpu.roll(g2, GDN_HEADS, 1) + pltpu.roll(r1 - g2, 2 * GDN_HEADS, 1)
    gc = jnp.dot(l_tril, packed.astype(BF), preferred_element_type=F32)
    gcum_all = gc + pltpu.roll(gc, LANES - GDN_HEADS, 1) + pltpu.roll(gc, LANES - 2 * GDN_HEADS, 1)

    def conv_silu(j):
        acc = (0.5 * cw_ref[0, pl.ds(j, 1), :]) * ext_ref[j, pl.ds(5, rows), :]
        for i in range(1, GDN_CONV):
            acc = acc + (0.5 * cw_ref[i, pl.ds(j, 1), :]) * ext_ref[j, pl.ds(5 + i, rows), :]
        return _silu_of_half(acc)

    def head_setup(h):
        qh = conv_silu(h)
        kh = conv_silu(GDN_HEADS + h)
        vv = conv_silu(2 * GDN_HEADS + h)
        qn = qh * lax.rsqrt(jnp.sum(qh * qh, axis=-1, keepdims=True) + NORM_EPS) * (GDN_DK ** -0.5)
        kn = kh * lax.rsqrt(jnp.sum(kh * kh, axis=-1, keepdims=True) + NORM_EPS)

        beta = _sigmoid(jnp.broadcast_to(sm[:, SM_B + h:SM_B + h + 1], (rows, LANES)))
        gcum = jnp.broadcast_to(gcum_all[:, h:h + 1], (rows, LANES))
        glast = jnp.concatenate(
            [jnp.broadcast_to(gcum[(n + 1) * c - 1:(n + 1) * c, :], (c, LANES)) for n in range(nchunk)],
            axis=0)
        gct = gcum.T
        kb = kn * beta
        eg = jnp.exp(gcum)
        knb = kn.astype(BF)
        a_t, intra_t = [], []
        for u in range(rows // LANES):
            rs = slice(u * LANES, (u + 1) * LANES)
            diff = gcum[rs] - gct[:, rs]
            decay = jnp.where(tril_t, jnp.exp(jnp.where(tril_t, diff, 0.0)), 0.0)
            a_t.append(jnp.where(strict_t, _mm_nt(kb[rs], knb[rs]) * decay, 0.0))
            intra_t.append(_mm_nt(qn[rs], knb[rs]) * decay)
        return dict(
            a=a_t, intra=intra_t,
            rhs=jnp.concatenate([vv * beta, kb * eg], axis=1),
            qg=qn * eg, kdec=kn * jnp.exp(glast - gcum), cd=jnp.exp(glast))

    def all_heads():
        hs = list(range(GDN_HEADS))
        w, t = [], []
        nt = rows // LANES
        for grp in (hs[:GDN_HEADS // 2], hs[GDN_HEADS // 2:]):
            wg = [head_setup(h) for h in grp]
            a_all = [a_ for d in wg for a_ in d["a"]]
            p = [_mm(a_, a_) for a_ in a_all]
            tg = [eye - a_ for a_ in a_all]
            for j in range(1, 6):
                tp = [_mm(ti, pi) for ti, pi in zip(tg, p)]
                if j < 5:
                    p = [_mm(pi, pi) for pi in p]
                tg = [ti + tpi for ti, tpi in zip(tg, tp)]
            w += wg
            t += [tg[i * nt:(i + 1) * nt] for i in range(len(wg))]
        sol = [[d["rhs"][u * LANES:(u + 1) * LANES] + _mm(tu - eye, d["rhs"][u * LANES:(u + 1) * LANES])
                for u, tu in enumerate(ti)] for d, ti in zip(w, t)]
        st = [st_ref[h] for h in hs]
        outs = [[] for _ in hs]
        cpt = LANES // c
        for n in range(nchunk):
            sl = slice(n * c, (n + 1) * c)
            lo = slice((n % cpt) * c, (n % cpt + 1) * c)
            ks = [_mm(jnp.concatenate([s_[n // cpt][lo, GDN_DV:], d["qg"][sl]], axis=0), si)
                  for s_, d, si in zip(sol, w, st)]
            vn = [s_[n // cpt][lo, :GDN_DV] - k_[:c] for s_, k_ in zip(sol, ks)]
            for u, (d, k_, v_) in enumerate(zip(w, ks, vn)):
                outs[u].append(k_[c:] + _mm(d["intra"][n // cpt][lo, lo], v_))
            st = [si * jnp.concatenate([d["cd"][sl], d["cd"][sl]], axis=0) + _mm_tn(d["kdec"][sl], v_)
                  for si, d, v_ in zip(st, w, vn)]
        for u, h in enumerate(hs):
            st_ref[h] = st[u]
            o = jnp.concatenate(outs[u], axis=0)
            on = o * lax.rsqrt(jnp.mean(o * o, axis=-1, keepdims=True) + NORM_EPS) * nw_ref[...]
            o_ref[h] = (on * _silu(z_ref[h].astype(F32))).astype(o_ref.dtype)

    all_heads()

    for j in range(3 * GDN_HEADS):
        ext_ref[j, 0:8, :] = ext_ref[j, rows:rows + 8, :]


def _gdn(p3, sm, conv_w3, alog_b, dtb_b, gdn_norm_w, b_, s_):
    rows = GDN_ROWS
    ns = s_ // rows
    hb = GDN_HEADS

    def cb(base):
        return pl.BlockSpec((hb, rows, LANES), lambda b, s, base=base: (base // hb, b * ns + s, 0))

    return pl.pallas_call(
        _gdn_body,
        grid=(b_, ns),
        in_specs=[cb(CB_GQ), cb(CB_GK), cb(CB_GV), cb(CB_GZ),
                  pl.BlockSpec((1, rows, LANES), lambda b, s: (0, b * ns + s, 0)),
                  pl.BlockSpec((GDN_CONV, 3 * hb, LANES), lambda b, s: (0, 0, 0)),
                  pl.BlockSpec((1, LANES), lambda b, s: (0, 0)),
                  pl.BlockSpec((1, LANES), lambda b, s: (0, 0)),
                  pl.BlockSpec((1, LANES), lambda b, s: (0, 0))],
        out_specs=pl.BlockSpec((hb, rows, LANES), lambda b, s: (0, b * ns + s, 0)),
        out_shape=jax.ShapeDtypeStruct((hb, b_ * s_, LANES), BF),
        scratch_shapes=[pltpu.VMEM((3 * hb, rows + 8, LANES), F32),
                        pltpu.VMEM((hb, GDN_DK, GDN_DV), F32)],
        compiler_params=pltpu.CompilerParams(dimension_semantics=("parallel", "arbitrary")),
        name="gdn",
    )(p3, p3, p3, p3, sm, conv_w3, alog_b, dtb_b, gdn_norm_w)


def _cmp_body(x_ref, pe_ref, w1_ref, w2_ref, nw_ref, o_ref, c_ref, xf_ref):
    kv = pl.program_id(1)
    nch = c_ref.shape[0]
    half = CMP_STRIDE * NSA_DH
    xf_ref[...] = x_ref[0].astype(F32)
    for p in range(CMP_STRIDE):
        c_ref[:, p * NSA_DH:(p + 1) * NSA_DH] = xf_ref[pl.ds(p, nch, stride=CMP_STRIDE), :]
    cc = c_ref[...]
    u = _mm(cc + pe_ref[0, 0:1, :], w1_ref[0, 0:half, :])
    v = _mm(cc + pe_ref[0, 1:2, :], w1_ref[0, half:2 * half, :])
    v_next = jnp.concatenate([v[1:], v[:1]], axis=0)
    hid = _silu(u + v_next)
    out = _mm(hid, w2_ref[0])
    normed = out * lax.rsqrt(jnp.mean(out * out, axis=-1, keepdims=True) + NORM_EPS) * nw_ref[...]
    o_ref[0, 0, 0] = jnp.where(kv == 0, normed, out)


def _cmp(p3, pe2, w1, w2, knw0, b_, s_):
    nch = s_ // CMP_STRIDE
    g_ = NSA_GROUPS
    return pl.pallas_call(
        _cmp_body,
        grid=(b_, 2, g_),
        in_specs=[pl.BlockSpec((1, s_, LANES), lambda b, kv, g: (CB_KC + 2 * kv + g, b, 0)),
                  pl.BlockSpec((1, 2, CMP_STRIDE * NSA_DH), lambda b, kv, g: (kv, 0, 0)),
                  pl.BlockSpec((1, CMP_BLOCK * NSA_DH, CMP_HIDDEN), lambda b, kv, g: (kv, 0, 0)),
                  pl.BlockSpec((1, CMP_HIDDEN, NSA_DH), lambda b, kv, g: (kv, 0, 0)),
                  pl.BlockSpec((1, NSA_DH), lambda b, kv, g: (0, 0))],
        out_specs=pl.BlockSpec((1, 1, 1, nch, NSA_DH), lambda b, kv, g: (b, kv, g, 0, 0)),
        out_shape=jax.ShapeDtypeStruct((b_, 2, g_, nch, NSA_DH), F32),
        scratch_shapes=[pltpu.VMEM((nch, CMP_STRIDE * NSA_DH), F32), pltpu.VMEM((s_, LANES), F32)],
        compiler_params=pltpu.CompilerParams(dimension_semantics=("parallel", "arbitrary", "arbitrary")),
        name="cmp",
    )(p3, pe2, w1, w2, knw0)


def _bias_of(d, rb_ref, h):
    val = jnp.full(d.shape, rb_ref[0, h], F32)
    for k in range(1, REL_BUCKETS):
        val = jnp.where(d >= REL_THR[k], rb_ref[k, h], val)
    return val * LOG2E


def _bias_body(rb_ref, tc_ref, ts_ref, tw_ref):
    h = pl.program_id(0)
    _, s_, ncp = tc_ref.shape

    def row_tile(it, carry):
        r0 = pl.multiple_of(it * NSA_TQ, NSA_TQ)
        t = r0 + lax.broadcasted_iota(jnp.int32, (NSA_TQ, ncp), 0)
        n = lax.broadcasted_iota(jnp.int32, (NSA_TQ, ncp), 1)
        d = t - (n * CMP_STRIDE + CMP_BLOCK - 1)
        tc_ref[0, pl.ds(r0, NSA_TQ), :] = jnp.where(d >= 0, _bias_of(d, rb_ref, h), NEG)
        return carry

    lax.fori_loop(0, s_ // NSA_TQ, row_tile, 0)
    i = lax.broadcasted_iota(jnp.int32, (NSA_TB, NSA_TB), 0)
    j = lax.broadcasted_iota(jnp.int32, (NSA_TB, NSA_TB), 1)
    for e in range(ts_ref.shape[1]):
        d = (e - NSA_PAD) * NSA_TB + i - j
        ts_ref[0, e] = jnp.where(d >= 0, _bias_of(d, rb_ref, h), NEG)
    for e in range(tw_ref.shape[1]):
        d = (e - NSA_PAD) * NSA_TB + i - j
        tw_ref[0, e] = jnp.where((d >= 0) & (d < WINDOW), _bias_of(d, rb_ref, h), NEG)


def _sel_table_len():
    a = 0
    while a * NSA_TB - (NSA_TB - 1) < REL_THR[REL_BUCKETS - 1]:
        a += 1
    return a + 1 + NSA_PAD


def _win_table_len():
    return (WINDOW + NSA_TB - 1) // NSA_TB + 2 + NSA_PAD


def _bias_tables(rel_bias, s_):
    ncp = s_ // CMP_STRIDE
    nd = _sel_table_len()
    nw = _win_table_len()
    return pl.pallas_call(
        _bias_body,
        grid=(NSA_HEADS,),
        in_specs=[pl.BlockSpec(memory_space=pltpu.SMEM)],
        out_specs=[pl.BlockSpec((1, s_, ncp), lambda h: (h, 0, 0)),
                   pl.BlockSpec((1, nd, NSA_TB, NSA_TB), lambda h: (h, 0, 0, 0)),
                   pl.BlockSpec((1, nw, NSA_TB, NSA_TB), lambda h: (h, 0, 0, 0))],
        out_shape=[jax.ShapeDtypeStruct((NSA_HEADS, s_, ncp), F32),
                   jax.ShapeDtypeStruct((NSA_HEADS, nd, NSA_TB, NSA_TB), F32),
                   jax.ShapeDtypeStruct((NSA_HEADS, nw, NSA_TB, NSA_TB), F32)],
        compiler_params=pltpu.CompilerParams(dimension_semantics=("parallel",)),
        name="bias",
    )(rel_bias)


def _rms_rows(x, w):
    return x * lax.rsqrt(jnp.mean(x * x, axis=-1, keepdims=True) + NORM_EPS) * w


def _nsa_body(q_ref, sm_ref, kc_ref, vc_ref, ks_ref, vs_ref, kw_ref, vw_ref, tc_ref, ts_ref, tw_ref,
              ovl_ref, e_ref, qnw_ref, knw_ref, o_ref,
              ksb, vsb, kwb, vwb, qa_ref, ms_ref, accs_ref, park_ref, sa_ref, sb_ref, wa_ref, wb_ref, wc_ref):
    tq, r_, tb = NSA_TQ, NSA_REP, NSA_TB
    g = pl.program_id(1)
    qi = pl.program_id(2)
    nsel = ovl_ref.shape[0]
    topn = min(SEL_TOPN, nsel)

    @pl.when(qi == 0)
    def _():
        ones = jnp.ones(vs_ref.shape[1:], BF)
        ksb[:, :NSA_DH] = _rms_rows(ks_ref[0].astype(F32), knw_ref[1:2, :]).astype(BF)
        ksb[:, NSA_DH:] = e_ref[...]
        kwb[...] = _rms_rows(kw_ref[0].astype(F32), knw_ref[2:3, :]).astype(BF)
        vsb[:, :NSA_DH] = vs_ref[0]
        vsb[:, NSA_DH:] = ones
        vwb[:, :NSA_DH] = vw_ref[0]
        vwb[:, NSA_DH:] = ones

    qscale = NSA_DH ** -0.5 * LOG2E
    for r in range(r_):
        qa_ref[r * tq:(r + 1) * tq, :NSA_DH] = (
            _rms_rows(q_ref[r].astype(F32), qnw_ref[...]) * qscale).astype(BF)
    qs = qa_ref[:, :NSA_DH]

    def flash_init(m_ref, acc_ref):
        m_ref[...] = jnp.full(m_ref.shape, M_INIT, F32)
        acc_ref[...] = jnp.zeros_like(acc_ref)

    def logits(kj, q, k_sc, tk, dst_ref):
        off = pl.multiple_of(kj * tk, tk)
        dst_ref[:, :tk] = _mm_nt(q, k_sc[pl.ds(off, tk), :])

    def flash_step(kj, s_ref, v_sc, tab_ref, tk, m_ref, acc_ref):
        nct = tk // tb
        nrt = tq // tb
        off = pl.multiple_of(kj * tk, tk)
        e0 = qi * nrt - kj * nct + NSA_PAD
        idx = {o: jnp.clip(e0 + o, 0, tab_ref.shape[1] - 1) for o in range(-(nct - 1), nrt)}
        m_prev = m_ref[...]
        m_rows, p_rows = [], []
        for rb in range(r_ * nrt):
            r, rho = divmod(rb, nrt)
            rs = slice(rb * tb, (rb + 1) * tb)
            pieces = [s_ref[rs, c * tb:(c + 1) * tb] + tab_ref[r, idx[rho - c]] for c in range(nct)]
            mx = pieces[0]
            for c in range(1, nct):
                mx = jnp.maximum(mx, pieces[c])
            m_next = jnp.maximum(m_prev[rs], jnp.max(mx, axis=-1, keepdims=True))
            m_rows.append(m_next)
            p_rows.append(jnp.concatenate([jnp.exp2(pc_ - m_next).astype(BF) for pc_ in pieces], axis=1))
        m_next = jnp.concatenate(m_rows, axis=0)
        p = jnp.concatenate(p_rows, axis=0)
        alpha = jnp.exp2(m_prev - m_next)
        acc_ref[...] = (jnp.concatenate([alpha, alpha], axis=1) * acc_ref[...]
                        + jnp.dot(p, v_sc[pl.ds(off, tk), :], preferred_element_type=F32))
        m_ref[...] = m_next

    def flash_result(acc_ref):
        acc = acc_ref[...]
        return acc[:, :NSA_DH] / jnp.maximum(acc[:, NSA_DH:], 1e-30)

    wfirst = jnp.maximum((qi * tq + tq - 1) // NSA_TK_WIN - (NSA_WIN_TILES - 1), 0)
    wbufs = (wa_ref, wb_ref, wc_ref)

    def window_output():
        nct = NSA_TK_WIN // tb
        nrt = tq // tb
        wk = NSA_WIN_TILES * NSA_TK_WIN
        idx = {(u, o): jnp.clip(qi * nrt - (wfirst + u) * nct + NSA_PAD + o, 0, tw_ref.shape[1] - 1)
               for u in range(NSA_WIN_TILES) for o in range(-(nct - 1), nrt)}
        p_rows = []
        for rb in range(r_ * nrt):
            r, rho = divmod(rb, nrt)
            rs = slice(rb * tb, (rb + 1) * tb)
            pieces = [wbufs[u][rs, c * tb:(c + 1) * tb] + tw_ref[r, idx[(u, rho - c)]]
                      for u in range(NSA_WIN_TILES) for c in range(nct)]
            mx = pieces[0]
            for pc_ in pieces[1:]:
                mx = jnp.maximum(mx, pc_)
            m = jnp.max(mx, axis=-1, keepdims=True)
            p_rows.append(jnp.concatenate([jnp.exp2(pc_ - m).astype(BF) for pc_ in pieces], axis=1))
        p = jnp.concatenate(p_rows, axis=0)
        off = pl.multiple_of(wfirst * NSA_TK_WIN, NSA_TK_WIN)
        pv = jnp.dot(p, vwb[pl.ds(off, wk), :], preferred_element_type=F32)
        return pv[:, :NSA_DH] / jnp.maximum(pv[:, NSA_DH:], 1e-30)

    lc = _mm_nt(qs, kc_ref[0, 0, 0]) + jnp.concatenate([tc_ref[r] for r in range(r_)], axis=0)
    mc = jnp.max(lc, axis=-1, keepdims=True)
    pc = jnp.exp2(lc - mc)
    lsum = jnp.sum(pc, axis=-1, keepdims=True)
    pc = pc * jnp.where(mc > 0.5 * NEG, 1.0 / jnp.maximum(lsum, 1e-30), 0.0)

    psum = pc[0:tq]
    for r in range(1, r_):
        psum = psum + pc[r * tq:(r + 1) * tq]
    s1, s2, s3 = _split3(psum)
    dn = (((1,), (1,)), ((), ()))
    ovl = ovl_ref[...]
    imp_t = (lax.dot_general(ovl, s1, dn, preferred_element_type=F32)
             + lax.dot_general(ovl, s2, dn, preferred_element_type=F32)
             + lax.dot_general(ovl, s3, dn, preferred_element_type=F32))

    for u in range(NSA_WIN_TILES):
        logits(wfirst + u, qs, kwb, NSA_TK_WIN, wbufs[u])
    oc = _mm(pc, vc_ref[0, 0, 0])
    o_w = window_output()

    sg = _sigmoid(sm_ref[0])

    def gate(r, br):
        c0 = SM_GATE + r * 3 + br
        c1 = SM_GATE + (r_ + r) * 3 + br
        return jnp.where(g == 0, sg[:, c0:c0 + 1], sg[:, c1:c1 + 1])

    for r in range(r_):
        rs = slice(r * tq, (r + 1) * tq)
        park_ref[rs, :] = gate(r, 0) * oc[rs] + gate(r, 2) * o_w[rs]

    jb = lax.broadcasted_iota(jnp.int32, (nsel, tq), 0)
    tpos = qi * tq + lax.broadcasted_iota(jnp.int32, (nsel, tq), 1)
    tblk = tpos // SEL_BLOCK
    forced = (jb == 0) | (jb == tblk) | (jb == tblk - 1)
    score = jnp.where(jb <= tblk, imp_t + jnp.where(forced, FORCE_BONUS, 0.0), NEG)
    sub = lax.broadcasted_iota(jnp.int32, (8, tq), 0)
    groups = [score[8 * v:8 * v + 8] for v in range(nsel // 8)]
    counts = [jnp.zeros((8, tq), F32) for _ in groups]
    for jp in range(nsel):
        row = jnp.broadcast_to(score[jp:jp + 1, :], (8, tq))
        for v, grp in enumerate(groups):
            if 8 * v > jp:
                beats = jnp.where(row >= grp, 1.0, 0.0)
            elif 8 * v + 8 <= jp:
                beats = jnp.where(row > grp, 1.0, 0.0)
            else:
                tie = jnp.where(sub + 8 * v > jp, 1.0, 0.0)
                beats = jnp.where(row > grp, 1.0, jnp.where(row == grp, tie, 0.0))
            counts[v] = counts[v] + beats
    unsel_t = jnp.where(jnp.concatenate(counts, axis=0) < topn, 0.0, NEG)
    unsel_pad = jnp.concatenate([unsel_t, jnp.zeros((LANES - nsel, tq), F32)], axis=0).astype(BF)
    ri = lax.broadcasted_iota(jnp.int32, (tq, tq), 0)
    ci = lax.broadcasted_iota(jnp.int32, (tq, tq), 1)
    eye = jnp.where(ri == ci, 1.0, 0.0).astype(BF)
    unsel_q = lax.dot_general(eye, unsel_pad, dn, preferred_element_type=F32).astype(BF)
    for r in range(r_):
        qa_ref[r * tq:(r + 1) * tq, NSA_DH:] = unsel_q

    qa = qa_ref[...]
    slast = (qi * tq + tq - 1) // NSA_TK_SEL
    flash_init(ms_ref, accs_ref)
    logits(0, qa, ksb, NSA_TK_SEL, sa_ref)

    def sel_step(kj, s_ref):
        flash_step(kj, s_ref, vsb, ts_ref, NSA_TK_SEL, ms_ref, accs_ref)

    def pair(i, carry):
        kj = 2 * i
        logits(kj + 1, qa, ksb, NSA_TK_SEL, sb_ref)
        sel_step(kj, sa_ref)
        logits(jnp.minimum(kj + 2, slast), qa, ksb, NSA_TK_SEL, sa_ref)
        sel_step(kj + 1, sb_ref)
        return carry

    lax.fori_loop(0, (slast + 1) // 2, pair, 0)

    @pl.when(slast % 2 == 0)
    def _():
        sel_step(slast, sa_ref)

    o_s = flash_result(accs_ref)
    sg = _sigmoid(sm_ref[0])
    for r in range(r_):
        rs = slice(r * tq, (r + 1) * tq)
        o_ref[r] = (park_ref[rs, :] + gate(r, 1) * o_s[rs]).astype(o_ref.dtype)


def _nsa(p3, sm, ckv, tab_c, tab_s, tab_w, ovl_t, e_sel, qnw, knw, b_, s_):
    tq, tk, r_ = NSA_TQ, NSA_TB, NSA_REP
    nq = s_ // tq
    ncp = s_ // CMP_STRIDE
    nsel = s_ // SEL_BLOCK
    nd = tab_s.shape[1]
    nw = tab_w.shape[1]
    assert s_ % NSA_TK_SEL == 0 and NSA_TK_SEL // NSA_TB - 1 <= NSA_PAD
    assert s_ >= NSA_WIN_TILES * NSA_TK_WIN and NSA_WIN_TILES == 3
    assert NSA_TQ % NSA_TB == 0 and max(NSA_TQ, NSA_TK_WIN) % min(NSA_TQ, NSA_TK_WIN) == 0

    def kv_spec(base):
        return pl.BlockSpec((1, s_, LANES), lambda b, g, q, base=base: (base + g, b, 0))

    return pl.pallas_call(
        _nsa_body,
        grid=(b_, NSA_GROUPS, nq),
        in_specs=[pl.BlockSpec((r_, tq, LANES), lambda b, g, q: (CB_NQ // r_ + g, b * nq + q, 0)),
                  pl.BlockSpec((1, tq, LANES), lambda b, g, q: (0, b * nq + q, 0)),
                  pl.BlockSpec((1, 1, 1, ncp, NSA_DH), lambda b, g, q: (b, 0, g, 0, 0)),
                  pl.BlockSpec((1, 1, 1, ncp, NSA_DH), lambda b, g, q: (b, 1, g, 0, 0)),
                  kv_spec(CB_KS), kv_spec(CB_VS), kv_spec(CB_KW), kv_spec(CB_VW),
                  pl.BlockSpec((r_, tq, ncp), lambda b, g, q: (g, q, 0)),
                  pl.BlockSpec((r_, nd, tk, tk), lambda b, g, q: (g, 0, 0, 0)),
                  pl.BlockSpec((r_, nw, tk, tk), lambda b, g, q: (g, 0, 0, 0)),
                  pl.BlockSpec((nsel, ncp), lambda b, g, q: (0, 0)),
                  pl.BlockSpec((s_, LANES), lambda b, g, q: (0, 0)),
                  pl.BlockSpec((1, NSA_DH), lambda b, g, q: (0, 0)),
                  pl.BlockSpec((3, NSA_DH), lambda b, g, q: (0, 0))],
        out_specs=pl.BlockSpec((r_, tq, LANES), lambda b, g, q: (g, b * nq + q, 0)),
        out_shape=jax.ShapeDtypeStruct((NSA_HEADS, b_ * s_, LANES), BF),
        scratch_shapes=[pltpu.VMEM((s_, 2 * NSA_DH), BF), pltpu.VMEM((s_, 2 * NSA_DH), BF),
                        pltpu.VMEM((s_, NSA_DH), BF), pltpu.VMEM((s_, 2 * NSA_DH), BF),
                        pltpu.VMEM((r_ * tq, 2 * NSA_DH), BF),
                        pltpu.VMEM((r_ * tq, LANES), F32), pltpu.VMEM((r_ * tq, 2 * NSA_DH), F32),
                        pltpu.VMEM((r_ * tq, NSA_DH), F32),
                        pltpu.VMEM((r_ * tq, NSA_TK_SEL), F32), pltpu.VMEM((r_ * tq, NSA_TK_SEL), F32)]
        + [pltpu.VMEM((r_ * tq, NSA_TK_WIN), F32)] * NSA_WIN_TILES,
        compiler_params=pltpu.CompilerParams(dimension_semantics=("parallel", "arbitrary", "arbitrary")),
        name="nsa",
    )(p3, sm, ckv, ckv, p3, p3, p3, p3, tab_c, tab_s, tab_w, ovl_t, e_sel, qnw, knw)


def _merge_body(x_ref, ya_ref, yb_ref, ga_ref, gb_ref, wpa_ref, wpb_ref, wo_ref, o_ref):
    nh = ya_ref.shape[0]
    ya = jnp.concatenate([ya_ref[j] for j in range(nh)], axis=1)
    yb = jnp.concatenate([yb_ref[j] for j in range(nh)], axis=1)
    ga = _sigmoid(jnp.concatenate([ga_ref[j].astype(F32) for j in range(nh)], axis=1))
    gb = _sigmoid(jnp.concatenate([gb_ref[j].astype(F32) for j in range(nh)], axis=1))
    mixed = (ga * jnp.dot(ya, wpa_ref[...], preferred_element_type=F32)
             + gb * jnp.dot(yb, wpb_ref[...], preferred_element_type=F32))
    o_ref[...] = x_ref[...] + jnp.dot(mixed.astype(BF), wo_ref[...], preferred_element_type=F32)


def _merge(x2, ya, yb, p3, wpa, wpb, wo, tm):
    t, d = x2.shape
    nh = d // LANES
    hspec = pl.BlockSpec((nh, tm, LANES), lambda i: (0, i, 0))
    wspec = pl.BlockSpec((d, d), lambda i: (0, 0))
    return pl.pallas_call(
        _merge_body,
        grid=(t // tm,),
        in_specs=[pl.BlockSpec((tm, d), lambda i: (i, 0)), hspec, hspec,
                  pl.BlockSpec((nh, tm, LANES), lambda i: (CB_MGA // nh, i, 0)),
                  pl.BlockSpec((nh, tm, LANES), lambda i: (CB_MGB // nh, i, 0)),
                  wspec, wspec, wspec],
        out_specs=pl.BlockSpec((tm, d), lambda i: (i, 0)),
        out_shape=jax.ShapeDtypeStruct((t, d), F32),
        compiler_params=pltpu.CompilerParams(dimension_semantics=("parallel",)),
        name="merge",
    )(x2, ya, yb, p3, p3, wpa, wpb, wo)


def _ffn_body(x_ref, nw_ref, wg_ref, wu_ref, wd_ref, o_ref):
    x = x_ref[...]
    h = (x * lax.rsqrt(jnp.mean(x * x, axis=-1, keepdims=True) + NORM_EPS) * nw_ref[...]).astype(BF)
    gate = jnp.dot(h, wg_ref[...], preferred_element_type=F32)
    up = jnp.dot(h, wu_ref[...], preferred_element_type=F32)
    act = (_silu(gate) * up).astype(BF)
    o_ref[...] = x + jnp.dot(act, wd_ref[...], preferred_element_type=F32)


def _ffn(x2, norm_w, wg, wu, wd, tm):
    t, d = x2.shape
    f = wg.shape[1]
    return pl.pallas_call(
        _ffn_body,
        grid=(t // tm,),
        in_specs=[pl.BlockSpec((tm, d), lambda i: (i, 0)),
                  pl.BlockSpec((1, d), lambda i: (0, 0)),
                  pl.BlockSpec((d, f), lambda i: (0, 0)),
                  pl.BlockSpec((d, f), lambda i: (0, 0)),
                  pl.BlockSpec((f, d), lambda i: (0, 0))],
        out_specs=pl.BlockSpec((tm, d), lambda i: (i, 0)),
        out_shape=jax.ShapeDtypeStruct((t, d), F32),
        compiler_params=pltpu.CompilerParams(dimension_semantics=("parallel",)),
        name="ffn",
    )(x2, norm_w, wg, wu, wd)


def _arrange_w_in(w_in):
    o_ga = 4 * GDN_HEADS * GDN_DK
    o_gb = o_ga + GDN_HEADS
    o_nq = o_gb + GDN_HEADS
    o_nkv = o_nq + NSA_HEADS * NSA_DH
    o_ng = o_nkv + 6 * NSA_GROUPS * NSA_DH
    o_mg = o_ng + 3 * NSA_HEADS
    d = w_in.shape[0]
    small = jnp.concatenate([w_in[:, o_ga:o_nq], w_in[:, o_ng:o_mg],
                             jnp.zeros((d, 2 * LANES - (o_nq - o_ga) - (o_mg - o_ng)), w_in.dtype)], axis=1)
    return jnp.concatenate([w_in[:, :o_ga], w_in[:, o_nq:o_nkv], w_in[:, o_mg:], w_in[:, o_nkv:o_ng], small],
                           axis=1).astype(BF)


def _overlap_t(s_):
    ncp = s_ // CMP_STRIDE
    nsel = s_ // SEL_BLOCK
    cs = np.arange(ncp) * CMP_STRIDE
    ss = np.arange(nsel) * SEL_BLOCK
    ov = (cs[None, :] < ss[:, None] + SEL_BLOCK) & (cs[None, :] + CMP_BLOCK > ss[:, None])
    ov[:, ncp - 1] = False
    return jnp.asarray(ov.astype(np.float32), BF)


def _sel_expand(s_):
    assert s_ // SEL_BLOCK <= LANES
    pos = np.arange(s_)
    e = (pos[:, None] // SEL_BLOCK == np.arange(LANES)[None, :]).astype(np.float32)
    return jnp.asarray(e, BF)


def kernel(x, norm1_w, w_in, conv_w, a_log, dt_bias, gdn_norm_w, cmp_pe, cmp_w1, cmp_w2, q_norm_w, k_norm_w,
           rel_bias, w_proj_a, w_proj_b, w_out, norm2_w, w_gate, w_up, w_down):
    b_, s_, d = x.shape
    t = b_ * s_
    x2 = x.reshape(t, d)
    tab_c, tab_s, tab_w = _bias_tables(rel_bias, s_)
    ovl_t = _overlap_t(s_)
    e_sel = _sel_expand(s_)
    for l in range(norm1_w.shape[0]):
        p3, sm = _proj(x2, norm1_w[l][None, :], _arrange_w_in(w_in[l]), tm=min(1024, t), tn=10 * LANES)
        conv_w3 = conv_w[l].reshape(GDN_CONV, 3 * GDN_HEADS, LANES)
        alog_b = jnp.pad(a_log[l], (0, LANES - GDN_HEADS))[None, :]
        dtb_b = jnp.pad(dt_bias[l], (0, LANES - GDN_HEADS))[None, :]
        y_a = _gdn(p3, sm, conv_w3, alog_b, dtb_b, gdn_norm_w[l][None, :], b_, s_)
        pe2 = cmp_pe[l].reshape(2, 2, CMP_STRIDE * NSA_DH)
        ckv = _cmp(p3, pe2, cmp_w1[l].astype(BF), cmp_w2[l].astype(BF), k_norm_w[l][0:1], b_, s_)
        y_b = _nsa(p3, sm, ckv, tab_c, tab_s, tab_w, ovl_t, e_sel, q_norm_w[l][None, :], k_norm_w[l], b_, s_)
        x2 = _merge(x2, y_a, y_b, p3, w_proj_a[l].astype(BF), w_proj_b[l].astype(BF), w_out[l].astype(BF),
                    tm=min(512, t))
        x2 = _ffn(x2, norm2_w[l][None, :], w_gate[l].astype(BF), w_up[l].astype(BF), w_down[l].astype(BF),
                  tm=min(512, t))
    return x2.reshape(b_, s_, d)
```

```python
import functools
import math

import numpy as np
import jax
import jax.numpy as jnp
from jax import lax
from jax.experimental import pallas as pl
from jax.experimental.pallas import tpu as pltpu

F32 = jnp.float32
BF = jnp.bfloat16

LANES = 128
D_MODEL = 1024
GDN_HEADS = 8
GDN_DK = 128
GDN_DV = 128
GDN_CONV = 4
GDN_CHUNK = 64
NSA_HEADS = 8
NSA_GROUPS = 2
NSA_REP = NSA_HEADS // NSA_GROUPS
NSA_DH = 128
CMP_BLOCK = 32
CMP_STRIDE = 16
CMP_HIDDEN = 256
SEL_BLOCK = 64
SEL_TOPN = 16
WINDOW = 512
FORCE_BONUS = 1000.0
REL_BUCKETS = 32
REL_MAX_DIST = 1024
FFN_HIDDEN = 2816
NORM_EPS = 1e-6
NEG = -1e30
M_INIT = -3e38
LOG2E = 1.4426950408889634

CB_GQ, CB_GK, CB_GV, CB_GZ = 0, 8, 16, 24
CB_NQ = 32
CB_MGA, CB_MGB = 40, 48
CB_KC, CB_VC, CB_KS, CB_VS, CB_KW, CB_VW = 56, 58, 60, 62, 64, 66
CB_SMALL = 68
N_CB = 70
SM_A, SM_B, SM_GATE = 0, 8, 16

GDN_ROWS = 256
NSA_TQ = 256
NSA_TB = 128
NSA_TK_SEL = 512
NSA_TK_WIN = 256
NSA_WIN_TILES = (WINDOW + max(NSA_TQ, NSA_TK_WIN) - 2) // NSA_TK_WIN + 1
NSA_PAD = NSA_TK_SEL // NSA_TB - 1


def _mm(a, b):
    return jnp.dot(a.astype(BF), b.astype(BF), preferred_element_type=F32)


def _mm_nt(a, b):
    return lax.dot_general(a.astype(BF), b.astype(BF), (((1,), (1,)), ((), ())),
                           preferred_element_type=F32)


def _mm_tn(a, b):
    return lax.dot_general(a.astype(BF), b.astype(BF), (((0,), (0,)), ((), ())),
                           preferred_element_type=F32)


def _split3(x):
    x1 = x.astype(BF)
    r1 = x - x1.astype(F32)
    x2 = r1.astype(BF)
    x3 = (r1 - x2.astype(F32)).astype(BF)
    return x1, x2, x3


def _sigmoid(x):
    return 0.5 * jnp.tanh(0.5 * x) + 0.5


def _silu_of_half(h):
    return h + h * jnp.tanh(h)


def _silu(x):
    return _silu_of_half(0.5 * x)


def _softplus(x):
    return jnp.maximum(x, 0.0) + jnp.log(1.0 + jnp.exp(-jnp.abs(x)))


def _rel_thresholds():
    d = np.arange(0, 4 * REL_MAX_DIST, dtype=np.int64)
    max_exact = REL_BUCKETS // 2
    d_f = np.maximum(d, 1).astype(np.float32)
    large = max_exact + (np.log(d_f / np.float32(max_exact)) / np.float32(math.log(REL_MAX_DIST / max_exact))
                         * np.float32(REL_BUCKETS - max_exact)).astype(np.int32)
    large = np.minimum(large, REL_BUCKETS - 1)
    bucket = np.where(d < max_exact, d, large)
    assert np.all(np.diff(bucket) >= 0)
    return [int(np.argmax(bucket >= k)) for k in range(REL_BUCKETS)]


REL_THR = _rel_thresholds()


def _proj_body(x_ref, nw_ref, w_ref, o_ref, sm_ref, h_ref):
    @pl.when(pl.program_id(1) == 0)
    def _():
        x = x_ref[...]
        y = x * lax.rsqrt(jnp.mean(x * x, axis=-1, keepdims=True) + NORM_EPS)
        h_ref[...] = (y * nw_ref[...]).astype(BF)

    nb = o_ref.shape[0]
    r = jnp.dot(h_ref[...], w_ref[...], preferred_element_type=F32)
    for j in range(nb):
        o_ref[j] = r[:, j * LANES:(j + 1) * LANES].astype(BF)

    @pl.when(pl.program_id(1) == CB_SMALL // nb)
    def _():
        sm_ref[0] = r[:, (CB_SMALL % nb) * LANES:(CB_SMALL % nb + 1) * LANES]


def _proj(x2, norm_w, w_all, tm, tn):
    t, d = x2.shape
    n = w_all.shape[1]
    nb = tn // LANES
    return pl.pallas_call(
        _proj_body,
        grid=(t // tm, n // tn),
        in_specs=[pl.BlockSpec((tm, d), lambda i, j: (i, 0)),
                  pl.BlockSpec((1, d), lambda i, j: (0, 0)),
                  pl.BlockSpec((d, tn), lambda i, j: (0, j))],
        out_specs=[pl.BlockSpec((nb, tm, LANES), lambda i, j: (j, i, 0)),
                   pl.BlockSpec((1, tm, LANES), lambda i, j: (0, i, 0))],
        out_shape=[jax.ShapeDtypeStruct((n // LANES, t, LANES), BF),
                   jax.ShapeDtypeStruct((1, t, LANES), F32)],
        scratch_shapes=[pltpu.VMEM((tm, d), BF)],
        compiler_params=pltpu.CompilerParams(dimension_semantics=("parallel", "arbitrary")),
        name="proj",
    )(x2, norm_w, w_all)


def _gdn_body(q_ref, k_ref, v_ref, z_ref, sm_ref, cw_ref, alog_ref, dtb_ref, nw_ref, o_ref,
              ext_ref, st_ref):
    rows = GDN_ROWS
    c = GDN_CHUNK
    nchunk = rows // c
    s = pl.program_id(1)

    @pl.when(s == 0)
    def _():
        ext_ref[:, 0:8, :] = jnp.zeros((3 * GDN_HEADS, 8, LANES), F32)
        st_ref[...] = jnp.zeros_like(st_ref)

    for j in range(GDN_HEADS):
        ext_ref[j, 8:8 + rows, :] = q_ref[j].astype(F32)
        ext_ref[GDN_HEADS + j, 8:8 + rows, :] = k_ref[j].astype(F32)
        ext_ref[2 * GDN_HEADS + j, 8:8 + rows, :] = v_ref[j].astype(F32)

    ri = lax.broadcasted_iota(jnp.int32, (rows, rows), 0)
    ci = lax.broadcasted_iota(jnp.int32, (rows, rows), 1)
    l_tril = jnp.where(((ri // c) == (ci // c)) & (ri >= ci), 1.0, 0.0).astype(BF)
    rt = lax.broadcasted_iota(jnp.int32, (LANES, LANES), 0)
    ct = lax.broadcasted_iota(jnp.int32, (LANES, LANES), 1)
    same_t = (rt // c) == (ct // c)
    tril_t = same_t & (rt >= ct)
    strict_t = same_t & (rt > ct)
    eye = jnp.where(rt == ct, 1.0, 0.0)

    sm = sm_ref[0]
    lane = lax.broadcasted_iota(jnp.int32, (rows, LANES), 1)
    gall = jnp.where(lane < GDN_HEADS, -jnp.exp(alog_ref[...]) * _softplus(sm + dtb_ref[...]), 0.0)
    g1 = gall.astype(BF).astype(F32)
    r1 = gall - g1
    g2 = r1.astype(BF).astype(F32)
    packed = g1 + pltpu.roll(g2, GDN_HEADS, 1) + pltpu.roll(r1 - g2, 2 * GDN_HEADS, 1)
    gc = jnp.dot(l_tril, packed.astype(BF), preferred_element_type=F32)
    gcum_all = gc + pltpu.roll(gc, LANES - GDN_HEADS, 1) + pltpu.roll(gc, LANES - 2 * GDN_HEADS, 1)

    def conv_silu(j):
        acc = (0.5 * cw_ref[0, pl.ds(j, 1), :]) * ext_ref[j, pl.ds(5, rows), :]
        for i in range(1, GDN_CONV):
            acc = acc + (0.5 * cw_ref[i, pl.ds(j, 1), :]) * ext_ref[j, pl.ds(5 + i, rows), :]
        return _silu_of_half(acc)

    def head_setup(h):
        qh = conv_silu(h)
        kh = conv_silu(GDN_HEADS + h)
        vv = conv_silu(2 * GDN_HEADS + h)
        qn = qh * lax.rsqrt(jnp.sum(qh * qh, axis=-1, keepdims=True) + NORM_EPS) * (GDN_DK ** -0.5)
        kn = kh * lax.rsqrt(jnp.sum(kh * kh, axis=-1, keepdims=True) + NORM_EPS)

        beta = _sigmoid(jnp.broadcast_to(sm[:, SM_B + h:SM_B + h + 1], (rows, LANES)))
        gcum = jnp.broadcast_to(gcum_all[:, h:h + 1], (rows, LANES))
        glast = jnp.concatenate(
            [jnp.broadcast_to(gcum[(n + 1) * c - 1:(n + 1) * c, :], (c, LANES)) for n in range(nchunk)],
            axis=0)
        gct = gcum.T
        kb = kn * beta
        eg = jnp.exp(gcum)
        knb = kn.astype(BF)
        a_t, intra_t = [], []
        for u in range(rows // LANES):
            rs = slice(u * LANES, (u + 1) * LANES)
            diff = gcum[rs] - gct[:, rs]
            decay = jnp.where(tril_t, jnp.exp(jnp.where(tril_t, diff, 0.0)), 0.0)
            a_t.append(jnp.where(strict_t, _mm_nt(kb[rs], knb[rs]) * decay, 0.0))
            intra_t.append(_mm_nt(qn[rs], knb[rs]) * decay)
        return dict(
            a=a_t, intra=intra_t,
            rhs=jnp.concatenate([vv * beta, kb * eg], axis=1),
            qg=qn * eg, kdec=kn * jnp.exp(glast - gcum), cd=jnp.exp(glast))

    def all_heads():
        hs = list(range(GDN_HEADS))
        w, t = [], []
        nt = rows // LANES
        for grp in (hs[:GDN_HEADS // 2], hs[GDN_HEADS // 2:]):
            wg = [head_setup(h) for h in grp]
            a_all = [a_ for d in wg for a_ in d["a"]]
            p = [_mm(a_, a_) for a_ in a_all]
            tg = [eye - a_ for a_ in a_all]
            for j in range(1, 6):
                tp = [_mm(ti, pi) for ti, pi in zip(tg, p)]
                if j < 5:
                    p = [_mm(pi, pi) for pi in p]
                tg = [ti + tpi for ti, tpi in zip(tg, tp)]
            w += wg
            t += [tg[i * nt:(i + 1) * nt] for i in range(len(wg))]
        sol = [[d["rhs"][u * LANES:(u + 1) * LANES] + _mm(tu - eye, d["rhs"][u * LANES:(u + 1) * LANES])
                for u, tu in enumerate(ti)] for d, ti in zip(w, t)]
        st = [st_ref[h] for h in hs]
        outs = [[] for _ in hs]
        cpt = LANES // c
        for n in range(nchunk):
            sl = slice(n * c, (n + 1) * c)
            lo = slice((n % cpt) * c, (n % cpt + 1) * c)
            ks = [_mm(jnp.concatenate([s_[n // cpt][lo, GDN_DV:], d["qg"][sl]], axis=0), si)
                  for s_, d, si in zip(sol, w, st)]
            vn = [s_[n // cpt][lo, :GDN_DV] - k_[:c] for s_, k_ in zip(sol, ks)]
            for u, (d, k_, v_) in enumerate(zip(w, ks, vn)):
                outs[u].append(k_[c:] + _mm(d["intra"][n // cpt][lo, lo], v_))
            st = [si * jnp.concatenate([d["cd"][sl], d["cd"][sl]], axis=0) + _mm_tn(d["kdec"][sl], v_)
                  for si, d, v_ in zip(st, w, vn)]
        for u, h in enumerate(hs):
            st_ref[h] = st[u]
            o = jnp.concatenate(outs[u], axis=0)
            on = o * lax.rsqrt(jnp.mean(o * o, axis=-1, keepdims=True) + NORM_EPS) * nw_ref[...]
            o_ref[h] = (on * _silu(z_ref[h].astype(F32))).astype(o_ref.dtype)

    all_heads()

    for j in range(3 * GDN_HEADS):
        ext_ref[j, 0:8, :] = ext_ref[j, rows:rows + 8, :]


def _gdn(p3, sm, conv_w3, alog_b, dtb_b, gdn_norm_w, b_, s_):
    rows = GDN_ROWS
    ns = s_ // rows
    hb = GDN_HEADS

    def cb(base):
        return pl.BlockSpec((hb, rows, LANES), lambda b, s, base=base: (base // hb, b * ns + s, 0))

    return pl.pallas_call(
        _gdn_body,
        grid=(b_, ns),
        in_specs=[cb(CB_GQ), cb(CB_GK), cb(CB_GV), cb(CB_GZ),
                  pl.BlockSpec((1, rows, LANES), lambda b, s: (0, b * ns + s, 0)),
                  pl.BlockSpec((GDN_CONV, 3 * hb, LANES), lambda b, s: (0, 0, 0)),
                  pl.BlockSpec((1, LANES), lambda b, s: (0, 0)),
                  pl.BlockSpec((1, LANES), lambda b, s: (0, 0)),
                  pl.BlockSpec((1, LANES), lambda b, s: (0, 0))],
        out_specs=pl.BlockSpec((hb, rows, LANES), lambda b, s: (0, b * ns + s, 0)),
        out_shape=jax.ShapeDtypeStruct((hb, b_ * s_, LANES), BF),
        scratch_shapes=[pltpu.VMEM((3 * hb, rows + 8, LANES), F32),
                        pltpu.VMEM((hb, GDN_DK, GDN_DV), F32)],
        compiler_params=pltpu.CompilerParams(dimension_semantics=("parallel", "arbitrary")),
        name="gdn",
    )(p3, p3, p3, p3, sm, conv_w3, alog_b, dtb_b, gdn_norm_w)


def _cmp_body(x_ref, pe_ref, w1_ref, w2_ref, nw_ref, o_ref, c_ref, xf_ref):
    kv = pl.program_id(1)
    nch = c_ref.shape[0]
    half = CMP_STRIDE * NSA_DH
    xf_ref[...] = x_ref[0].astype(F32)
    for p in range(CMP_STRIDE):
        c_ref[:, p * NSA_DH:(p + 1) * NSA_DH] = xf_ref[pl.ds(p, nch, stride=CMP_STRIDE), :]
    cc = c_ref[...]
    u = _mm(cc + pe_ref[0, 0:1, :], w1_ref[0, 0:half, :])
    v = _mm(cc + pe_ref[0, 1:2, :], w1_ref[0, half:2 * half, :])
    v_next = jnp.concatenate([v[1:], v[:1]], axis=0)
    hid = _silu(u + v_next)
    out = _mm(hid, w2_ref[0])
    normed = out * lax.rsqrt(jnp.mean(out * out, axis=-1, keepdims=True) + NORM_EPS) * nw_ref[...]
    o_ref[0, 0, 0] = jnp.where(kv == 0, normed, out)


def _cmp(p3, pe2, w1, w2, knw0, b_, s_):
    nch = s_ // CMP_STRIDE
    g_ = NSA_GROUPS
    return pl.pallas_call(
        _cmp_body,
        grid=(b_, 2, g_),
        in_specs=[pl.BlockSpec((1, s_, LANES), lambda b, kv, g: (CB_KC + 2 * kv + g, b, 0)),
                  pl.BlockSpec((1, 2, CMP_STRIDE * NSA_DH), lambda b, kv, g: (kv, 0, 0)),
                  pl.BlockSpec((1, CMP_BLOCK * NSA_DH, CMP_HIDDEN), lambda b, kv, g: (kv, 0, 0)),
                  pl.BlockSpec((1, CMP_HIDDEN, NSA_DH), lambda b, kv, g: (kv, 0, 0)),
                  pl.BlockSpec((1, NSA_DH), lambda b, kv, g: (0, 0))],
        out_specs=pl.BlockSpec((1, 1, 1, nch, NSA_DH), lambda b, kv, g: (b, kv, g, 0, 0)),
        out_shape=jax.ShapeDtypeStruct((b_, 2, g_, nch, NSA_DH), F32),
        scratch_shapes=[pltpu.VMEM((nch, CMP_STRIDE * NSA_DH), F32), pltpu.VMEM((s_, LANES), F32)],
        compiler_params=pltpu.CompilerParams(dimension_semantics=("parallel", "arbitrary", "arbitrary")),
        name="cmp",
    )(p3, pe2, w1, w2, knw0)


def _bias_of(d, rb_ref, h):
    val = jnp.full(d.shape, rb_ref[0, h], F32)
    for k in range(1, REL_BUCKETS):
        val = jnp.where(d >= REL_THR[k], rb_ref[k, h], val)
    return val * LOG2E


def _bias_body(rb_ref, tc_ref, ts_ref, tw_ref):
    h = pl.program_id(0)
    _, s_, ncp = tc_ref.shape

    r = lax.broadcasted_iota(jnp.int32, (CMP_STRIDE, 2 * ncp), 0)
    k = lax.broadcasted_iota(jnp.int32, (CMP_STRIDE, 2 * ncp), 1)
    d = CMP_STRIDE * (ncp - 1 - k) + r - (CMP_BLOCK - 1)
    gen = jnp.where(d >= 0, _bias_of(d, rb_ref, h), NEG)

    def row_group(a, carry):
        row0 = pl.multiple_of(a * CMP_STRIDE, CMP_STRIDE)
        tc_ref[0, pl.ds(row0, CMP_STRIDE), :] = pltpu.roll(gen, (ncp + 1 + a) % (2 * ncp), 1)[:, :ncp]
        return carry

    lax.fori_loop(0, s_ // CMP_STRIDE, row_group, 0, unroll=8)
    i = lax.broadcasted_iota(jnp.int32, (NSA_TB, NSA_TB), 0)
    j = lax.broadcasted_iota(jnp.int32, (NSA_TB, NSA_TB), 1)
    for e in range(ts_ref.shape[1]):
        d = (e - NSA_PAD) * NSA_TB + i - j
        ts_ref[0, e] = jnp.where(d >= 0, _bias_of(d, rb_ref, h), NEG)
    for e in range(tw_ref.shape[1]):
        d = (e - NSA_PAD) * NSA_TB + i - j
        tw_ref[0, e] = jnp.where((d >= 0) & (d < WINDOW), _bias_of(d, rb_ref, h), NEG)


def _sel_table_len():
    a = 0
    while a * NSA_TB - (NSA_TB - 1) < REL_THR[REL_BUCKETS - 1]:
        a += 1
    return a + 1 + NSA_PAD


def _win_table_len():
    return (WINDOW + NSA_TB - 1) // NSA_TB + 2 + NSA_PAD


def _bias_tables(rel_bias, s_):
    ncp = s_ // CMP_STRIDE
    nd = _sel_table_len()
    nw = _win_table_len()
    return pl.pallas_call(
        _bias_body,
        grid=(NSA_HEADS,),
        in_specs=[pl.BlockSpec(memory_space=pltpu.SMEM)],
        out_specs=[pl.BlockSpec((1, s_, ncp), lambda h: (h, 0, 0)),
                   pl.BlockSpec((1, nd, NSA_TB, NSA_TB), lambda h: (h, 0, 0, 0)),
                   pl.BlockSpec((1, nw, NSA_TB, NSA_TB), lambda h: (h, 0, 0, 0))],
        out_shape=[jax.ShapeDtypeStruct((NSA_HEADS, s_, ncp), F32),
                   jax.ShapeDtypeStruct((NSA_HEADS, nd, NSA_TB, NSA_TB), F32),
                   jax.ShapeDtypeStruct((NSA_HEADS, nw, NSA_TB, NSA_TB), F32)],
        compiler_params=pltpu.CompilerParams(dimension_semantics=("parallel",)),
        name="bias",
    )(rel_bias)


def _rms_rows(x, w):
    return x * lax.rsqrt(jnp.mean(x * x, axis=-1, keepdims=True) + NORM_EPS) * w


def _nsa_body(q_ref, sm_ref, kc_ref, vc_ref, ks_ref, vs_ref, kw_ref, vw_ref, tc_ref, ts_ref, tw_ref,
              ovl_ref, e_ref, qnw_ref, knw_ref, o_ref,
              ksb, vsb, kwb, vwb, qa_ref, ms_ref, accs_ref, park_ref, sa_ref, sb_ref, wa_ref, wb_ref, wc_ref):
    tq, r_, tb = NSA_TQ, NSA_REP, NSA_TB
    g = pl.program_id(1)
    qi = pl.program_id(2)
    nsel = ovl_ref.shape[0]
    topn = min(SEL_TOPN, nsel)

    @pl.when(qi == 0)
    def _():
        ones = jnp.ones(vs_ref.shape[1:], BF)
        ksb[:, :NSA_DH] = _rms_rows(ks_ref[0].astype(F32), knw_ref[1:2, :]).astype(BF)
        ksb[:, NSA_DH:] = e_ref[...]
        kwb[...] = _rms_rows(kw_ref[0].astype(F32), knw_ref[2:3, :]).astype(BF)
        vsb[:, :NSA_DH] = vs_ref[0]
        vsb[:, NSA_DH:] = ones
        vwb[:, :NSA_DH] = vw_ref[0]
        vwb[:, NSA_DH:] = ones

    qscale = NSA_DH ** -0.5 * LOG2E
    for r in range(r_):
        qa_ref[r * tq:(r + 1) * tq, :NSA_DH] = (
            _rms_rows(q_ref[r].astype(F32), qnw_ref[...]) * qscale).astype(BF)
    qs = qa_ref[:, :NSA_DH]

    def logits(kj, q, k_sc, tk, dst_ref):
        off = pl.multiple_of(kj * tk, tk)
        dst_ref[:, :tk] = _mm_nt(q, k_sc[pl.ds(off, tk), :])

    def flash_init(m_ref, acc_ref):
        m_ref[...] = jnp.full(m_ref.shape, M_INIT, F32)
        acc_ref[...] = jnp.zeros_like(acc_ref)

    def flash_step(kj, s_ref, v_sc, tab_ref, tk, m_ref, acc_ref):
        nct = tk // tb
        nrt = tq // tb
        off = pl.multiple_of(kj * tk, tk)
        e0 = qi * nrt - kj * nct + NSA_PAD
        idx = {o: jnp.clip(e0 + o, 0, tab_ref.shape[1] - 1) for o in range(-(nct - 1), nrt)}
        m_prev = m_ref[...]
        m_rows, p_rows = [], []
        for rb in range(r_ * nrt):
            r, rho = divmod(rb, nrt)
            rs = slice(rb * tb, (rb + 1) * tb)
            pieces = [s_ref[rs, c * tb:(c + 1) * tb] + tab_ref[r, idx[rho - c]] for c in range(nct)]
            mx = pieces[0]
            for c in range(1, nct):
                mx = jnp.maximum(mx, pieces[c])
            m_next = jnp.maximum(m_prev[rs], jnp.max(mx, axis=-1, keepdims=True))
            m_rows.append(m_next)
            p_rows.append(jnp.concatenate([jnp.exp2(pc_ - m_next).astype(BF) for pc_ in pieces], axis=1))
        m_next = jnp.concatenate(m_rows, axis=0)
        p = jnp.concatenate(p_rows, axis=0)
        alpha = jnp.exp2(m_prev - m_next)
        acc_ref[...] = (jnp.concatenate([alpha, alpha], axis=1) * acc_ref[...]
                        + jnp.dot(p, v_sc[pl.ds(off, tk), :], preferred_element_type=F32))
        m_ref[...] = m_next

    def flash_result(acc_ref):
        acc = acc_ref[...]
        return acc[:, :NSA_DH] / jnp.maximum(acc[:, NSA_DH:], 1e-30)

    wfirst = jnp.maximum((qi * tq + tq - 1) // NSA_TK_WIN - (NSA_WIN_TILES - 1), 0)
    wbufs = (wa_ref, wb_ref, wc_ref)

    def window_output():
        nct = NSA_TK_WIN // tb
        nrt = tq // tb
        wk = NSA_WIN_TILES * NSA_TK_WIN
        idx = {(u, o): jnp.clip(qi * nrt - (wfirst + u) * nct + NSA_PAD + o, 0, tw_ref.shape[1] - 1)
               for u in range(NSA_WIN_TILES) for o in range(-(nct - 1), nrt)}
        p_rows = []
        for rb in range(r_ * nrt):
            r, rho = divmod(rb, nrt)
            rs = slice(rb * tb, (rb + 1) * tb)
            pieces = [wbufs[u][rs, c * tb:(c + 1) * tb] + tw_ref[r, idx[(u, rho - c)]]
                      for u in range(NSA_WIN_TILES) for c in range(nct)]
            mx = pieces[0]
            for pc_ in pieces[1:]:
                mx = jnp.maximum(mx, pc_)
            m = jnp.max(mx, axis=-1, keepdims=True)
            p_rows.append(jnp.concatenate([jnp.exp2(pc_ - m).astype(BF) for pc_ in pieces], axis=1))
        p = jnp.concatenate(p_rows, axis=0)
        off = pl.multiple_of(wfirst * NSA_TK_WIN, NSA_TK_WIN)
        pv = jnp.dot(p, vwb[pl.ds(off, wk), :], preferred_element_type=F32)
        return pv[:, :NSA_DH] / jnp.maximum(pv[:, NSA_DH:], 1e-30)

    lc = _mm_nt(qs, kc_ref[0, 0, 0]) + jnp.concatenate([tc_ref[r] for r in range(r_)], axis=0)
    mc = jnp.max(lc, axis=-1, keepdims=True)
    pc = jnp.exp2(lc - mc)
    lsum = jnp.sum(pc, axis=-1, keepdims=True)
    pc = pc * jnp.where(mc > 0.5 * NEG, 1.0 / jnp.maximum(lsum, 1e-30), 0.0)

    psum = pc[0:tq]
    for r in range(1, r_):
        psum = psum + pc[r * tq:(r + 1) * tq]
    s1, s2, s3 = _split3(psum)
    dn = (((1,), (1,)), ((), ()))
    ovl = ovl_ref[...]
    imp_t = (lax.dot_general(ovl, s1, dn, preferred_element_type=F32)
             + lax.dot_general(ovl, s2, dn, preferred_element_type=F32)
             + lax.dot_general(ovl, s3, dn, preferred_element_type=F32))

    for u in range(NSA_WIN_TILES):
        logits(wfirst + u, qs, kwb, NSA_TK_WIN, wbufs[u])
    oc = _mm(pc, vc_ref[0, 0, 0])
    o_w = window_output()

    sg = _sigmoid(sm_ref[0])

    def gate(r, br):
        c0 = SM_GATE + r * 3 + br
        c1 = SM_GATE + (r_ + r) * 3 + br
        return jnp.where(g == 0, sg[:, c0:c0 + 1], sg[:, c1:c1 + 1])

    for r in range(r_):
        rs = slice(r * tq, (r + 1) * tq)
        park_ref[rs, :] = gate(r, 0) * oc[rs] + gate(r, 2) * o_w[rs]

    jb = lax.broadcasted_iota(jnp.int32, (nsel, tq), 0)
    tpos = qi * tq + lax.broadcasted_iota(jnp.int32, (nsel, tq), 1)
    tblk = tpos // SEL_BLOCK
    forced = (jb == 0) | (jb == tblk) | (jb == tblk - 1)
    score = jnp.where(jb <= tblk, imp_t + jnp.where(forced, FORCE_BONUS, 0.0), NEG)
    sub = lax.broadcasted_iota(jnp.int32, (8, tq), 0)
    groups = [score[8 * v:8 * v + 8] for v in range(nsel // 8)]
    counts = [jnp.zeros((8, tq), F32) for _ in groups]
    for jp in range(nsel):
        row = jnp.broadcast_to(score[jp:jp + 1, :], (8, tq))
        for v, grp in enumerate(groups):
            if 8 * v > jp:
                beats = jnp.where(row >= grp, 1.0, 0.0)
            elif 8 * v + 8 <= jp:
                beats = jnp.where(row > grp, 1.0, 0.0)
            else:
                tie = jnp.where(sub + 8 * v > jp, 1.0, 0.0)
                beats = jnp.where(row > grp, 1.0, jnp.where(row == grp, tie, 0.0))
            counts[v] = counts[v] + beats
    unsel_t = jnp.where(jnp.concatenate(counts, axis=0) < topn, 0.0, NEG)
    unsel_pad = jnp.concatenate([unsel_t, jnp.zeros((LANES - nsel, tq), F32)], axis=0).astype(BF)
    ri = lax.broadcasted_iota(jnp.int32, (tq, tq), 0)
    ci = lax.broadcasted_iota(jnp.int32, (tq, tq), 1)
    eye = jnp.where(ri == ci, 1.0, 0.0).astype(BF)
    unsel_q = lax.dot_general(eye, unsel_pad, dn, preferred_element_type=F32).astype(BF)
    for r in range(r_):
        qa_ref[r * tq:(r + 1) * tq, NSA_DH:] = unsel_q

    qa = qa_ref[...]
    slast = (qi * tq + tq - 1) // NSA_TK_SEL
    flash_init(ms_ref, accs_ref)
    logits(0, qa, ksb, NSA_TK_SEL, sa_ref)

    def sel_step(kj, s_ref):
        flash_step(kj, s_ref, vsb, ts_ref, NSA_TK_SEL, ms_ref, accs_ref)

    def pair(i, carry):
        kj = 2 * i
        logits(kj + 1, qa, ksb, NSA_TK_SEL, sb_ref)
        sel_step(kj, sa_ref)
        logits(jnp.minimum(kj + 2, slast), qa, ksb, NSA_TK_SEL, sa_ref)
        sel_step(kj + 1, sb_ref)
        return carry

    lax.fori_loop(0, (slast + 1) // 2, pair, 0)

    @pl.when(slast % 2 == 0)
    def _():
        sel_step(slast, sa_ref)

    o_s = flash_result(accs_ref)
    sg = _sigmoid(sm_ref[0])
    for r in range(r_):
        rs = slice(r * tq, (r + 1) * tq)
        o_ref[r] = (park_ref[rs, :] + gate(r, 1) * o_s[rs]).astype(o_ref.dtype)


def _nsa(p3, sm, ckv, tab_c, tab_s, tab_w, ovl_t, e_sel, qnw, knw, b_, s_):
    tq, tk, r_ = NSA_TQ, NSA_TB, NSA_REP
    nq = s_ // tq
    ncp = s_ // CMP_STRIDE
    nsel = s_ // SEL_BLOCK
    nd = tab_s.shape[1]
    nw = tab_w.shape[1]
    assert s_ % NSA_TK_SEL == 0 and NSA_TK_SEL // NSA_TB - 1 <= NSA_PAD
    assert s_ >= NSA_WIN_TILES * NSA_TK_WIN and NSA_WIN_TILES == 3
    assert NSA_TQ % NSA_TB == 0 and max(NSA_TQ, NSA_TK_WIN) % min(NSA_TQ, NSA_TK_WIN) == 0

    def kv_spec(base):
        return pl.BlockSpec((1, s_, LANES), lambda b, g, q, base=base: (base + g, b, 0))

    return pl.pallas_call(
        _nsa_body,
        grid=(b_, NSA_GROUPS, nq),
        in_specs=[pl.BlockSpec((r_, tq, LANES), lambda b, g, q: (CB_NQ // r_ + g, b * nq + q, 0)),
                  pl.BlockSpec((1, tq, LANES), lambda b, g, q: (0, b * nq + q, 0)),
                  pl.BlockSpec((1, 1, 1, ncp, NSA_DH), lambda b, g, q: (b, 0, g, 0, 0)),
                  pl.BlockSpec((1, 1, 1, ncp, NSA_DH), lambda b, g, q: (b, 1, g, 0, 0)),
                  kv_spec(CB_KS), kv_spec(CB_VS), kv_spec(CB_KW), kv_spec(CB_VW),
                  pl.BlockSpec((r_, tq, ncp), lambda b, g, q: (g, q, 0)),
                  pl.BlockSpec((r_, nd, tk, tk), lambda b, g, q: (g, 0, 0, 0)),
                  pl.BlockSpec((r_, nw, tk, tk), lambda b, g, q: (g, 0, 0, 0)),
                  pl.BlockSpec((nsel, ncp), lambda b, g, q: (0, 0)),
                  pl.BlockSpec((s_, LANES), lambda b, g, q: (0, 0)),
                  pl.BlockSpec((1, NSA_DH), lambda b, g, q: (0, 0)),
                  pl.BlockSpec((3, NSA_DH), lambda b, g, q: (0, 0))],
        out_specs=pl.BlockSpec((r_, tq, LANES), lambda b, g, q: (g, b * nq + q, 0)),
        out_shape=jax.ShapeDtypeStruct((NSA_HEADS, b_ * s_, LANES), BF),
        scratch_shapes=[pltpu.VMEM((s_, 2 * NSA_DH), BF), pltpu.VMEM((s_, 2 * NSA_DH), BF),
                        pltpu.VMEM((s_, NSA_DH), BF), pltpu.VMEM((s_, 2 * NSA_DH), BF),
                        pltpu.VMEM((r_ * tq, 2 * NSA_DH), BF),
                        pltpu.VMEM((r_ * tq, LANES), F32), pltpu.VMEM((r_ * tq, 2 * NSA_DH), F32),
                        pltpu.VMEM((r_ * tq, NSA_DH), F32),
                        pltpu.VMEM((r_ * tq, NSA_TK_SEL), F32), pltpu.VMEM((r_ * tq, NSA_TK_SEL), F32)]
        + [pltpu.VMEM((r_ * tq, NSA_TK_WIN), F32)] * NSA_WIN_TILES,
        compiler_params=pltpu.CompilerParams(dimension_semantics=("parallel", "arbitrary", "arbitrary")),
        name="nsa",
    )(p3, sm, ckv, ckv, p3, p3, p3, p3, tab_c, tab_s, tab_w, ovl_t, e_sel, qnw, knw)


def _merge_body(x_ref, ya_ref, yb_ref, ga_ref, gb_ref, wpa_ref, wpb_ref, wo_ref, o_ref):
    nh = ya_ref.shape[0]
    ya = jnp.concatenate([ya_ref[j] for j in range(nh)], axis=1)
    yb = jnp.concatenate([yb_ref[j] for j in range(nh)], axis=1)
    ga = _sigmoid(jnp.concatenate([ga_ref[j].astype(F32) for j in range(nh)], axis=1))
    gb = _sigmoid(jnp.concatenate([gb_ref[j].astype(F32) for j in range(nh)], axis=1))
    mixed = (ga * jnp.dot(ya, wpa_ref[...], preferred_element_type=F32)
             + gb * jnp.dot(yb, wpb_ref[...], preferred_element_type=F32))
    o_ref[...] = x_ref[...] + jnp.dot(mixed.astype(BF), wo_ref[...], preferred_element_type=F32)


def _merge(x2, ya, yb, p3, wpa, wpb, wo, tm):
    t, d = x2.shape
    nh = d // LANES
    hspec = pl.BlockSpec((nh, tm, LANES), lambda i: (0, i, 0))
    wspec = pl.BlockSpec((d, d), lambda i: (0, 0))
    return pl.pallas_call(
        _merge_body,
        grid=(t // tm,),
        in_specs=[pl.BlockSpec((tm, d), lambda i: (i, 0)), hspec, hspec,
                  pl.BlockSpec((nh, tm, LANES), lambda i: (CB_MGA // nh, i, 0)),
                  pl.BlockSpec((nh, tm, LANES), lambda i: (CB_MGB // nh, i, 0)),
                  wspec, wspec, wspec],
        out_specs=pl.BlockSpec((tm, d), lambda i: (i, 0)),
        out_shape=jax.ShapeDtypeStruct((t, d), F32),
        compiler_params=pltpu.CompilerParams(dimension_semantics=("parallel",)),
        name="merge",
    )(x2, ya, yb, p3, p3, wpa, wpb, wo)


def _ffn_body(x_ref, nw_ref, wg_ref, wu_ref, wd_ref, o_ref):
    x = x_ref[...]
    h = (x * lax.rsqrt(jnp.mean(x * x, axis=-1, keepdims=True) + NORM_EPS) * nw_ref[...]).astype(BF)
    gate = jnp.dot(h, wg_ref[...], preferred_element_type=F32)
    up = jnp.dot(h, wu_ref[...], preferred_element_type=F32)
    act = (_silu(gate) * up).astype(BF)
    o_ref[...] = x + jnp.dot(act, wd_ref[...], preferred_element_type=F32)


def _ffn(x2, norm_w, wg, wu, wd, tm):
    t, d = x2.shape
    f = wg.shape[1]
    return pl.pallas_call(
        _ffn_body,
        grid=(t // tm,),
        in_specs=[pl.BlockSpec((tm, d), lambda i: (i, 0)),
                  pl.BlockSpec((1, d), lambda i: (0, 0)),
                  pl.BlockSpec((d, f), lambda i: (0, 0)),
                  pl.BlockSpec((d, f), lambda i: (0, 0)),
                  pl.BlockSpec((f, d), lambda i: (0, 0))],
        out_specs=pl.BlockSpec((tm, d), lambda i: (i, 0)),
        out_shape=jax.ShapeDtypeStruct((t, d), F32),
        compiler_params=pltpu.CompilerParams(dimension_semantics=("parallel",)),
        name="ffn",
    )(x2, norm_w, wg, wu, wd)


def _arrange_w_in(w_in):
    o_ga = 4 * GDN_HEADS * GDN_DK
    o_gb = o_ga + GDN_HEADS
    o_nq = o_gb + GDN_HEADS
    o_nkv = o_nq + NSA_HEADS * NSA_DH
    o_ng = o_nkv + 6 * NSA_GROUPS * NSA_DH
    o_mg = o_ng + 3 * NSA_HEADS
    d = w_in.shape[0]
    w = w_in.astype(BF)
    pad = jnp.zeros((d, 2 * LANES - (o_nq - o_ga) - (o_mg - o_ng)), BF)
    return jnp.concatenate([w[:, :o_ga], w[:, o_nq:o_nkv], w[:, o_mg:], w[:, o_nkv:o_ng],
                            w[:, o_ga:o_nq], w[:, o_ng:o_mg], pad], axis=1)


def _overlap_t(s_):
    ncp = s_ // CMP_STRIDE
    nsel = s_ // SEL_BLOCK
    cs = np.arange(ncp) * CMP_STRIDE
    ss = np.arange(nsel) * SEL_BLOCK
    ov = (cs[None, :] < ss[:, None] + SEL_BLOCK) & (cs[None, :] + CMP_BLOCK > ss[:, None])
    ov[:, ncp - 1] = False
    return jnp.asarray(ov.astype(np.float32), BF)


def _sel_expand(s_):
    assert s_ // SEL_BLOCK <= LANES
    pos = np.arange(s_)
    e = (pos[:, None] // SEL_BLOCK == np.arange(LANES)[None, :]).astype(np.float32)
    return jnp.asarray(e, BF)


def kernel(x, norm1_w, w_in, conv_w, a_log, dt_bias, gdn_norm_w, cmp_pe, cmp_w1, cmp_w2, q_norm_w, k_norm_w,
           rel_bias, w_proj_a, w_proj_b, w_out, norm2_w, w_gate, w_up, w_down):
    b_, s_, d = x.shape
    t = b_ * s_
    x2 = x.reshape(t, d)
    tab_c, tab_s, tab_w = _bias_tables(rel_bias, s_)
    ovl_t = _overlap_t(s_)
    e_sel = _sel_expand(s_)
    for l in range(norm1_w.shape[0]):
        p3, sm = _proj(x2, norm1_w[l][None, :], _arrange_w_in(w_in[l]), tm=min(1024, t), tn=10 * LANES)
        conv_w3 = conv_w[l].reshape(GDN_CONV, 3 * GDN_HEADS, LANES)
        alog_b = jnp.pad(a_log[l], (0, LANES - GDN_HEADS))[None, :]
        dtb_b = jnp.pad(dt_bias[l], (0, LANES - GDN_HEADS))[None, :]
        y_a = _gdn(p3, sm, conv_w3, alog_b, dtb_b, gdn_norm_w[l][None, :], b_, s_)
        pe2 = cmp_pe[l].reshape(2, 2, CMP_STRIDE * NSA_DH)
        ckv = _cmp(p3, pe2, cmp_w1[l].astype(BF), cmp_w2[l].astype(BF), k_norm_w[l][0:1], b_, s_)
        y_b = _nsa(p3, sm, ckv, tab_c, tab_s, tab_w, ovl_t, e_sel, q_norm_w[l][None, :], k_norm_w[l], b_, s_)
        x2 = _merge(x2, y_a, y_b, p3, w_proj_a[l].astype(BF), w_proj_b[l].astype(BF), w_out[l].astype(BF),
                    tm=min(512, t))
        x2 = _ffn(x2, norm2_w[l][None, :], w_gate[l].astype(BF), w_up[l].astype(BF), w_down[l].astype(BF),
                  tm=min(512, t))
    return x2.reshape(b_, s_, d)
```

```python
import functools
import math

import numpy as np
import jax
import jax.numpy as jnp
from jax import lax
from jax.experimental import pallas as pl
from jax.experimental.pallas import tpu as pltpu

F32 = jnp.float32
BF = jnp.bfloat16

LANES = 128
D_MODEL = 1024
GDN_HEADS = 8
GDN_DK = 128
GDN_DV = 128
GDN_CONV = 4
GDN_CHUNK = 64
NSA_HEADS = 8
NSA_GROUPS = 2
NSA_REP = NSA_HEADS // NSA_GROUPS
NSA_DH = 128
CMP_BLOCK = 32
CMP_STRIDE = 16
CMP_HIDDEN = 256
SEL_BLOCK = 64
SEL_TOPN = 16
WINDOW = 512
FORCE_BONUS = 1000.0
REL_BUCKETS = 32
REL_MAX_DIST = 1024
FFN_HIDDEN = 2816
NORM_EPS = 1e-6
NEG = -1e30
M_INIT = -3e38
LOG2E = 1.4426950408889634

CB_GQ, CB_GK, CB_GV, CB_GZ = 0, 8, 16, 24
CB_NQ = 32
CB_MGA, CB_MGB = 40, 48
CB_KC, CB_VC, CB_KS, CB_VS, CB_KW, CB_VW = 56, 58, 60, 62, 64, 66
CB_SMALL = 68
N_CB = 70
SM_A, SM_B, SM_GATE = 0, 8, 16

GDN_ROWS = 256
NSA_TQ = 256
NSA_TB = 128
NSA_TK_SEL = 512
NSA_TK_WIN = 256
NSA_WIN_TILES = (WINDOW + max(NSA_TQ, NSA_TK_WIN) - 2) // NSA_TK_WIN + 1
NSA_PAD = NSA_TK_SEL // NSA_TB - 1


def _mm(a, b):
    return jnp.dot(a.astype(BF), b.astype(BF), preferred_element_type=F32)


def _mm_nt(a, b):
    return lax.dot_general(a.astype(BF), b.astype(BF), (((1,), (1,)), ((), ())),
                           preferred_element_type=F32)


def _mm_tn(a, b):
    return lax.dot_general(a.astype(BF), b.astype(BF), (((0,), (0,)), ((), ())),
                           preferred_element_type=F32)


def _split3(x):
    x1 = x.astype(BF)
    r1 = x - x1.astype(F32)
    x2 = r1.astype(BF)
    x3 = (r1 - x2.astype(F32)).astype(BF)
    return x1, x2, x3


def _sigmoid(x):
    return 0.5 * jnp.tanh(0.5 * x) + 0.5


def _silu_of_half(h):
    return h + h * jnp.tanh(h)


def _silu(x):
    return _silu_of_half(0.5 * x)


def _softplus(x):
    return jnp.maximum(x, 0.0) + jnp.log(1.0 + jnp.exp(-jnp.abs(x)))


def _rel_thresholds():
    d = np.arange(0, 4 * REL_MAX_DIST, dtype=np.int64)
    max_exact = REL_BUCKETS // 2
    d_f = np.maximum(d, 1).astype(np.float32)
    large = max_exact + (np.log(d_f / np.float32(max_exact)) / np.float32(math.log(REL_MAX_DIST / max_exact))
                         * np.float32(REL_BUCKETS - max_exact)).astype(np.int32)
    large = np.minimum(large, REL_BUCKETS - 1)
    bucket = np.where(d < max_exact, d, large)
    assert np.all(np.diff(bucket) >= 0)
    return [int(np.argmax(bucket >= k)) for k in range(REL_BUCKETS)]


REL_THR = _rel_thresholds()


def _proj_body(x_ref, nw_ref, w_ref, o_ref, sm_ref, h_ref):
    @pl.when(pl.program_id(1) == 0)
    def _():
        x = x_ref[...]
        y = x * lax.rsqrt(jnp.mean(x * x, axis=-1, keepdims=True) + NORM_EPS)
        h_ref[...] = (y * nw_ref[...]).astype(BF)

    nb = o_ref.shape[0]
    r = jnp.dot(h_ref[...], w_ref[...], preferred_element_type=F32)
    for j in range(nb):
        o_ref[j] = r[:, j * LANES:(j + 1) * LANES].astype(BF)

    @pl.when(pl.program_id(1) == CB_SMALL // nb)
    def _():
        sm_ref[0] = r[:, (CB_SMALL % nb) * LANES:(CB_SMALL % nb + 1) * LANES]


def _proj(x2, norm_w, w_all, tm, tn):
    t, d = x2.shape
    n = w_all.shape[1]
    nb = tn // LANES
    return pl.pallas_call(
        _proj_body,
        grid=(t // tm, n // tn),
        in_specs=[pl.BlockSpec((tm, d), lambda i, j: (i, 0)),
                  pl.BlockSpec((1, d), lambda i, j: (0, 0)),
                  pl.BlockSpec((d, tn), lambda i, j: (0, j))],
        out_specs=[pl.BlockSpec((nb, tm, LANES), lambda i, j: (j, i, 0)),
                   pl.BlockSpec((1, tm, LANES), lambda i, j: (0, i, 0))],
        out_shape=[jax.ShapeDtypeStruct((n // LANES, t, LANES), BF),
                   jax.ShapeDtypeStruct((1, t, LANES), F32)],
        scratch_shapes=[pltpu.VMEM((tm, d), BF)],
        compiler_params=pltpu.CompilerParams(dimension_semantics=("parallel", "arbitrary")),
        name="proj",
    )(x2, norm_w, w_all)


def _gdn_body(q_ref, k_ref, v_ref, z_ref, sm_ref, cw_ref, alog_ref, dtb_ref, nw_ref, o_ref,
              ext_ref, st_ref):
    rows = GDN_ROWS
    c = GDN_CHUNK
    nchunk = rows // c
    s = pl.program_id(1)

    @pl.when(s == 0)
    def _():
        ext_ref[:, 0:8, :] = jnp.zeros((3 * GDN_HEADS, 8, LANES), F32)
        st_ref[...] = jnp.zeros_like(st_ref)

    for j in range(GDN_HEADS):
        ext_ref[j, 8:8 + rows, :] = q_ref[j].astype(F32)
        ext_ref[GDN_HEADS + j, 8:8 + rows, :] = k_ref[j].astype(F32)
        ext_ref[2 * GDN_HEADS + j, 8:8 + rows, :] = v_ref[j].astype(F32)

    ri = lax.broadcasted_iota(jnp.int32, (rows, rows), 0)
    ci = lax.broadcasted_iota(jnp.int32, (rows, rows), 1)
    l_tril = jnp.where(((ri // c) == (ci // c)) & (ri >= ci), 1.0, 0.0).astype(BF)
    rt = lax.broadcasted_iota(jnp.int32, (LANES, LANES), 0)
    ct = lax.broadcasted_iota(jnp.int32, (LANES, LANES), 1)
    same_t = (rt // c) == (ct // c)
    tril_t = same_t & (rt >= ct)
    strict_t = same_t & (rt > ct)
    eye = jnp.where(rt == ct, 1.0, 0.0)

    sm = sm_ref[0]
    lane = lax.broadcasted_iota(jnp.int32, (rows, LANES), 1)
    gall = jnp.where(lane < GDN_HEADS, -jnp.exp(alog_ref[...]) * _softplus(sm + dtb_ref[...]), 0.0)
    g1 = gall.astype(BF).astype(F32)
    r1 = gall - g1
    g2 = r1.astype(BF).astype(F32)
    packed = g1 + pltpu.roll(g2, GDN_HEADS, 1) + pltpu.roll(r1 - g2, 2 * GDN_HEADS, 1)
    gc = jnp.dot(l_tril, packed.astype(BF), preferred_element_type=F32)
    gcum_all = gc + pltpu.roll(gc, LANES - GDN_HEADS, 1) + pltpu.roll(gc, LANES - 2 * GDN_HEADS, 1)

    def conv_silu(j):
        acc = (0.5 * cw_ref[0, pl.ds(j, 1), :]) * ext_ref[j, pl.ds(5, rows), :]
        for i in range(1, GDN_CONV):
            acc = acc + (0.5 * cw_ref[i, pl.ds(j, 1), :]) * ext_ref[j, pl.ds(5 + i, rows), :]
        return _silu_of_half(acc)

    def head_setup(h):
        qh = conv_silu(h)
        kh = conv_silu(GDN_HEADS + h)
        vv = conv_silu(2 * GDN_HEADS + h)
        qn = qh * lax.rsqrt(jnp.sum(qh * qh, axis=-1, keepdims=True) + NORM_EPS) * (GDN_DK ** -0.5)
        kn = kh * lax.rsqrt(jnp.sum(kh * kh, axis=-1, keepdims=True) + NORM_EPS)

        beta = _sigmoid(jnp.broadcast_to(sm[:, SM_B + h:SM_B + h + 1], (rows, LANES)))
        gcum = jnp.broadcast_to(gcum_all[:, h:h + 1], (rows, LANES))
        glast = jnp.concatenate(
            [jnp.broadcast_to(gcum[(n + 1) * c - 1:(n + 1) * c, :], (c, LANES)) for n in range(nchunk)],
            axis=0)
        gct = gcum.T
        kb = kn * beta
        eg = jnp.exp(gcum)
        knb = kn.astype(BF)
        a_t, intra_t = [], []
        for u in range(rows // LANES):
            rs = slice(u * LANES, (u + 1) * LANES)
            diff = gcum[rs] - gct[:, rs]
            decay = jnp.where(tril_t, jnp.exp(jnp.where(tril_t, diff, 0.0)), 0.0)
            a_t.append(jnp.where(strict_t, _mm_nt(kb[rs], knb[rs]) * decay, 0.0))
            intra_t.append(_mm_nt(qn[rs], knb[rs]) * decay)
        return dict(
            a=a_t, intra=intra_t,
            rhs=jnp.concatenate([vv * beta, kb * eg], axis=1),
            qg=qn * eg, kdec=kn * jnp.exp(glast - gcum), cd=jnp.exp(glast))

    def all_heads():
        hs = list(range(GDN_HEADS))
        nt = rows // LANES
        w = [head_setup(h) for h in hs]
        a_all = [a_ for d in w for a_ in d["a"]]
        p = [_mm(a_, a_) for a_ in a_all]
        tg = [eye - a_ for a_ in a_all]
        for j in range(1, 6):
            tp = [_mm(ti, pi) for ti, pi in zip(tg, p)]
            if j < 5:
                p = [_mm(pi, pi) for pi in p]
            tg = [ti + tpi for ti, tpi in zip(tg, tp)]
        t = [tg[i * nt:(i + 1) * nt] for i in range(len(w))]
        sol = [[d["rhs"][u * LANES:(u + 1) * LANES] + _mm(tu - eye, d["rhs"][u * LANES:(u + 1) * LANES])
                for u, tu in enumerate(ti)] for d, ti in zip(w, t)]
        st = [st_ref[h] for h in hs]
        outs = [[] for _ in hs]
        cpt = LANES // c
        for n in range(nchunk):
            sl = slice(n * c, (n + 1) * c)
            lo = slice((n % cpt) * c, (n % cpt + 1) * c)
            ks = [_mm(jnp.concatenate([s_[n // cpt][lo, GDN_DV:], d["qg"][sl]], axis=0), si)
                  for s_, d, si in zip(sol, w, st)]
            vn = [s_[n // cpt][lo, :GDN_DV] - k_[:c] for s_, k_ in zip(sol, ks)]
            for u, (d, k_, v_) in enumerate(zip(w, ks, vn)):
                outs[u].append(k_[c:] + _mm(d["intra"][n // cpt][lo, lo], v_))
            st = [si * jnp.concatenate([d["cd"][sl], d["cd"][sl]], axis=0) + _mm_tn(d["kdec"][sl], v_)
                  for si, d, v_ in zip(st, w, vn)]
        for u, h in enumerate(hs):
            st_ref[h] = st[u]
            o = jnp.concatenate(outs[u], axis=0)
            on = o * lax.rsqrt(jnp.mean(o * o, axis=-1, keepdims=True) + NORM_EPS) * nw_ref[...]
            o_ref[h] = (on * _silu(z_ref[h].astype(F32))).astype(o_ref.dtype)

    all_heads()

    for j in range(3 * GDN_HEADS):
        ext_ref[j, 0:8, :] = ext_ref[j, rows:rows + 8, :]


def _gdn(p3, sm, conv_w3, alog_b, dtb_b, gdn_norm_w, b_, s_):
    rows = GDN_ROWS
    ns = s_ // rows
    hb = GDN_HEADS

    def cb(base):
        return pl.BlockSpec((hb, rows, LANES), lambda b, s, base=base: (base // hb, b * ns + s, 0))

    return pl.pallas_call(
        _gdn_body,
        grid=(b_, ns),
        in_specs=[cb(CB_GQ), cb(CB_GK), cb(CB_GV), cb(CB_GZ),
                  pl.BlockSpec((1, rows, LANES), lambda b, s: (0, b * ns + s, 0)),
                  pl.BlockSpec((GDN_CONV, 3 * hb, LANES), lambda b, s: (0, 0, 0)),
                  pl.BlockSpec((1, LANES), lambda b, s: (0, 0)),
                  pl.BlockSpec((1, LANES), lambda b, s: (0, 0)),
                  pl.BlockSpec((1, LANES), lambda b, s: (0, 0))],
        out_specs=pl.BlockSpec((hb, rows, LANES), lambda b, s: (0, b * ns + s, 0)),
        out_shape=jax.ShapeDtypeStruct((hb, b_ * s_, LANES), BF),
        scratch_shapes=[pltpu.VMEM((3 * hb, rows + 8, LANES), F32),
                        pltpu.VMEM((hb, GDN_DK, GDN_DV), F32)],
        compiler_params=pltpu.CompilerParams(dimension_semantics=("parallel", "arbitrary")),
        name="gdn",
    )(p3, p3, p3, p3, sm, conv_w3, alog_b, dtb_b, gdn_norm_w)


def _cmp_body(x_ref, pe_ref, w1_ref, w2_ref, nw_ref, o_ref, c_ref, xf_ref):
    kv = pl.program_id(1)
    nch = c_ref.shape[0]
    half = CMP_STRIDE * NSA_DH
    xf_ref[...] = x_ref[0].astype(F32)
    for p in range(CMP_STRIDE):
        c_ref[:, p * NSA_DH:(p + 1) * NSA_DH] = xf_ref[pl.ds(p, nch, stride=CMP_STRIDE), :]
    cc = c_ref[...]
    u = _mm(cc + pe_ref[0, 0:1, :], w1_ref[0, 0:half, :])
    v = _mm(cc + pe_ref[0, 1:2, :], w1_ref[0, half:2 * half, :])
    v_next = jnp.concatenate([v[1:], v[:1]], axis=0)
    hid = _silu(u + v_next)
    out = _mm(hid, w2_ref[0])
    normed = out * lax.rsqrt(jnp.mean(out * out, axis=-1, keepdims=True) + NORM_EPS) * nw_ref[...]
    o_ref[0, 0, 0] = jnp.where(kv == 0, normed, out)


def _cmp(p3, pe2, w1, w2, knw0, b_, s_):
    nch = s_ // CMP_STRIDE
    g_ = NSA_GROUPS
    return pl.pallas_call(
        _cmp_body,
        grid=(b_, 2, g_),
        in_specs=[pl.BlockSpec((1, s_, LANES), lambda b, kv, g: (CB_KC + 2 * kv + g, b, 0)),
                  pl.BlockSpec((1, 2, CMP_STRIDE * NSA_DH), lambda b, kv, g: (kv, 0, 0)),
                  pl.BlockSpec((1, CMP_BLOCK * NSA_DH, CMP_HIDDEN), lambda b, kv, g: (kv, 0, 0)),
                  pl.BlockSpec((1, CMP_HIDDEN, NSA_DH), lambda b, kv, g: (kv, 0, 0)),
                  pl.BlockSpec((1, NSA_DH), lambda b, kv, g: (0, 0))],
        out_specs=pl.BlockSpec((1, 1, 1, nch, NSA_DH), lambda b, kv, g: (b, kv, g, 0, 0)),
        out_shape=jax.ShapeDtypeStruct((b_, 2, g_, nch, NSA_DH), F32),
        scratch_shapes=[pltpu.VMEM((nch, CMP_STRIDE * NSA_DH), F32), pltpu.VMEM((s_, LANES), F32)],
        compiler_params=pltpu.CompilerParams(dimension_semantics=("parallel", "arbitrary", "arbitrary")),
        name="cmp",
    )(p3, pe2, w1, w2, knw0)


def _bias_of(d, rb_ref, h):
    val = jnp.full(d.shape, rb_ref[0, h], F32)
    for k in range(1, REL_BUCKETS):
        val = jnp.where(d >= REL_THR[k], rb_ref[k, h], val)
    return val * LOG2E


def _bias_body(rb_ref, tc_ref, ts_ref, tw_ref):
    h = pl.program_id(0)
    _, s_, ncp = tc_ref.shape

    r = lax.broadcasted_iota(jnp.int32, (CMP_STRIDE, 2 * ncp), 0)
    k = lax.broadcasted_iota(jnp.int32, (CMP_STRIDE, 2 * ncp), 1)
    d = CMP_STRIDE * (ncp - 1 - k) + r - (CMP_BLOCK - 1)
    gen = jnp.where(d >= 0, _bias_of(d, rb_ref, h), NEG)

    def row_group(a, carry):
        row0 = pl.multiple_of(a * CMP_STRIDE, CMP_STRIDE)
        tc_ref[0, pl.ds(row0, CMP_STRIDE), :] = pltpu.roll(gen, (ncp + 1 + a) % (2 * ncp), 1)[:, :ncp]
        return carry

    lax.fori_loop(0, s_ // CMP_STRIDE, row_group, 0, unroll=8)
    i = lax.broadcasted_iota(jnp.int32, (NSA_TB, NSA_TB), 0)
    j = lax.broadcasted_iota(jnp.int32, (NSA_TB, NSA_TB), 1)
    for e in range(ts_ref.shape[1]):
        d = (e - NSA_PAD) * NSA_TB + i - j
        ts_ref[0, e] = jnp.where(d >= 0, _bias_of(d, rb_ref, h), NEG)
    for e in range(tw_ref.shape[1]):
        d = (e - NSA_PAD) * NSA_TB + i - j
        tw_ref[0, e] = jnp.where((d >= 0) & (d < WINDOW), _bias_of(d, rb_ref, h), NEG)


def _sel_table_len():
    a = 0
    while a * NSA_TB - (NSA_TB - 1) < REL_THR[REL_BUCKETS - 1]:
        a += 1
    return a + 1 + NSA_PAD


def _win_table_len():
    return (WINDOW + NSA_TB - 1) // NSA_TB + 2 + NSA_PAD


def _bias_tables(rel_bias, s_):
    ncp = s_ // CMP_STRIDE
    nd = _sel_table_len()
    nw = _win_table_len()
    return pl.pallas_call(
        _bias_body,
        grid=(NSA_HEADS,),
        in_specs=[pl.BlockSpec(memory_space=pltpu.SMEM)],
        out_specs=[pl.BlockSpec((1, s_, ncp), lambda h: (h, 0, 0)),
                   pl.BlockSpec((1, nd, NSA_TB, NSA_TB), lambda h: (h, 0, 0, 0)),
                   pl.BlockSpec((1, nw, NSA_TB, NSA_TB), lambda h: (h, 0, 0, 0))],
        out_shape=[jax.ShapeDtypeStruct((NSA_HEADS, s_, ncp), F32),
                   jax.ShapeDtypeStruct((NSA_HEADS, nd, NSA_TB, NSA_TB), F32),
                   jax.ShapeDtypeStruct((NSA_HEADS, nw, NSA_TB, NSA_TB), F32)],
        compiler_params=pltpu.CompilerParams(dimension_semantics=("parallel",)),
        name="bias",
    )(rel_bias)


def _rms_rows(x, w):
    return x * lax.rsqrt(jnp.mean(x * x, axis=-1, keepdims=True) + NORM_EPS) * w


def _nsa_body(q_ref, sm_ref, kc_ref, vc_ref, ks_ref, vs_ref, kw_ref, vw_ref, tc_ref, ts_ref, tw_ref,
              ovl_ref, e_ref, qnw_ref, knw_ref, o_ref,
              ksb, vsb, kwb, vwb, qa_ref, ms_ref, accs_ref, park_ref, sa_ref, sb_ref, wa_ref, wb_ref, wc_ref):
    tq, r_, tb = NSA_TQ, NSA_REP, NSA_TB
    g = pl.program_id(1)
    qi = pl.program_id(2)
    nsel = ovl_ref.shape[0]
    topn = min(SEL_TOPN, nsel)

    @pl.when(qi == 0)
    def _():
        ones = jnp.ones(vs_ref.shape[1:], BF)
        ksb[:, :NSA_DH] = _rms_rows(ks_ref[0].astype(F32), knw_ref[1:2, :]).astype(BF)
        ksb[:, NSA_DH:] = e_ref[...]
        kwb[...] = _rms_rows(kw_ref[0].astype(F32), knw_ref[2:3, :]).astype(BF)
        vsb[:, :NSA_DH] = vs_ref[0]
        vsb[:, NSA_DH:] = ones
        vwb[:, :NSA_DH] = vw_ref[0]
        vwb[:, NSA_DH:] = ones

    qscale = NSA_DH ** -0.5 * LOG2E
    for r in range(r_):
        qa_ref[r * tq:(r + 1) * tq, :NSA_DH] = (
            _rms_rows(q_ref[r].astype(F32), qnw_ref[...]) * qscale).astype(BF)
    qs = qa_ref[:, :NSA_DH]

    def logits(kj, q, k_sc, tk, dst_ref):
        off = pl.multiple_of(kj * tk, tk)
        dst_ref[:, :tk] = _mm_nt(q, k_sc[pl.ds(off, tk), :])

    def flash_init(m_ref, acc_ref):
        m_ref[...] = jnp.full(m_ref.shape, M_INIT, F32)
        acc_ref[...] = jnp.zeros_like(acc_ref)

    def flash_step(kj, s_ref, v_sc, tab_ref, tk, m_ref, acc_ref):
        nct = tk // tb
        nrt = tq // tb
        off = pl.multiple_of(kj * tk, tk)
        e0 = qi * nrt - kj * nct + NSA_PAD
        idx = {o: jnp.clip(e0 + o, 0, tab_ref.shape[1] - 1) for o in range(-(nct - 1), nrt)}
        m_prev = m_ref[...]
        m_rows, p_rows = [], []
        for rb in range(r_ * nrt):
            r, rho = divmod(rb, nrt)
            rs = slice(rb * tb, (rb + 1) * tb)
            pieces = [s_ref[rs, c * tb:(c + 1) * tb] + tab_ref[r, idx[rho - c]] for c in range(nct)]
            mx = pieces[0]
            for c in range(1, nct):
                mx = jnp.maximum(mx, pieces[c])
            m_next = jnp.maximum(m_prev[rs], jnp.max(mx, axis=-1, keepdims=True))
            m_rows.append(m_next)
            p_rows.append(jnp.concatenate([jnp.exp2(pc_ - m_next).astype(BF) for pc_ in pieces], axis=1))
        m_next = jnp.concatenate(m_rows, axis=0)
        p = jnp.concatenate(p_rows, axis=0)
        alpha = jnp.exp2(m_prev - m_next)
        acc_ref[...] = (jnp.concatenate([alpha, alpha], axis=1) * acc_ref[...]
                        + jnp.dot(p, v_sc[pl.ds(off, tk), :], preferred_element_type=F32))
        m_ref[...] = m_next

    def flash_result(acc_ref):
        acc = acc_ref[...]
        return acc[:, :NSA_DH] / jnp.maximum(acc[:, NSA_DH:], 1e-30)

    wfirst = jnp.maximum((qi * tq + tq - 1) // NSA_TK_WIN - (NSA_WIN_TILES - 1), 0)
    wbufs = (wa_ref, wb_ref, wc_ref)

    def window_output():
        nct = NSA_TK_WIN // tb
        nrt = tq // tb
        wk = NSA_WIN_TILES * NSA_TK_WIN
        idx = {(u, o): jnp.clip(qi * nrt - (wfirst + u) * nct + NSA_PAD + o, 0, tw_ref.shape[1] - 1)
               for u in range(NSA_WIN_TILES) for o in range(-(nct - 1), nrt)}
        p_rows = []
        for rb in range(r_ * nrt):
            r, rho = divmod(rb, nrt)
            rs = slice(rb * tb, (rb + 1) * tb)
            pieces = [wbufs[u][rs, c * tb:(c + 1) * tb] + tw_ref[r, idx[(u, rho - c)]]
                      for u in range(NSA_WIN_TILES) for c in range(nct)]
            mx = pieces[0]
            for pc_ in pieces[1:]:
                mx = jnp.maximum(mx, pc_)
            m = jnp.max(mx, axis=-1, keepdims=True)
            p_rows.append(jnp.concatenate([jnp.exp2(pc_ - m).astype(BF) for pc_ in pieces], axis=1))
        p = jnp.concatenate(p_rows, axis=0)
        off = pl.multiple_of(wfirst * NSA_TK_WIN, NSA_TK_WIN)
        pv = jnp.dot(p, vwb[pl.ds(off, wk), :], preferred_element_type=F32)
        return pv[:, :NSA_DH] / jnp.maximum(pv[:, NSA_DH:], 1e-30)

    lc = _mm_nt(qs, kc_ref[0, 0, 0]) + jnp.concatenate([tc_ref[r] for r in range(r_)], axis=0)
    for u in range(NSA_WIN_TILES):
        logits(wfirst + u, qs, kwb, NSA_TK_WIN, wbufs[u])
    mc = jnp.max(lc, axis=-1, keepdims=True)
    pc = jnp.exp2(lc - mc)
    lsum = jnp.sum(pc, axis=-1, keepdims=True)
    pc = pc * jnp.where(mc > 0.5 * NEG, 1.0 / jnp.maximum(lsum, 1e-30), 0.0)

    psum = pc[0:tq]
    for r in range(1, r_):
        psum = psum + pc[r * tq:(r + 1) * tq]
    s1, s2, s3 = _split3(psum)
    dn = (((1,), (1,)), ((), ()))
    ovl = ovl_ref[...]
    imp_t = (lax.dot_general(ovl, s1, dn, preferred_element_type=F32)
             + lax.dot_general(ovl, s2, dn, preferred_element_type=F32)
             + lax.dot_general(ovl, s3, dn, preferred_element_type=F32))

    oc = _mm(pc, vc_ref[0, 0, 0])
    o_w = window_output()

    sg = _sigmoid(sm_ref[0])

    def gate(r, br):
        c0 = SM_GATE + r * 3 + br
        c1 = SM_GATE + (r_ + r) * 3 + br
        return jnp.where(g == 0, sg[:, c0:c0 + 1], sg[:, c1:c1 + 1])

    for r in range(r_):
        rs = slice(r * tq, (r + 1) * tq)
        park_ref[rs, :] = gate(r, 0) * oc[rs] + gate(r, 2) * o_w[rs]

    jb = lax.broadcasted_iota(jnp.int32, (nsel, tq), 0)
    tpos = qi * tq + lax.broadcasted_iota(jnp.int32, (nsel, tq), 1)
    tblk = tpos // SEL_BLOCK
    forced = (jb == 0) | (jb == tblk) | (jb == tblk - 1)
    score = jnp.where(jb <= tblk, imp_t + jnp.where(forced, FORCE_BONUS, 0.0), NEG)
    sub = lax.broadcasted_iota(jnp.int32, (8, tq), 0)
    groups = [score[8 * v:8 * v + 8] for v in range(nsel // 8)]
    counts = [jnp.zeros((8, tq), F32) for _ in groups]
    for jp in range(nsel):
        row = jnp.broadcast_to(score[jp:jp + 1, :], (8, tq))
        for v, grp in enumerate(groups):
            if 8 * v > jp:
                beats = jnp.where(row >= grp, 1.0, 0.0)
            elif 8 * v + 8 <= jp:
                beats = jnp.where(row > grp, 1.0, 0.0)
            else:
                tie = jnp.where(sub + 8 * v > jp, 1.0, 0.0)
                beats = jnp.where(row > grp, 1.0, jnp.where(row == grp, tie, 0.0))
            counts[v] = counts[v] + beats
    unsel_t = jnp.where(jnp.concatenate(counts, axis=0) < topn, 0.0, NEG)
    unsel_pad = jnp.concatenate([unsel_t, jnp.zeros((LANES - nsel, tq), F32)], axis=0).astype(BF)
    ri = lax.broadcasted_iota(jnp.int32, (tq, tq), 0)
    ci = lax.broadcasted_iota(jnp.int32, (tq, tq), 1)
    eye = jnp.where(ri == ci, 1.0, 0.0).astype(BF)
    unsel_q = lax.dot_general(eye, unsel_pad, dn, preferred_element_type=F32).astype(BF)
    for r in range(r_):
        qa_ref[r * tq:(r + 1) * tq, NSA_DH:] = unsel_q

    qa = qa_ref[...]
    slast = (qi * tq + tq - 1) // NSA_TK_SEL
    flash_init(ms_ref, accs_ref)
    logits(0, qa, ksb, NSA_TK_SEL, sa_ref)

    def sel_step(kj, s_ref):
        flash_step(kj, s_ref, vsb, ts_ref, NSA_TK_SEL, ms_ref, accs_ref)

    def pair(i, carry):
        kj = 2 * i
        logits(kj + 1, qa, ksb, NSA_TK_SEL, sb_ref)
        sel_step(kj, sa_ref)
        logits(jnp.minimum(kj + 2, slast), qa, ksb, NSA_TK_SEL, sa_ref)
        sel_step(kj + 1, sb_ref)
        return carry

    lax.fori_loop(0, (slast + 1) // 2, pair, 0)

    @pl.when(slast % 2 == 0)
    def _():
        sel_step(slast, sa_ref)

    o_s = flash_result(accs_ref)
    sg = _sigmoid(sm_ref[0])
    for r in range(r_):
        rs = slice(r * tq, (r + 1) * tq)
        o_ref[r] = (park_ref[rs, :] + gate(r, 1) * o_s[rs]).astype(o_ref.dtype)


def _nsa(p3, sm, ckv, tab_c, tab_s, tab_w, ovl_t, e_sel, qnw, knw, b_, s_):
    tq, tk, r_ = NSA_TQ, NSA_TB, NSA_REP
    nq = s_ // tq
    ncp = s_ // CMP_STRIDE
    nsel = s_ // SEL_BLOCK
    nd = tab_s.shape[1]
    nw = tab_w.shape[1]
    assert s_ % NSA_TK_SEL == 0 and NSA_TK_SEL // NSA_TB - 1 <= NSA_PAD
    assert s_ >= NSA_WIN_TILES * NSA_TK_WIN and NSA_WIN_TILES == 3
    assert NSA_TQ % NSA_TB == 0 and max(NSA_TQ, NSA_TK_WIN) % min(NSA_TQ, NSA_TK_WIN) == 0

    def kv_spec(base):
        return pl.BlockSpec((1, s_, LANES), lambda b, g, q, base=base: (base + g, b, 0))

    return pl.pallas_call(
        _nsa_body,
        grid=(b_, NSA_GROUPS, nq),
        in_specs=[pl.BlockSpec((r_, tq, LANES), lambda b, g, q: (CB_NQ // r_ + g, b * nq + q, 0)),
                  pl.BlockSpec((1, tq, LANES), lambda b, g, q: (0, b * nq + q, 0)),
                  pl.BlockSpec((1, 1, 1, ncp, NSA_DH), lambda b, g, q: (b, 0, g, 0, 0)),
                  pl.BlockSpec((1, 1, 1, ncp, NSA_DH), lambda b, g, q: (b, 1, g, 0, 0)),
                  kv_spec(CB_KS), kv_spec(CB_VS), kv_spec(CB_KW), kv_spec(CB_VW),
                  pl.BlockSpec((r_, tq, ncp), lambda b, g, q: (g, q, 0)),
                  pl.BlockSpec((r_, nd, tk, tk), lambda b, g, q: (g, 0, 0, 0)),
                  pl.BlockSpec((r_, nw, tk, tk), lambda b, g, q: (g, 0, 0, 0)),
                  pl.BlockSpec((nsel, ncp), lambda b, g, q: (0, 0)),
                  pl.BlockSpec((s_, LANES), lambda b, g, q: (0, 0)),
                  pl.BlockSpec((1, NSA_DH), lambda b, g, q: (0, 0)),
                  pl.BlockSpec((3, NSA_DH), lambda b, g, q: (0, 0))],
        out_specs=pl.BlockSpec((r_, tq, LANES), lambda b, g, q: (g, b * nq + q, 0)),
        out_shape=jax.ShapeDtypeStruct((NSA_HEADS, b_ * s_, LANES), BF),
        scratch_shapes=[pltpu.VMEM((s_, 2 * NSA_DH), BF), pltpu.VMEM((s_, 2 * NSA_DH), BF),
                        pltpu.VMEM((s_, NSA_DH), BF), pltpu.VMEM((s_, 2 * NSA_DH), BF),
                        pltpu.VMEM((r_ * tq, 2 * NSA_DH), BF),
                        pltpu.VMEM((r_ * tq, LANES), F32), pltpu.VMEM((r_ * tq, 2 * NSA_DH), F32),
                        pltpu.VMEM((r_ * tq, NSA_DH), F32),
                        pltpu.VMEM((r_ * tq, NSA_TK_SEL), F32), pltpu.VMEM((r_ * tq, NSA_TK_SEL), F32)]
        + [pltpu.VMEM((r_ * tq, NSA_TK_WIN), F32)] * NSA_WIN_TILES,
        compiler_params=pltpu.CompilerParams(dimension_semantics=("parallel", "arbitrary", "arbitrary")),
        name="nsa",
    )(p3, sm, ckv, ckv, p3, p3, p3, p3, tab_c, tab_s, tab_w, ovl_t, e_sel, qnw, knw)


def _merge_body(x_ref, ya_ref, yb_ref, ga_ref, gb_ref, wpa_ref, wpb_ref, wo_ref, o_ref):
    nh = ya_ref.shape[0]
    ya = jnp.concatenate([ya_ref[j] for j in range(nh)], axis=1)
    yb = jnp.concatenate([yb_ref[j] for j in range(nh)], axis=1)
    ga = _sigmoid(jnp.concatenate([ga_ref[j].astype(F32) for j in range(nh)], axis=1))
    gb = _sigmoid(jnp.concatenate([gb_ref[j].astype(F32) for j in range(nh)], axis=1))
    mixed = (ga * jnp.dot(ya, wpa_ref[...], preferred_element_type=F32)
             + gb * jnp.dot(yb, wpb_ref[...], preferred_element_type=F32))
    o_ref[...] = x_ref[...] + jnp.dot(mixed.astype(BF), wo_ref[...], preferred_element_type=F32)


def _merge(x2, ya, yb, p3, wpa, wpb, wo, tm):
    t, d = x2.shape
    nh = d // LANES
    hspec = pl.BlockSpec((nh, tm, LANES), lambda i: (0, i, 0))
    wspec = pl.BlockSpec((d, d), lambda i: (0, 0))
    return pl.pallas_call(
        _merge_body,
        grid=(t // tm,),
        in_specs=[pl.BlockSpec((tm, d), lambda i: (i, 0)), hspec, hspec,
                  pl.BlockSpec((nh, tm, LANES), lambda i: (CB_MGA // nh, i, 0)),
                  pl.BlockSpec((nh, tm, LANES), lambda i: (CB_MGB // nh, i, 0)),
                  wspec, wspec, wspec],
        out_specs=pl.BlockSpec((tm, d), lambda i: (i, 0)),
        out_shape=jax.ShapeDtypeStruct((t, d), F32),
        compiler_params=pltpu.CompilerParams(dimension_semantics=("parallel",)),
        name="merge",
    )(x2, ya, yb, p3, p3, wpa, wpb, wo)


def _ffn_body(x_ref, nw_ref, wg_ref, wu_ref, wd_ref, o_ref):
    x = x_ref[...]
    h = (x * lax.rsqrt(jnp.mean(x * x, axis=-1, keepdims=True) + NORM_EPS) * nw_ref[...]).astype(BF)
    gate = jnp.dot(h, wg_ref[...], preferred_element_type=F32)
    up = jnp.dot(h, wu_ref[...], preferred_element_type=F32)
    act = (_silu(gate) * up).astype(BF)
    o_ref[...] = x + jnp.dot(act, wd_ref[...], preferred_element_type=F32)


def _ffn(x2, norm_w, wg, wu, wd, tm):
    t, d = x2.shape
    f = wg.shape[1]
    return pl.pallas_call(
        _ffn_body,
        grid=(t // tm,),
        in_specs=[pl.BlockSpec((tm, d), lambda i: (i, 0)),
                  pl.BlockSpec((1, d), lambda i: (0, 0)),
                  pl.BlockSpec((d, f), lambda i: (0, 0)),
                  pl.BlockSpec((d, f), lambda i: (0, 0)),
                  pl.BlockSpec((f, d), lambda i: (0, 0))],
        out_specs=pl.BlockSpec((tm, d), lambda i: (i, 0)),
        out_shape=jax.ShapeDtypeStruct((t, d), F32),
        compiler_params=pltpu.CompilerParams(dimension_semantics=("parallel",)),
        name="ffn",
    )(x2, norm_w, wg, wu, wd)


def _arrange_w_in(w_in):
    o_ga = 4 * GDN_HEADS * GDN_DK
    o_gb = o_ga + GDN_HEADS
    o_nq = o_gb + GDN_HEADS
    o_nkv = o_nq + NSA_HEADS * NSA_DH
    o_ng = o_nkv + 6 * NSA_GROUPS * NSA_DH
    o_mg = o_ng + 3 * NSA_HEADS
    d = w_in.shape[0]
    w = w_in.astype(BF)
    pad = jnp.zeros((d, 2 * LANES - (o_nq - o_ga) - (o_mg - o_ng)), BF)
    return jnp.concatenate([w[:, :o_ga], w[:, o_nq:o_nkv], w[:, o_mg:], w[:, o_nkv:o_ng],
                            w[:, o_ga:o_nq], w[:, o_ng:o_mg], pad], axis=1)


def _overlap_t(s_):
    ncp = s_ // CMP_STRIDE
    nsel = s_ // SEL_BLOCK
    cs = np.arange(ncp) * CMP_STRIDE
    ss = np.arange(nsel) * SEL_BLOCK
    ov = (cs[None, :] < ss[:, None] + SEL_BLOCK) & (cs[None, :] + CMP_BLOCK > ss[:, None])
    ov[:, ncp - 1] = False
    return jnp.asarray(ov.astype(np.float32), BF)


def _sel_expand(s_):
    assert s_ // SEL_BLOCK <= LANES
    pos = np.arange(s_)
    e = (pos[:, None] // SEL_BLOCK == np.arange(LANES)[None, :]).astype(np.float32)
    return jnp.asarray(e, BF)


def kernel(x, norm1_w, w_in, conv_w, a_log, dt_bias, gdn_norm_w, cmp_pe, cmp_w1, cmp_w2, q_norm_w, k_norm_w,
           rel_bias, w_proj_a, w_proj_b, w_out, norm2_w, w_gate, w_up, w_down):
    b_, s_, d = x.shape
    t = b_ * s_
    x2 = x.reshape(t, d)
    tab_c, tab_s, tab_w = _bias_tables(rel_bias, s_)
    ovl_t = _overlap_t(s_)
    e_sel = _sel_expand(s_)
    for l in range(norm1_w.shape[0]):
        p3, sm = _proj(x2, norm1_w[l][None, :], _arrange_w_in(w_in[l]), tm=min(1024, t), tn=10 * LANES)
        conv_w3 = conv_w[l].reshape(GDN_CONV, 3 * GDN_HEADS, LANES)
        alog_b = jnp.pad(a_log[l], (0, LANES - GDN_HEADS))[None, :]
        dtb_b = jnp.pad(dt_bias[l], (0, LANES - GDN_HEADS))[None, :]
        y_a = _gdn(p3, sm, conv_w3, alog_b, dtb_b, gdn_norm_w[l][None, :], b_, s_)
        pe2 = cmp_pe[l].reshape(2, 2, CMP_STRIDE * NSA_DH)
        ckv = _cmp(p3, pe2, cmp_w1[l].astype(BF), cmp_w2[l].astype(BF), k_norm_w[l][0:1], b_, s_)
        y_b = _nsa(p3, sm, ckv, tab_c, tab_s, tab_w, ovl_t, e_sel, q_norm_w[l][None, :], k_norm_w[l], b_, s_)
        x2 = _merge(x2, y_a, y_b, p3, w_proj_a[l].astype(BF), w_proj_b[l].astype(BF), w_out[l].astype(BF),
                    tm=min(512, t))
        x2 = _ffn(x2, norm2_w[l][None, :], w_gate[l].astype(BF), w_up[l].astype(BF), w_down[l].astype(BF),
                  tm=min(512, t))
    return x2.reshape(b_, s_, d)
```

```python
import math

import numpy as np
import jax
import jax.numpy as jnp
from jax import lax
from jax.experimental import pallas as pl
from jax.experimental.pallas import tpu as pltpu

F32 = jnp.float32
BF = jnp.bfloat16

LANES = 128
SUBLANES = 8
GDN_HEADS = 8
GDN_DK = 128
GDN_DV = 128
GDN_CONV = 4
GDN_CHUNK = 64
NSA_HEADS = 8
NSA_GROUPS = 2
NSA_REP = NSA_HEADS // NSA_GROUPS
NSA_DH = 128
CMP_BLOCK = 32
CMP_STRIDE = 16
CMP_HIDDEN = 256
SEL_BLOCK = 64
SEL_TOPN = 16
WINDOW = 512
FORCE_BONUS = 1000.0
REL_BUCKETS = 32
REL_MAX_DIST = 1024
NORM_EPS = 1e-6
NEG = -1e30
M_INIT = -3e38
LOG2E = 1.4426950408889634

CB_GQ, CB_GK, CB_GV, CB_GZ = 0, 8, 16, 24
CB_NQ = 32
CB_MGA, CB_MGB = 40, 48
CB_KC, CB_VC, CB_KS, CB_VS, CB_KW, CB_VW = 56, 58, 60, 62, 64, 66
CB_SMALL = 68
SM_A, SM_B, SM_GATE = 0, 8, 16

PROJ_TM, PROJ_TN = 2048, 10 * LANES
MERGE_TM = 512
FFN_TM = 512
GDN_ROWS = 256
NSA_TQ = 256
NSA_TB = 128
NSA_TK_SEL = 512
NSA_TK_WIN = 256
NSA_WIN_TILES = (WINDOW + max(NSA_TQ, NSA_TK_WIN) - 2) // NSA_TK_WIN + 1
NSA_PAD = NSA_TK_SEL // NSA_TB - 1


def _mm(a, b):
    return jnp.dot(a.astype(BF), b.astype(BF), preferred_element_type=F32)


def _mm_nt(a, b):
    return lax.dot_general(a.astype(BF), b.astype(BF), (((1,), (1,)), ((), ())),
                           preferred_element_type=F32)


def _mm_tn(a, b):
    return lax.dot_general(a.astype(BF), b.astype(BF), (((0,), (0,)), ((), ())),
                           preferred_element_type=F32)


def _split3(x):
    x1 = x.astype(BF)
    r1 = x - x1.astype(F32)
    x2 = r1.astype(BF)
    x3 = (r1 - x2.astype(F32)).astype(BF)
    return x1, x2, x3


def _sigmoid(x):
    return 0.5 * jnp.tanh(0.5 * x) + 0.5


def _silu_of_half(h):
    return h + h * jnp.tanh(h)


def _silu(x):
    return _silu_of_half(0.5 * x)


def _softplus(x):
    return jnp.maximum(x, 0.0) + jnp.log(1.0 + jnp.exp(-jnp.abs(x)))


def _rel_thresholds():
    d = np.arange(0, 4 * REL_MAX_DIST, dtype=np.int64)
    max_exact = REL_BUCKETS // 2
    d_f = np.maximum(d, 1).astype(np.float32)
    large = max_exact + (np.log(d_f / np.float32(max_exact)) / np.float32(math.log(REL_MAX_DIST / max_exact))
                         * np.float32(REL_BUCKETS - max_exact)).astype(np.int32)
    large = np.minimum(large, REL_BUCKETS - 1)
    bucket = np.where(d < max_exact, d, large)
    assert np.all(np.diff(bucket) >= 0)
    return [int(np.argmax(bucket >= k)) for k in range(REL_BUCKETS)]


REL_THR = _rel_thresholds()


def _proj_body(x_ref, nw_ref, w_ref, o_ref, sm_ref, h_ref):
    @pl.when(pl.program_id(1) == 0)
    def _():
        x = x_ref[...]
        y = x * lax.rsqrt(jnp.mean(x * x, axis=-1, keepdims=True) + NORM_EPS)
        h_ref[...] = (y * nw_ref[...]).astype(BF)

    nb = o_ref.shape[0]
    r = jnp.dot(h_ref[...], w_ref[...], preferred_element_type=F32)
    for j in range(nb):
        o_ref[j] = r[:, j * LANES:(j + 1) * LANES].astype(BF)

    @pl.when(pl.program_id(1) == CB_SMALL // nb)
    def _():
        sm_ref[0] = r[:, (CB_SMALL % nb) * LANES:(CB_SMALL % nb + 1) * LANES]


def _proj(x2, norm_w, w_all, tm, tn):
    t, d = x2.shape
    n = w_all.shape[1]
    nb = tn // LANES
    return pl.pallas_call(
        _proj_body,
        grid=(t // tm, n // tn),
        in_specs=[pl.BlockSpec((tm, d), lambda i, j: (i, 0)),
                  pl.BlockSpec((1, d), lambda i, j: (0, 0)),
                  pl.BlockSpec((d, tn), lambda i, j: (0, j))],
        out_specs=[pl.BlockSpec((nb, tm, LANES), lambda i, j: (j, i, 0)),
                   pl.BlockSpec((1, tm, LANES), lambda i, j: (0, i, 0))],
        out_shape=[jax.ShapeDtypeStruct((n // LANES, t, LANES), BF),
                   jax.ShapeDtypeStruct((1, t, LANES), F32)],
        scratch_shapes=[pltpu.VMEM((tm, d), BF)],
        compiler_params=pltpu.CompilerParams(dimension_semantics=("parallel", "arbitrary")),
        name="proj",
    )(x2, norm_w, w_all)


def _gdn_body(q_ref, k_ref, v_ref, z_ref, sm_ref, cw_ref, alog_ref, dtb_ref, nw_ref, o_ref,
              ext_ref, st_ref):
    rows = GDN_ROWS
    c = GDN_CHUNK
    nchunk = rows // c
    halo = SUBLANES
    s = pl.program_id(1)

    @pl.when(s == 0)
    def _():
        ext_ref[:, 0:halo, :] = jnp.zeros((3 * GDN_HEADS, halo, LANES), F32)
        st_ref[...] = jnp.zeros_like(st_ref)

    for j in range(GDN_HEADS):
        ext_ref[j, halo:halo + rows, :] = q_ref[j].astype(F32)
        ext_ref[GDN_HEADS + j, halo:halo + rows, :] = k_ref[j].astype(F32)
        ext_ref[2 * GDN_HEADS + j, halo:halo + rows, :] = v_ref[j].astype(F32)

    ri = lax.broadcasted_iota(jnp.int32, (rows, rows), 0)
    ci = lax.broadcasted_iota(jnp.int32, (rows, rows), 1)
    l_tril = jnp.where(((ri // c) == (ci // c)) & (ri >= ci), 1.0, 0.0).astype(BF)
    rt = lax.broadcasted_iota(jnp.int32, (LANES, LANES), 0)
    ct = lax.broadcasted_iota(jnp.int32, (LANES, LANES), 1)
    same_t = (rt // c) == (ct // c)
    tril_t = same_t & (rt >= ct)
    strict_t = same_t & (rt > ct)
    eye = jnp.where(rt == ct, 1.0, 0.0)

    sm = sm_ref[0]
    lane = lax.broadcasted_iota(jnp.int32, (rows, LANES), 1)
    gall = jnp.where(lane < GDN_HEADS, -jnp.exp(alog_ref[...]) * _softplus(sm + dtb_ref[...]), 0.0)
    g1 = gall.astype(BF).astype(F32)
    r1 = gall - g1
    g2 = r1.astype(BF).astype(F32)
    packed = g1 + pltpu.roll(g2, GDN_HEADS, 1) + pltpu.roll(r1 - g2, 2 * GDN_HEADS, 1)
    gc = jnp.dot(l_tril, packed.astype(BF), preferred_element_type=F32)
    gcum_all = gc + pltpu.roll(gc, LANES - GDN_HEADS, 1) + pltpu.roll(gc, LANES - 2 * GDN_HEADS, 1)

    def conv_silu(j):
        first = halo - (GDN_CONV - 1)
        acc = (0.5 * cw_ref[0, pl.ds(j, 1), :]) * ext_ref[j, pl.ds(first, rows), :]
        for i in range(1, GDN_CONV):
            acc = acc + (0.5 * cw_ref[i, pl.ds(j, 1), :]) * ext_ref[j, pl.ds(first + i, rows), :]
        return _silu_of_half(acc)

    def head_setup(h):
        qh = conv_silu(h)
        kh = conv_silu(GDN_HEADS + h)
        vv = conv_silu(2 * GDN_HEADS + h)
        qn = qh * lax.rsqrt(jnp.sum(qh * qh, axis=-1, keepdims=True) + NORM_EPS) * (GDN_DK ** -0.5)
        kn = kh * lax.rsqrt(jnp.sum(kh * kh, axis=-1, keepdims=True) + NORM_EPS)

        beta = _sigmoid(jnp.broadcast_to(sm[:, SM_B + h:SM_B + h + 1], (rows, LANES)))
        gcum = jnp.broadcast_to(gcum_all[:, h:h + 1], (rows, LANES))
        glast = jnp.concatenate(
            [jnp.broadcast_to(gcum[(n + 1) * c - 1:(n + 1) * c, :], (c, LANES)) for n in range(nchunk)],
            axis=0)
        gct = gcum.T
        kb = kn * beta
        eg = jnp.exp(gcum)
        knb = kn.astype(BF)
        a_t, intra_t = [], []
        for u in range(rows // LANES):
            rs = slice(u * LANES, (u + 1) * LANES)
            diff = gcum[rs] - gct[:, rs]
            decay = jnp.where(tril_t, jnp.exp(jnp.where(tril_t, diff, 0.0)), 0.0)
            a_t.append(jnp.where(strict_t, _mm_nt(kb[rs], knb[rs]) * decay, 0.0))
            intra_t.append(_mm_nt(qn[rs], knb[rs]) * decay)
        return dict(
            a=a_t, intra=intra_t,
            rhs=jnp.concatenate([vv * beta, kb * eg], axis=1),
            qg=qn * eg, kdec=kn * jnp.exp(glast - gcum), cd=jnp.exp(glast))

    def all_heads():
        hs = list(range(GDN_HEADS))
        nt = rows // LANES
        w = [head_setup(h) for h in hs]
        a_all = [a_ for d in w for a_ in d["a"]]
        p = [_mm(a_, a_) for a_ in a_all]
        tg = [eye - a_ for a_ in a_all]
        for j in range(1, 6):
            tp = [_mm(ti, pi) for ti, pi in zip(tg, p)]
            if j < 5:
                p = [_mm(pi, pi) for pi in p]
            tg = [ti + tpi for ti, tpi in zip(tg, tp)]
        t = [tg[i * nt:(i + 1) * nt] for i in range(len(w))]
        sol = [[d["rhs"][u * LANES:(u + 1) * LANES] + _mm(tu - eye, d["rhs"][u * LANES:(u + 1) * LANES])
                for u, tu in enumerate(ti)] for d, ti in zip(w, t)]
        st = [st_ref[h] for h in hs]
        outs = [[] for _ in hs]
        cpt = LANES // c
        for n in range(nchunk):
            sl = slice(n * c, (n + 1) * c)
            lo = slice((n % cpt) * c, (n % cpt + 1) * c)
            ks = [_mm(jnp.concatenate([s_[n // cpt][lo, GDN_DV:], d["qg"][sl]], axis=0), si)
                  for s_, d, si in zip(sol, w, st)]
            vn = [s_[n // cpt][lo, :GDN_DV] - k_[:c] for s_, k_ in zip(sol, ks)]
            for u, (d, k_, v_) in enumerate(zip(w, ks, vn)):
                outs[u].append(k_[c:] + _mm(d["intra"][n // cpt][lo, lo], v_))
            st = [si * jnp.concatenate([d["cd"][sl], d["cd"][sl]], axis=0) + _mm_tn(d["kdec"][sl], v_)
                  for si, d, v_ in zip(st, w, vn)]
        for u, h in enumerate(hs):
            st_ref[h] = st[u]
            o = jnp.concatenate(outs[u], axis=0)
            on = o * lax.rsqrt(jnp.mean(o * o, axis=-1, keepdims=True) + NORM_EPS) * nw_ref[...]
            o_ref[h] = (on * _silu(z_ref[h].astype(F32))).astype(o_ref.dtype)

    all_heads()

    for j in range(3 * GDN_HEADS):
        ext_ref[j, 0:halo, :] = ext_ref[j, rows:rows + halo, :]


def _gdn(p3, sm, conv_w3, alog_b, dtb_b, gdn_norm_w, b_, s_):
    rows = GDN_ROWS
    ns = s_ // rows
    hb = GDN_HEADS

    def cb(base):
        return pl.BlockSpec((hb, rows, LANES), lambda b, s, base=base: (base // hb, b * ns + s, 0))

    return pl.pallas_call(
        _gdn_body,
        grid=(b_, ns),
        in_specs=[cb(CB_GQ), cb(CB_GK), cb(CB_GV), cb(CB_GZ),
                  pl.BlockSpec((1, rows, LANES), lambda b, s: (0, b * ns + s, 0)),
                  pl.BlockSpec((GDN_CONV, 3 * hb, LANES), lambda b, s: (0, 0, 0)),
                  pl.BlockSpec((1, LANES), lambda b, s: (0, 0)),
                  pl.BlockSpec((1, LANES), lambda b, s: (0, 0)),
                  pl.BlockSpec((1, LANES), lambda b, s: (0, 0))],
        out_specs=pl.BlockSpec((hb, rows, LANES), lambda b, s: (0, b * ns + s, 0)),
        out_shape=jax.ShapeDtypeStruct((hb, b_ * s_, LANES), BF),
        scratch_shapes=[pltpu.VMEM((3 * hb, rows + SUBLANES, LANES), F32),
                        pltpu.VMEM((hb, GDN_DK, GDN_DV), F32)],
        compiler_params=pltpu.CompilerParams(dimension_semantics=("parallel", "arbitrary")),
        name="gdn",
    )(p3, p3, p3, p3, sm, conv_w3, alog_b, dtb_b, gdn_norm_w)


def _cmp_body(x_ref, pe_ref, w1_ref, w2_ref, nw_ref, o_ref, c_ref, xf_ref):
    kv = pl.program_id(1)
    nch = c_ref.shape[0]
    half = CMP_STRIDE * NSA_DH
    xf_ref[...] = x_ref[0].astype(F32)
    for p in range(CMP_STRIDE):
        c_ref[:, p * NSA_DH:(p + 1) * NSA_DH] = xf_ref[pl.ds(p, nch, stride=CMP_STRIDE), :]
    cc = c_ref[...]
    u = _mm(cc + pe_ref[0, 0:1, :], w1_ref[0, 0:half, :])
    v = _mm(cc + pe_ref[0, 1:2, :], w1_ref[0, half:2 * half, :])
    v_next = jnp.concatenate([v[1:], v[:1]], axis=0)
    hid = _silu(u + v_next)
    out = _mm(hid, w2_ref[0])
    normed = out * lax.rsqrt(jnp.mean(out * out, axis=-1, keepdims=True) + NORM_EPS) * nw_ref[...]
    o_ref[0, 0, 0] = jnp.where(kv == 0, normed, out)


def _cmp(p3, pe2, w1, w2, knw0, b_, s_):
    nch = s_ // CMP_STRIDE
    g_ = NSA_GROUPS
    return pl.pallas_call(
        _cmp_body,
        grid=(b_, 2, g_),
        in_specs=[pl.BlockSpec((1, s_, LANES), lambda b, kv, g: (CB_KC + 2 * kv + g, b, 0)),
                  pl.BlockSpec((1, 2, CMP_STRIDE * NSA_DH), lambda b, kv, g: (kv, 0, 0)),
                  pl.BlockSpec((1, CMP_BLOCK * NSA_DH, CMP_HIDDEN), lambda b, kv, g: (kv, 0, 0)),
                  pl.BlockSpec((1, CMP_HIDDEN, NSA_DH), lambda b, kv, g: (kv, 0, 0)),
                  pl.BlockSpec((1, NSA_DH), lambda b, kv, g: (0, 0))],
        out_specs=pl.BlockSpec((1, 1, 1, nch, NSA_DH), lambda b, kv, g: (b, kv, g, 0, 0)),
        out_shape=jax.ShapeDtypeStruct((b_, 2, g_, nch, NSA_DH), F32),
        scratch_shapes=[pltpu.VMEM((nch, CMP_STRIDE * NSA_DH), F32), pltpu.VMEM((s_, LANES), F32)],
        compiler_params=pltpu.CompilerParams(dimension_semantics=("parallel", "arbitrary", "arbitrary")),
        name="cmp",
    )(p3, pe2, w1, w2, knw0)


def _bias_of(d, rb_ref, h):
    val = jnp.full(d.shape, rb_ref[0, h], F32)
    for k in range(1, REL_BUCKETS):
        val = jnp.where(d >= REL_THR[k], rb_ref[k, h], val)
    return val * LOG2E


def _bias_body(rb_ref, tc_ref, ts_ref, tw_ref):
    h = pl.program_id(0)
    _, s_, ncp = tc_ref.shape

    r = lax.broadcasted_iota(jnp.int32, (CMP_STRIDE, 2 * ncp), 0)
    k = lax.broadcasted_iota(jnp.int32, (CMP_STRIDE, 2 * ncp), 1)
    d = CMP_STRIDE * (ncp - 1 - k) + r - (CMP_BLOCK - 1)
    gen = jnp.where(d >= 0, _bias_of(d, rb_ref, h), NEG)

    def row_group(a, carry):
        row0 = pl.multiple_of(a * CMP_STRIDE, CMP_STRIDE)
        tc_ref[0, pl.ds(row0, CMP_STRIDE), :] = pltpu.roll(gen, (ncp + 1 + a) % (2 * ncp), 1)[:, :ncp]
        return carry

    lax.fori_loop(0, s_ // CMP_STRIDE, row_group, 0, unroll=8)
    i = lax.broadcasted_iota(jnp.int32, (NSA_TB, NSA_TB), 0)
    j = lax.broadcasted_iota(jnp.int32, (NSA_TB, NSA_TB), 1)
    for e in range(ts_ref.shape[1]):
        d = (e - NSA_PAD) * NSA_TB + i - j
        ts_ref[0, e] = jnp.where(d >= 0, _bias_of(d, rb_ref, h), NEG)
    for e in range(tw_ref.shape[1]):
        d = (e - NSA_PAD) * NSA_TB + i - j
        tw_ref[0, e] = jnp.where((d >= 0) & (d < WINDOW), _bias_of(d, rb_ref, h), NEG)


def _sel_table_len():
    a = 0
    while a * NSA_TB - (NSA_TB - 1) < REL_THR[REL_BUCKETS - 1]:
        a += 1
    return a + 1 + NSA_PAD


def _win_table_len():
    return (WINDOW + NSA_TB - 1) // NSA_TB + 2 + NSA_PAD


def _bias_tables(rel_bias, s_):
    ncp = s_ // CMP_STRIDE
    nd = _sel_table_len()
    nw = _win_table_len()
    return pl.pallas_call(
        _bias_body,
        grid=(NSA_HEADS,),
        in_specs=[pl.BlockSpec(memory_space=pltpu.SMEM)],
        out_specs=[pl.BlockSpec((1, s_, ncp), lambda h: (h, 0, 0)),
                   pl.BlockSpec((1, nd, NSA_TB, NSA_TB), lambda h: (h, 0, 0, 0)),
                   pl.BlockSpec((1, nw, NSA_TB, NSA_TB), lambda h: (h, 0, 0, 0))],
        out_shape=[jax.ShapeDtypeStruct((NSA_HEADS, s_, ncp), F32),
                   jax.ShapeDtypeStruct((NSA_HEADS, nd, NSA_TB, NSA_TB), F32),
                   jax.ShapeDtypeStruct((NSA_HEADS, nw, NSA_TB, NSA_TB), F32)],
        compiler_params=pltpu.CompilerParams(dimension_semantics=("parallel",)),
        name="bias",
    )(rel_bias)


def _rms_rows(x, w):
    return x * lax.rsqrt(jnp.mean(x * x, axis=-1, keepdims=True) + NORM_EPS) * w


def _nsa_body(q_ref, sm_ref, kc_ref, vc_ref, ks_ref, vs_ref, kw_ref, vw_ref, tc_ref, ts_ref, tw_ref,
              ovl_ref, e_ref, qnw_ref, knw_ref, o_ref,
              ksb, vsb, kwb, vwb, qa_ref, ms_ref, accs_ref, park_ref, sa_ref, sb_ref, wa_ref, wb_ref, wc_ref):
    tq, r_, tb = NSA_TQ, NSA_REP, NSA_TB
    g = pl.program_id(1)
    qi = pl.program_id(2)
    nsel = ovl_ref.shape[0]
    topn = min(SEL_TOPN, nsel)

    @pl.when(qi == 0)
    def _():
        ones = jnp.ones(vs_ref.shape[1:], BF)
        ksb[:, :NSA_DH] = _rms_rows(ks_ref[0].astype(F32), knw_ref[1:2, :]).astype(BF)
        ksb[:, NSA_DH:] = e_ref[...]
        kwb[...] = _rms_rows(kw_ref[0].astype(F32), knw_ref[2:3, :]).astype(BF)
        vsb[:, :NSA_DH] = vs_ref[0]
        vsb[:, NSA_DH:] = ones
        vwb[:, :NSA_DH] = vw_ref[0]
        vwb[:, NSA_DH:] = ones

    qscale = NSA_DH ** -0.5 * LOG2E
    for r in range(r_):
        qa_ref[r * tq:(r + 1) * tq, :NSA_DH] = (
            _rms_rows(q_ref[r].astype(F32), qnw_ref[...]) * qscale).astype(BF)
    qs = qa_ref[:, :NSA_DH]

    def logits(kj, q, k_sc, tk, dst_ref):
        off = pl.multiple_of(kj * tk, tk)
        dst_ref[:, :tk] = _mm_nt(q, k_sc[pl.ds(off, tk), :])

    def flash_init(m_ref, acc_ref):
        m_ref[...] = jnp.full(m_ref.shape, M_INIT, F32)
        acc_ref[...] = jnp.zeros_like(acc_ref)

    def flash_step(kj, s_ref, v_sc, tab_ref, tk, m_ref, acc_ref):
        nct = tk // tb
        nrt = tq // tb
        off = pl.multiple_of(kj * tk, tk)
        e0 = qi * nrt - kj * nct + NSA_PAD
        idx = {o: jnp.clip(e0 + o, 0, tab_ref.shape[1] - 1) for o in range(-(nct - 1), nrt)}
        m_prev = m_ref[...]
        m_rows, p_rows = [], []
        for rb in range(r_ * nrt):
            r, rho = divmod(rb, nrt)
            rs = slice(rb * tb, (rb + 1) * tb)
            pieces = [s_ref[rs, c * tb:(c + 1) * tb] + tab_ref[r, idx[rho - c]] for c in range(nct)]
            mx = pieces[0]
            for c in range(1, nct):
                mx = jnp.maximum(mx, pieces[c])
            m_next = jnp.maximum(m_prev[rs], jnp.max(mx, axis=-1, keepdims=True))
            m_rows.append(m_next)
            p_rows.append(jnp.concatenate([jnp.exp2(pc_ - m_next).astype(BF) for pc_ in pieces], axis=1))
        m_next = jnp.concatenate(m_rows, axis=0)
        p = jnp.concatenate(p_rows, axis=0)
        alpha = jnp.exp2(m_prev - m_next)
        acc_ref[...] = (jnp.concatenate([alpha, alpha], axis=1) * acc_ref[...]
                        + jnp.dot(p, v_sc[pl.ds(off, tk), :], preferred_element_type=F32))
        m_ref[...] = m_next

    def flash_result(acc_ref):
        acc = acc_ref[...]
        return acc[:, :NSA_DH] / jnp.maximum(acc[:, NSA_DH:], 1e-30)

    wfirst = jnp.maximum((qi * tq + tq - 1) // NSA_TK_WIN - (NSA_WIN_TILES - 1), 0)
    wbufs = (wa_ref, wb_ref, wc_ref)

    def window_output():
        nct = NSA_TK_WIN // tb
        nrt = tq // tb
        wk = NSA_WIN_TILES * NSA_TK_WIN
        idx = {(u, o): jnp.clip(qi * nrt - (wfirst + u) * nct + NSA_PAD + o, 0, tw_ref.shape[1] - 1)
               for u in range(NSA_WIN_TILES) for o in range(-(nct - 1), nrt)}
        p_rows = []
        for rb in range(r_ * nrt):
            r, rho = divmod(rb, nrt)
            rs = slice(rb * tb, (rb + 1) * tb)
            pieces = [wbufs[u][rs, c * tb:(c + 1) * tb] + tw_ref[r, idx[(u, rho - c)]]
                      for u in range(NSA_WIN_TILES) for c in range(nct)]
            mx = pieces[0]
            for pc_ in pieces[1:]:
                mx = jnp.maximum(mx, pc_)
            m = jnp.max(mx, axis=-1, keepdims=True)
            p_rows.append(jnp.concatenate([jnp.exp2(pc_ - m).astype(BF) for pc_ in pieces], axis=1))
        p = jnp.concatenate(p_rows, axis=0)
        off = pl.multiple_of(wfirst * NSA_TK_WIN, NSA_TK_WIN)
        pv = jnp.dot(p, vwb[pl.ds(off, wk), :], preferred_element_type=F32)
        return pv[:, :NSA_DH] / jnp.maximum(pv[:, NSA_DH:], 1e-30)

    lc = _mm_nt(qs, kc_ref[0, 0, 0]) + jnp.concatenate([tc_ref[r] for r in range(r_)], axis=0)
    for u in range(NSA_WIN_TILES):
        logits(wfirst + u, qs, kwb, NSA_TK_WIN, wbufs[u])
    mc = jnp.max(lc, axis=-1, keepdims=True)
    pc = jnp.exp2(lc - mc)
    lsum = jnp.sum(pc, axis=-1, keepdims=True)
    pc = pc * jnp.where(mc > 0.5 * NEG, 1.0 / jnp.maximum(lsum, 1e-30), 0.0)

    psum = pc[0:tq]
    for r in range(1, r_):
        psum = psum + pc[r * tq:(r + 1) * tq]
    s1, s2, s3 = _split3(psum)
    dn = (((1,), (1,)), ((), ()))
    ovl = ovl_ref[...]
    imp_t = (lax.dot_general(ovl, s1, dn, preferred_element_type=F32)
             + lax.dot_general(ovl, s2, dn, preferred_element_type=F32)
             + lax.dot_general(ovl, s3, dn, preferred_element_type=F32))

    oc = _mm(pc, vc_ref[0, 0, 0])
    o_w = window_output()

    sg = _sigmoid(sm_ref[0])

    def gate(r, br):
        c0 = SM_GATE + r * 3 + br
        c1 = SM_GATE + (r_ + r) * 3 + br
        return jnp.where(g == 0, sg[:, c0:c0 + 1], sg[:, c1:c1 + 1])

    for r in range(r_):
        rs = slice(r * tq, (r + 1) * tq)
        park_ref[rs, :] = gate(r, 0) * oc[rs] + gate(r, 2) * o_w[rs]

    jb = lax.broadcasted_iota(jnp.int32, (nsel, tq), 0)
    tpos = qi * tq + lax.broadcasted_iota(jnp.int32, (nsel, tq), 1)
    tblk = tpos // SEL_BLOCK
    forced = (jb == 0) | (jb == tblk) | (jb == tblk - 1)
    score = jnp.where(jb <= tblk, imp_t + jnp.where(forced, FORCE_BONUS, 0.0), NEG)
    sl8 = SUBLANES
    sub = lax.broadcasted_iota(jnp.int32, (sl8, tq), 0)
    groups = [score[sl8 * v:sl8 * (v + 1)] for v in range(nsel // sl8)]
    counts = [jnp.zeros((sl8, tq), F32) for _ in groups]
    for jp in range(nsel):
        row = jnp.broadcast_to(score[jp:jp + 1, :], (sl8, tq))
        for v, grp in enumerate(groups):
            if sl8 * v > jp:
                beats = jnp.where(row >= grp, 1.0, 0.0)
            elif sl8 * (v + 1) <= jp:
                beats = jnp.where(row > grp, 1.0, 0.0)
            else:
                tie = jnp.where(sub + sl8 * v > jp, 1.0, 0.0)
                beats = jnp.where(row > grp, 1.0, jnp.where(row == grp, tie, 0.0))
            counts[v] = counts[v] + beats
    unsel_t = jnp.where(jnp.concatenate(counts, axis=0) < topn, 0.0, NEG)
    unsel_pad = jnp.concatenate([unsel_t, jnp.zeros((LANES - nsel, tq), F32)], axis=0).astype(BF)
    ri = lax.broadcasted_iota(jnp.int32, (tq, tq), 0)
    ci = lax.broadcasted_iota(jnp.int32, (tq, tq), 1)
    eye = jnp.where(ri == ci, 1.0, 0.0).astype(BF)
    unsel_q = lax.dot_general(eye, unsel_pad, dn, preferred_element_type=F32).astype(BF)
    for r in range(r_):
        qa_ref[r * tq:(r + 1) * tq, NSA_DH:] = unsel_q

    qa = qa_ref[...]
    slast = (qi * tq + tq - 1) // NSA_TK_SEL
    flash_init(ms_ref, accs_ref)
    logits(0, qa, ksb, NSA_TK_SEL, sa_ref)

    def sel_step(kj, s_ref):
        flash_step(kj, s_ref, vsb, ts_ref, NSA_TK_SEL, ms_ref, accs_ref)

    def pair(i, carry):
        kj = 2 * i
        logits(kj + 1, qa, ksb, NSA_TK_SEL, sb_ref)
        sel_step(kj, sa_ref)
        logits(jnp.minimum(kj + 2, slast), qa, ksb, NSA_TK_SEL, sa_ref)
        sel_step(kj + 1, sb_ref)
        return carry

    lax.fori_loop(0, (slast + 1) // 2, pair, 0)

    @pl.when(slast % 2 == 0)
    def _():
        sel_step(slast, sa_ref)

    o_s = flash_result(accs_ref)
    sg = _sigmoid(sm_ref[0])
    for r in range(r_):
        rs = slice(r * tq, (r + 1) * tq)
        o_ref[r] = (park_ref[rs, :] + gate(r, 1) * o_s[rs]).astype(o_ref.dtype)


def _nsa(p3, sm, ckv, tab_c, tab_s, tab_w, ovl_t, e_sel, qnw, knw, b_, s_):
    tq, tk, r_ = NSA_TQ, NSA_TB, NSA_REP
    nq = s_ // tq
    ncp = s_ // CMP_STRIDE
    nsel = s_ // SEL_BLOCK
    nd = tab_s.shape[1]
    nw = tab_w.shape[1]
    assert s_ % NSA_TK_SEL == 0 and NSA_TK_SEL // NSA_TB - 1 <= NSA_PAD
    assert s_ >= NSA_WIN_TILES * NSA_TK_WIN and NSA_WIN_TILES == 3
    assert NSA_TQ % NSA_TB == 0 and max(NSA_TQ, NSA_TK_WIN) % min(NSA_TQ, NSA_TK_WIN) == 0

    def kv_spec(base):
        return pl.BlockSpec((1, s_, LANES), lambda b, g, q, base=base: (base + g, b, 0))

    return pl.pallas_call(
        _nsa_body,
        grid=(b_, NSA_GROUPS, nq),
        in_specs=[pl.BlockSpec((r_, tq, LANES), lambda b, g, q: (CB_NQ // r_ + g, b * nq + q, 0)),
                  pl.BlockSpec((1, tq, LANES), lambda b, g, q: (0, b * nq + q, 0)),
                  pl.BlockSpec((1, 1, 1, ncp, NSA_DH), lambda b, g, q: (b, 0, g, 0, 0)),
                  pl.BlockSpec((1, 1, 1, ncp, NSA_DH), lambda b, g, q: (b, 1, g, 0, 0)),
                  kv_spec(CB_KS), kv_spec(CB_VS), kv_spec(CB_KW), kv_spec(CB_VW),
                  pl.BlockSpec((r_, tq, ncp), lambda b, g, q: (g, q, 0)),
                  pl.BlockSpec((r_, nd, tk, tk), lambda b, g, q: (g, 0, 0, 0)),
                  pl.BlockSpec((r_, nw, tk, tk), lambda b, g, q: (g, 0, 0, 0)),
                  pl.BlockSpec((nsel, ncp), lambda b, g, q: (0, 0)),
                  pl.BlockSpec((s_, LANES), lambda b, g, q: (0, 0)),
                  pl.BlockSpec((1, NSA_DH), lambda b, g, q: (0, 0)),
                  pl.BlockSpec((3, NSA_DH), lambda b, g, q: (0, 0))],
        out_specs=pl.BlockSpec((r_, tq, LANES), lambda b, g, q: (g, b * nq + q, 0)),
        out_shape=jax.ShapeDtypeStruct((NSA_HEADS, b_ * s_, LANES), BF),
        scratch_shapes=[pltpu.VMEM((s_, 2 * NSA_DH), BF), pltpu.VMEM((s_, 2 * NSA_DH), BF),
                        pltpu.VMEM((s_, NSA_DH), BF), pltpu.VMEM((s_, 2 * NSA_DH), BF),
                        pltpu.VMEM((r_ * tq, 2 * NSA_DH), BF),
                        pltpu.VMEM((r_ * tq, LANES), F32), pltpu.VMEM((r_ * tq, 2 * NSA_DH), F32),
                        pltpu.VMEM((r_ * tq, NSA_DH), F32),
                        pltpu.VMEM((r_ * tq, NSA_TK_SEL), F32), pltpu.VMEM((r_ * tq, NSA_TK_SEL), F32)]
        + [pltpu.VMEM((r_ * tq, NSA_TK_WIN), F32)] * NSA_WIN_TILES,
        compiler_params=pltpu.CompilerParams(dimension_semantics=("parallel", "arbitrary", "arbitrary")),
        name="nsa",
    )(p3, sm, ckv, ckv, p3, p3, p3, p3, tab_c, tab_s, tab_w, ovl_t, e_sel, qnw, knw)


def _merge_body(x_ref, ya_ref, yb_ref, ga_ref, gb_ref, wpa_ref, wpb_ref, wo_ref, o_ref):
    nh = ya_ref.shape[0]
    ya = jnp.concatenate([ya_ref[j] for j in range(nh)], axis=1)
    yb = jnp.concatenate([yb_ref[j] for j in range(nh)], axis=1)
    ga = _sigmoid(jnp.concatenate([ga_ref[j].astype(F32) for j in range(nh)], axis=1))
    gb = _sigmoid(jnp.concatenate([gb_ref[j].astype(F32) for j in range(nh)], axis=1))
    mixed = (ga * jnp.dot(ya, wpa_ref[...], preferred_element_type=F32)
             + gb * jnp.dot(yb, wpb_ref[...], preferred_element_type=F32))
    o_ref[...] = x_ref[...] + jnp.dot(mixed.astype(BF), wo_ref[...], preferred_element_type=F32)


def _merge(x2, ya, yb, p3, wpa, wpb, wo, tm):
    t, d = x2.shape
    nh = d // LANES
    hspec = pl.BlockSpec((nh, tm, LANES), lambda i: (0, i, 0))
    wspec = pl.BlockSpec((d, d), lambda i: (0, 0))
    return pl.pallas_call(
        _merge_body,
        grid=(t // tm,),
        in_specs=[pl.BlockSpec((tm, d), lambda i: (i, 0)), hspec, hspec,
                  pl.BlockSpec((nh, tm, LANES), lambda i: (CB_MGA // nh, i, 0)),
                  pl.BlockSpec((nh, tm, LANES), lambda i: (CB_MGB // nh, i, 0)),
                  wspec, wspec, wspec],
        out_specs=pl.BlockSpec((tm, d), lambda i: (i, 0)),
        out_shape=jax.ShapeDtypeStruct((t, d), F32),
        compiler_params=pltpu.CompilerParams(dimension_semantics=("parallel",)),
        name="merge",
    )(x2, ya, yb, p3, p3, wpa, wpb, wo)


def _ffn_body(x_ref, nw_ref, wg_ref, wu_ref, wd_ref, o_ref):
    x = x_ref[...]
    h = (x * lax.rsqrt(jnp.mean(x * x, axis=-1, keepdims=True) + NORM_EPS) * nw_ref[...]).astype(BF)
    gate = jnp.dot(h, wg_ref[...], preferred_element_type=F32)
    up = jnp.dot(h, wu_ref[...], preferred_element_type=F32)
    act = (_silu(gate) * up).astype(BF)
    o_ref[...] = x + jnp.dot(act, wd_ref[...], preferred_element_type=F32)


def _ffn(x2, norm_w, wg, wu, wd, tm):
    t, d = x2.shape
    f = wg.shape[1]
    return pl.pallas_call(
        _ffn_body,
        grid=(t // tm,),
        in_specs=[pl.BlockSpec((tm, d), lambda i: (i, 0)),
                  pl.BlockSpec((1, d), lambda i: (0, 0)),
                  pl.BlockSpec((d, f), lambda i: (0, 0)),
                  pl.BlockSpec((d, f), lambda i: (0, 0)),
                  pl.BlockSpec((f, d), lambda i: (0, 0))],
        out_specs=pl.BlockSpec((tm, d), lambda i: (i, 0)),
        out_shape=jax.ShapeDtypeStruct((t, d), F32),
        compiler_params=pltpu.CompilerParams(dimension_semantics=("parallel",)),
        name="ffn",
    )(x2, norm_w, wg, wu, wd)


def _arrange_w_in(w_in):
    o_ga = 4 * GDN_HEADS * GDN_DK
    o_gb = o_ga + GDN_HEADS
    o_nq = o_gb + GDN_HEADS
    o_nkv = o_nq + NSA_HEADS * NSA_DH
    o_ng = o_nkv + 6 * NSA_GROUPS * NSA_DH
    o_mg = o_ng + 3 * NSA_HEADS
    d = w_in.shape[0]
    w = w_in.astype(BF)
    pad = jnp.zeros((d, 2 * LANES - (o_nq - o_ga) - (o_mg - o_ng)), BF)
    return jnp.concatenate([w[:, :o_ga], w[:, o_nq:o_nkv], w[:, o_mg:], w[:, o_nkv:o_ng],
                            w[:, o_ga:o_nq], w[:, o_ng:o_mg], pad], axis=1)


def _overlap_t(s_):
    ncp = s_ // CMP_STRIDE
    nsel = s_ // SEL_BLOCK
    cs = np.arange(ncp) * CMP_STRIDE
    ss = np.arange(nsel) * SEL_BLOCK
    ov = (cs[None, :] < ss[:, None] + SEL_BLOCK) & (cs[None, :] + CMP_BLOCK > ss[:, None])
    ov[:, ncp - 1] = False
    return jnp.asarray(ov.astype(np.float32), BF)


def _sel_expand(s_):
    assert s_ // SEL_BLOCK <= LANES
    pos = np.arange(s_)
    e = (pos[:, None] // SEL_BLOCK == np.arange(LANES)[None, :]).astype(np.float32)
    return jnp.asarray(e, BF)


def kernel(x, norm1_w, w_in, conv_w, a_log, dt_bias, gdn_norm_w, cmp_pe, cmp_w1, cmp_w2, q_norm_w, k_norm_w,
           rel_bias, w_proj_a, w_proj_b, w_out, norm2_w, w_gate, w_up, w_down):
    b_, s_, d = x.shape
    t = b_ * s_
    x2 = x.reshape(t, d)
    tab_c, tab_s, tab_w = _bias_tables(rel_bias, s_)
    ovl_t = _overlap_t(s_)
    e_sel = _sel_expand(s_)
    for l in range(norm1_w.shape[0]):
        p3, sm = _proj(x2, norm1_w[l][None, :], _arrange_w_in(w_in[l]), tm=min(PROJ_TM, t), tn=PROJ_TN)
        conv_w3 = conv_w[l].reshape(GDN_CONV, 3 * GDN_HEADS, LANES)
        alog_b = jnp.pad(a_log[l], (0, LANES - GDN_HEADS))[None, :]
        dtb_b = jnp.pad(dt_bias[l], (0, LANES - GDN_HEADS))[None, :]
        y_a = _gdn(p3, sm, conv_w3, alog_b, dtb_b, gdn_norm_w[l][None, :], b_, s_)
        pe2 = cmp_pe[l].reshape(2, 2, CMP_STRIDE * NSA_DH)
        ckv = _cmp(p3, pe2, cmp_w1[l].astype(BF), cmp_w2[l].astype(BF), k_norm_w[l][0:1], b_, s_)
        y_b = _nsa(p3, sm, ckv, tab_c, tab_s, tab_w, ovl_t, e_sel, q_norm_w[l][None, :], k_norm_w[l], b_, s_)
        x2 = _merge(x2, y_a, y_b, p3, w_proj_a[l].astype(BF), w_proj_b[l].astype(BF), w_out[l].astype(BF),
                    tm=min(MERGE_TM, t))
        x2 = _ffn(x2, norm2_w[l][None, :], w_gate[l].astype(BF), w_up[l].astype(BF), w_down[l].astype(BF),
                  tm=min(FFN_TM, t))
    return x2.reshape(b_, s_, d)
```

```python
import math

import numpy as np
import jax
import jax.numpy as jnp
from jax import lax
from jax.experimental import pallas as pl
from jax.experimental.pallas import tpu as pltpu

F32 = jnp.float32
BF = jnp.bfloat16

LANES = 128
SUBLANES = 8
GDN_HEADS = 8
GDN_DK = 128
GDN_DV = 128
GDN_CONV = 4
GDN_CHUNK = 64
NSA_HEADS = 8
NSA_GROUPS = 2
NSA_REP = NSA_HEADS // NSA_GROUPS
NSA_DH = 128
CMP_BLOCK = 32
CMP_STRIDE = 16
CMP_HIDDEN = 256
SEL_BLOCK = 64
SEL_TOPN = 16
WINDOW = 512
FORCE_BONUS = 1000.0
REL_BUCKETS = 32
REL_MAX_DIST = 1024
NORM_EPS = 1e-6
NEG = -1e30
M_INIT = -3e38
LOG2E = 1.4426950408889634

CB_GQ, CB_GK, CB_GV, CB_GZ = 0, 8, 16, 24
CB_NQ = 32
CB_MGA, CB_MGB = 40, 48
CB_KC, CB_VC, CB_KS, CB_VS, CB_KW, CB_VW = 56, 58, 60, 62, 64, 66
CB_SMALL = 68
SM_A, SM_B, SM_GATE = 0, 8, 16

PROJ_TM, PROJ_TN = 2048, 10 * LANES
MERGE_TM = 512
FFN_TM = 512
GDN_ROWS = 256
NSA_TQ = 256
NSA_TB = 128
NSA_TK_SEL = 512
NSA_TK_WIN = 256
NSA_WIN_TILES = (WINDOW + max(NSA_TQ, NSA_TK_WIN) - 2) // NSA_TK_WIN + 1
NSA_PAD = NSA_TK_SEL // NSA_TB - 1


def _mm(a, b):
    return jnp.dot(a.astype(BF), b.astype(BF), preferred_element_type=F32)


def _mm_nt(a, b):
    return lax.dot_general(a.astype(BF), b.astype(BF), (((1,), (1,)), ((), ())),
                           preferred_element_type=F32)


def _mm_tn(a, b):
    return lax.dot_general(a.astype(BF), b.astype(BF), (((0,), (0,)), ((), ())),
                           preferred_element_type=F32)


def _split3(x):
    x1 = x.astype(BF)
    r1 = x - x1.astype(F32)
    x2 = r1.astype(BF)
    x3 = (r1 - x2.astype(F32)).astype(BF)
    return x1, x2, x3


def _sigmoid(x):
    return 0.5 * jnp.tanh(0.5 * x) + 0.5


def _silu_of_half(h):
    return h + h * jnp.tanh(h)


def _silu(x):
    return _silu_of_half(0.5 * x)


def _softplus(x):
    return jnp.maximum(x, 0.0) + jnp.log(1.0 + jnp.exp(-jnp.abs(x)))


def _rel_thresholds():
    d = np.arange(0, 4 * REL_MAX_DIST, dtype=np.int64)
    max_exact = REL_BUCKETS // 2
    d_f = np.maximum(d, 1).astype(np.float32)
    large = max_exact + (np.log(d_f / np.float32(max_exact)) / np.float32(math.log(REL_MAX_DIST / max_exact))
                         * np.float32(REL_BUCKETS - max_exact)).astype(np.int32)
    large = np.minimum(large, REL_BUCKETS - 1)
    bucket = np.where(d < max_exact, d, large)
    assert np.all(np.diff(bucket) >= 0)
    return [int(np.argmax(bucket >= k)) for k in range(REL_BUCKETS)]


REL_THR = _rel_thresholds()


def _proj_body(x_ref, nw_ref, w_ref, o_ref, sm_ref, h_ref):
    @pl.when(pl.program_id(1) == 0)
    def _():
        x = x_ref[...]
        y = x * lax.rsqrt(jnp.mean(x * x, axis=-1, keepdims=True) + NORM_EPS)
        h_ref[...] = (y * nw_ref[...]).astype(BF)

    nb = o_ref.shape[0]
    r = jnp.dot(h_ref[...], w_ref[...], preferred_element_type=F32)
    for j in range(nb):
        o_ref[j] = r[:, j * LANES:(j + 1) * LANES].astype(BF)

    @pl.when(pl.program_id(1) == CB_SMALL // nb)
    def _():
        sm_ref[0] = r[:, (CB_SMALL % nb) * LANES:(CB_SMALL % nb + 1) * LANES]


def _proj(x2, norm_w, w_all, tm, tn):
    t, d = x2.shape
    n = w_all.shape[1]
    nb = tn // LANES
    return pl.pallas_call(
        _proj_body,
        grid=(t // tm, n // tn),
        in_specs=[pl.BlockSpec((tm, d), lambda i, j: (i, 0)),
                  pl.BlockSpec((1, d), lambda i, j: (0, 0)),
                  pl.BlockSpec((d, tn), lambda i, j: (0, j))],
        out_specs=[pl.BlockSpec((nb, tm, LANES), lambda i, j: (j, i, 0)),
                   pl.BlockSpec((1, tm, LANES), lambda i, j: (0, i, 0))],
        out_shape=[jax.ShapeDtypeStruct((n // LANES, t, LANES), BF),
                   jax.ShapeDtypeStruct((1, t, LANES), F32)],
        scratch_shapes=[pltpu.VMEM((tm, d), BF)],
        compiler_params=pltpu.CompilerParams(dimension_semantics=("parallel", "arbitrary")),
        name="proj",
    )(x2, norm_w, w_all)


def _gdn_body(q_ref, k_ref, v_ref, z_ref, sm_ref, cw_ref, alog_ref, dtb_ref, nw_ref, o_ref,
              ext_ref, st_ref):
    rows = GDN_ROWS
    c = GDN_CHUNK
    nchunk = rows // c
    halo = SUBLANES
    s = pl.program_id(1)

    @pl.when(s == 0)
    def _():
        ext_ref[:, 0:halo, :] = jnp.zeros((3 * GDN_HEADS, halo, LANES), F32)
        st_ref[...] = jnp.zeros_like(st_ref)

    for j in range(GDN_HEADS):
        ext_ref[j, halo:halo + rows, :] = q_ref[j].astype(F32)
        ext_ref[GDN_HEADS + j, halo:halo + rows, :] = k_ref[j].astype(F32)
        ext_ref[2 * GDN_HEADS + j, halo:halo + rows, :] = v_ref[j].astype(F32)

    ri = lax.broadcasted_iota(jnp.int32, (rows, rows), 0)
    ci = lax.broadcasted_iota(jnp.int32, (rows, rows), 1)
    l_tril = jnp.where(((ri // c) == (ci // c)) & (ri >= ci), 1.0, 0.0).astype(BF)
    rt = lax.broadcasted_iota(jnp.int32, (LANES, LANES), 0)
    ct = lax.broadcasted_iota(jnp.int32, (LANES, LANES), 1)
    same_t = (rt // c) == (ct // c)
    tril_t = same_t & (rt >= ct)
    strict_t = same_t & (rt > ct)
    eye = jnp.where(rt == ct, 1.0, 0.0)

    sm = sm_ref[0]
    lane = lax.broadcasted_iota(jnp.int32, (rows, LANES), 1)
    gall = jnp.where(lane < GDN_HEADS, -jnp.exp(alog_ref[...]) * _softplus(sm + dtb_ref[...]), 0.0)
    g1 = gall.astype(BF).astype(F32)
    r1 = gall - g1
    g2 = r1.astype(BF).astype(F32)
    packed = g1 + pltpu.roll(g2, GDN_HEADS, 1) + pltpu.roll(r1 - g2, 2 * GDN_HEADS, 1)
    gc = jnp.dot(l_tril, packed.astype(BF), preferred_element_type=F32)
    gcum_all = gc + pltpu.roll(gc, LANES - GDN_HEADS, 1) + pltpu.roll(gc, LANES - 2 * GDN_HEADS, 1)

    def conv_silu(j):
        first = halo - (GDN_CONV - 1)
        acc = (0.5 * cw_ref[0, pl.ds(j, 1), :]) * ext_ref[j, pl.ds(first, rows), :]
        for i in range(1, GDN_CONV):
            acc = acc + (0.5 * cw_ref[i, pl.ds(j, 1), :]) * ext_ref[j, pl.ds(first + i, rows), :]
        return _silu_of_half(acc)

    def head_setup(h):
        qh = conv_silu(h)
        kh = conv_silu(GDN_HEADS + h)
        vv = conv_silu(2 * GDN_HEADS + h)
        qn = qh * lax.rsqrt(jnp.sum(qh * qh, axis=-1, keepdims=True) + NORM_EPS) * (GDN_DK ** -0.5)
        kn = kh * lax.rsqrt(jnp.sum(kh * kh, axis=-1, keepdims=True) + NORM_EPS)

        beta = _sigmoid(jnp.broadcast_to(sm[:, SM_B + h:SM_B + h + 1], (rows, LANES)))
        gcum = jnp.broadcast_to(gcum_all[:, h:h + 1], (rows, LANES))
        glast = jnp.concatenate(
            [jnp.broadcast_to(gcum[(n + 1) * c - 1:(n + 1) * c, :], (c, LANES)) for n in range(nchunk)],
            axis=0)
        gct = gcum.T
        kb = kn * beta
        eg = jnp.exp(gcum)
        knb = kn.astype(BF)
        a_t, intra_t = [], []
        for u in range(rows // LANES):
            rs = slice(u * LANES, (u + 1) * LANES)
            diff = gcum[rs] - gct[:, rs]
            decay = jnp.where(tril_t, jnp.exp(jnp.where(tril_t, diff, 0.0)), 0.0)
            a_t.append(jnp.where(strict_t, _mm_nt(kb[rs], knb[rs]) * decay, 0.0))
            intra_t.append(_mm_nt(qn[rs], knb[rs]) * decay)
        return dict(
            a=a_t, intra=intra_t,
            rhs=jnp.concatenate([vv * beta, kb * eg], axis=1),
            qg=qn * eg, kdec=kn * jnp.exp(glast - gcum), cd=jnp.exp(glast))

    def all_heads():
        hs = list(range(GDN_HEADS))
        nt = rows // LANES
        w = [head_setup(h) for h in hs]
        a_all = [a_ for d in w for a_ in d["a"]]
        p = [_mm(a_, a_) for a_ in a_all]
        tg = [eye - a_ for a_ in a_all]
        for j in range(1, 6):
            tp = [_mm(ti, pi) for ti, pi in zip(tg, p)]
            if j < 5:
                p = [_mm(pi, pi) for pi in p]
            tg = [ti + tpi for ti, tpi in zip(tg, tp)]
        t = [tg[i * nt:(i + 1) * nt] for i in range(len(w))]
        sol = [[d["rhs"][u * LANES:(u + 1) * LANES] + _mm(tu - eye, d["rhs"][u * LANES:(u + 1) * LANES])
                for u, tu in enumerate(ti)] for d, ti in zip(w, t)]
        st = [st_ref[h] for h in hs]
        outs = [[] for _ in hs]
        cpt = LANES // c
        for n in range(nchunk):
            sl = slice(n * c, (n + 1) * c)
            lo = slice((n % cpt) * c, (n % cpt + 1) * c)
            ks = [_mm(jnp.concatenate([s_[n // cpt][lo, GDN_DV:], d["qg"][sl]], axis=0), si)
                  for s_, d, si in zip(sol, w, st)]
            vn = [s_[n // cpt][lo, :GDN_DV] - k_[:c] for s_, k_ in zip(sol, ks)]
            for u, (d, k_, v_) in enumerate(zip(w, ks, vn)):
                outs[u].append(k_[c:] + _mm(d["intra"][n // cpt][lo, lo], v_))
            st = [si * jnp.concatenate([d["cd"][sl], d["cd"][sl]], axis=0) + _mm_tn(d["kdec"][sl], v_)
                  for si, d, v_ in zip(st, w, vn)]
        for u, h in enumerate(hs):
            st_ref[h] = st[u]
            o = jnp.concatenate(outs[u], axis=0)
            on = o * lax.rsqrt(jnp.mean(o * o, axis=-1, keepdims=True) + NORM_EPS) * nw_ref[...]
            o_ref[h] = (on * _silu(z_ref[h].astype(F32))).astype(o_ref.dtype)

    all_heads()

    for j in range(3 * GDN_HEADS):
        ext_ref[j, 0:halo, :] = ext_ref[j, rows:rows + halo, :]


def _gdn(p3, sm, conv_w3, alog_b, dtb_b, gdn_norm_w, b_, s_):
    rows = GDN_ROWS
    ns = s_ // rows
    hb = GDN_HEADS

    def cb(base):
        return pl.BlockSpec((hb, rows, LANES), lambda b, s, base=base: (base // hb, b * ns + s, 0))

    return pl.pallas_call(
        _gdn_body,
        grid=(b_, ns),
        in_specs=[cb(CB_GQ), cb(CB_GK), cb(CB_GV), cb(CB_GZ),
                  pl.BlockSpec((1, rows, LANES), lambda b, s: (0, b * ns + s, 0)),
                  pl.BlockSpec((GDN_CONV, 3 * hb, LANES), lambda b, s: (0, 0, 0)),
                  pl.BlockSpec((1, LANES), lambda b, s: (0, 0)),
                  pl.BlockSpec((1, LANES), lambda b, s: (0, 0)),
                  pl.BlockSpec((1, LANES), lambda b, s: (0, 0))],
        out_specs=pl.BlockSpec((hb, rows, LANES), lambda b, s: (0, b * ns + s, 0)),
        out_shape=jax.ShapeDtypeStruct((hb, b_ * s_, LANES), BF),
        scratch_shapes=[pltpu.VMEM((3 * hb, rows + SUBLANES, LANES), F32),
                        pltpu.VMEM((hb, GDN_DK, GDN_DV), F32)],
        compiler_params=pltpu.CompilerParams(dimension_semantics=("parallel", "arbitrary")),
        name="gdn",
    )(p3, p3, p3, p3, sm, conv_w3, alog_b, dtb_b, gdn_norm_w)


def _cmp_body(x_ref, pe_ref, w1_ref, w2_ref, nw_ref, o_ref, c_ref, xf_ref):
    kv = pl.program_id(1)
    nch = c_ref.shape[0]
    half = CMP_STRIDE * NSA_DH
    xf_ref[...] = x_ref[0].astype(F32)
    for p in range(CMP_STRIDE):
        c_ref[:, p * NSA_DH:(p + 1) * NSA_DH] = xf_ref[pl.ds(p, nch, stride=CMP_STRIDE), :]
    cc = c_ref[...]
    u = _mm(cc + pe_ref[0, 0:1, :], w1_ref[0, 0:half, :])
    v = _mm(cc + pe_ref[0, 1:2, :], w1_ref[0, half:2 * half, :])
    v_next = jnp.concatenate([v[1:], v[:1]], axis=0)
    hid = _silu(u + v_next)
    out = _mm(hid, w2_ref[0])
    normed = out * lax.rsqrt(jnp.mean(out * out, axis=-1, keepdims=True) + NORM_EPS) * nw_ref[...]
    o_ref[0, 0, 0] = jnp.where(kv == 0, normed, out)


def _cmp(p3, pe2, w1, w2, knw0, b_, s_):
    nch = s_ // CMP_STRIDE
    g_ = NSA_GROUPS
    return pl.pallas_call(
        _cmp_body,
        grid=(b_, 2, g_),
        in_specs=[pl.BlockSpec((1, s_, LANES), lambda b, kv, g: (CB_KC + 2 * kv + g, b, 0)),
                  pl.BlockSpec((1, 2, CMP_STRIDE * NSA_DH), lambda b, kv, g: (kv, 0, 0)),
                  pl.BlockSpec((1, CMP_BLOCK * NSA_DH, CMP_HIDDEN), lambda b, kv, g: (kv, 0, 0)),
                  pl.BlockSpec((1, CMP_HIDDEN, NSA_DH), lambda b, kv, g: (kv, 0, 0)),
                  pl.BlockSpec((1, NSA_DH), lambda b, kv, g: (0, 0))],
        out_specs=pl.BlockSpec((1, 1, 1, nch, NSA_DH), lambda b, kv, g: (b, kv, g, 0, 0)),
        out_shape=jax.ShapeDtypeStruct((b_, 2, g_, nch, NSA_DH), F32),
        scratch_shapes=[pltpu.VMEM((nch, CMP_STRIDE * NSA_DH), F32), pltpu.VMEM((s_, LANES), F32)],
        compiler_params=pltpu.CompilerParams(dimension_semantics=("parallel", "arbitrary", "arbitrary")),
        name="cmp",
    )(p3, pe2, w1, w2, knw0)


def _bias_of(d, rb_ref, h):
    val = jnp.full(d.shape, rb_ref[0, h], F32)
    for k in range(1, REL_BUCKETS):
        val = jnp.where(d >= REL_THR[k], rb_ref[k, h], val)
    return val * LOG2E


def _bias_body(rb_ref, tc_ref, ts_ref, tw_ref):
    h = pl.program_id(0)
    _, s_, ncp = tc_ref.shape

    r = lax.broadcasted_iota(jnp.int32, (CMP_STRIDE, 2 * ncp), 0)
    k = lax.broadcasted_iota(jnp.int32, (CMP_STRIDE, 2 * ncp), 1)
    d = CMP_STRIDE * (ncp - 1 - k) + r - (CMP_BLOCK - 1)
    gen = jnp.where(d >= 0, _bias_of(d, rb_ref, h), NEG)

    def row_group(a, carry):
        row0 = pl.multiple_of(a * CMP_STRIDE, CMP_STRIDE)
        tc_ref[0, pl.ds(row0, CMP_STRIDE), :] = pltpu.roll(gen, (ncp + 1 + a) % (2 * ncp), 1)[:, :ncp]
        return carry

    lax.fori_loop(0, s_ // CMP_STRIDE, row_group, 0, unroll=8)
    i = lax.broadcasted_iota(jnp.int32, (NSA_TB, NSA_TB), 0)
    j = lax.broadcasted_iota(jnp.int32, (NSA_TB, NSA_TB), 1)
    for e in range(ts_ref.shape[1]):
        d = (e - NSA_PAD) * NSA_TB + i - j
        ts_ref[0, e] = jnp.where(d >= 0, _bias_of(d, rb_ref, h), NEG)
    for e in range(tw_ref.shape[1]):
        d = (e - NSA_PAD) * NSA_TB + i - j
        tw_ref[0, e] = jnp.where((d >= 0) & (d < WINDOW), _bias_of(d, rb_ref, h), NEG)


def _sel_table_len():
    a = 0
    while a * NSA_TB - (NSA_TB - 1) < REL_THR[REL_BUCKETS - 1]:
        a += 1
    return a + 1 + NSA_PAD


def _win_table_len():
    return (WINDOW + NSA_TB - 1) // NSA_TB + 2 + NSA_PAD


def _bias_tables(rel_bias, s_):
    ncp = s_ // CMP_STRIDE
    nd = _sel_table_len()
    nw = _win_table_len()
    return pl.pallas_call(
        _bias_body,
        grid=(NSA_HEADS,),
        in_specs=[pl.BlockSpec(memory_space=pltpu.SMEM)],
        out_specs=[pl.BlockSpec((1, s_, ncp), lambda h: (h, 0, 0)),
                   pl.BlockSpec((1, nd, NSA_TB, NSA_TB), lambda h: (h, 0, 0, 0)),
                   pl.BlockSpec((1, nw, NSA_TB, NSA_TB), lambda h: (h, 0, 0, 0))],
        out_shape=[jax.ShapeDtypeStruct((NSA_HEADS, s_, ncp), F32),
                   jax.ShapeDtypeStruct((NSA_HEADS, nd, NSA_TB, NSA_TB), F32),
                   jax.ShapeDtypeStruct((NSA_HEADS, nw, NSA_TB, NSA_TB), F32)],
        compiler_params=pltpu.CompilerParams(dimension_semantics=("parallel",)),
        name="bias",
    )(rel_bias)


def _rms_rows(x, w):
    return x * lax.rsqrt(jnp.mean(x * x, axis=-1, keepdims=True) + NORM_EPS) * w


def _nsa_body(q_ref, sm_ref, kc_ref, vc_ref, ks_ref, vs_ref, kw_ref, vw_ref, tc_ref, ts_ref, tw_ref,
              ovl_ref, e_ref, qnw_ref, knw_ref, o_ref,
              ksb, vsb, kwb, vwb, qa_ref, ms_ref, accs_ref, park_ref, sa_ref, sb_ref, wa_ref, wb_ref, wc_ref):
    tq, r_, tb = NSA_TQ, NSA_REP, NSA_TB
    g = pl.program_id(1)
    qi = pl.program_id(2)
    nsel = ovl_ref.shape[0]
    topn = min(SEL_TOPN, nsel)

    @pl.when(qi == 0)
    def _():
        ones = jnp.ones(vs_ref.shape[1:], BF)
        ksb[:, :NSA_DH] = _rms_rows(ks_ref[0].astype(F32), knw_ref[1:2, :]).astype(BF)
        ksb[:, NSA_DH:] = e_ref[...]
        kwb[...] = _rms_rows(kw_ref[0].astype(F32), knw_ref[2:3, :]).astype(BF)
        vsb[:, :NSA_DH] = vs_ref[0]
        vsb[:, NSA_DH:] = ones
        vwb[:, :NSA_DH] = vw_ref[0]
        vwb[:, NSA_DH:] = ones

    qscale = NSA_DH ** -0.5 * LOG2E
    for r in range(r_):
        qa_ref[r * tq:(r + 1) * tq, :NSA_DH] = (
            _rms_rows(q_ref[r].astype(F32), qnw_ref[...]) * qscale).astype(BF)
    qs = qa_ref[:, :NSA_DH]

    def logits(kj, q, k_sc, tk, dst_ref):
        off = pl.multiple_of(kj * tk, tk)
        dst_ref[:, :tk] = _mm_nt(q, k_sc[pl.ds(off, tk), :])

    def flash_init(m_ref, acc_ref):
        m_ref[...] = jnp.full(m_ref.shape, M_INIT, F32)
        acc_ref[...] = jnp.zeros_like(acc_ref)

    def flash_step(kj, s_ref, v_sc, tab_ref, tk, m_ref, acc_ref):
        nct = tk // tb
        nrt = tq // tb
        off = pl.multiple_of(kj * tk, tk)
        e0 = qi * nrt - kj * nct + NSA_PAD
        idx = {o: jnp.clip(e0 + o, 0, tab_ref.shape[1] - 1) for o in range(-(nct - 1), nrt)}
        m_prev = m_ref[...]
        m_rows, p_rows = [], []
        for rb in range(r_ * nrt):
            r, rho = divmod(rb, nrt)
            rs = slice(rb * tb, (rb + 1) * tb)
            pieces = [s_ref[rs, c * tb:(c + 1) * tb] + tab_ref[r, idx[rho - c]] for c in range(nct)]
            mx = pieces[0]
            for c in range(1, nct):
                mx = jnp.maximum(mx, pieces[c])
            m_next = jnp.maximum(m_prev[rs], jnp.max(mx, axis=-1, keepdims=True))
            m_rows.append(m_next)
            p_rows.append(jnp.concatenate([jnp.exp2(pc_ - m_next).astype(BF) for pc_ in pieces], axis=1))
        m_next = jnp.concatenate(m_rows, axis=0)
        p = jnp.concatenate(p_rows, axis=0)
        alpha = jnp.exp2(m_prev - m_next)
        acc_ref[...] = (jnp.concatenate([alpha, alpha], axis=1) * acc_ref[...]
                        + jnp.dot(p, v_sc[pl.ds(off, tk), :], preferred_element_type=F32))
        m_ref[...] = m_next

    def flash_result(acc_ref):
        acc = acc_ref[...]
        return acc[:, :NSA_DH] / jnp.maximum(acc[:, NSA_DH:], 1e-30)

    wfirst = jnp.maximum((qi * tq + tq - 1) // NSA_TK_WIN - (NSA_WIN_TILES - 1), 0)
    wbufs = (wa_ref, wb_ref, wc_ref)

    def window_output():
        nct = NSA_TK_WIN // tb
        nrt = tq // tb
        wk = NSA_WIN_TILES * NSA_TK_WIN
        idx = {(u, o): jnp.clip(qi * nrt - (wfirst + u) * nct + NSA_PAD + o, 0, tw_ref.shape[1] - 1)
               for u in range(NSA_WIN_TILES) for o in range(-(nct - 1), nrt)}
        p_rows = []
        for rb in range(r_ * nrt):
            r, rho = divmod(rb, nrt)
            rs = slice(rb * tb, (rb + 1) * tb)
            pieces = [wbufs[u][rs, c * tb:(c + 1) * tb] + tw_ref[r, idx[(u, rho - c)]]
                      for u in range(NSA_WIN_TILES) for c in range(nct)]
            mx = pieces[0]
            for pc_ in pieces[1:]:
                mx = jnp.maximum(mx, pc_)
            m = jnp.max(mx, axis=-1, keepdims=True)
            p_rows.append(jnp.concatenate([jnp.exp2(pc_ - m).astype(BF) for pc_ in pieces], axis=1))
        p = jnp.concatenate(p_rows, axis=0)
        off = pl.multiple_of(wfirst * NSA_TK_WIN, NSA_TK_WIN)
        pv = jnp.dot(p, vwb[pl.ds(off, wk), :], preferred_element_type=F32)
        return pv[:, :NSA_DH] / jnp.maximum(pv[:, NSA_DH:], 1e-30)

    lc = _mm_nt(qs, kc_ref[0, 0, 0]) + jnp.concatenate([tc_ref[r] for r in range(r_)], axis=0)
    for u in range(NSA_WIN_TILES):
        logits(wfirst + u, qs, kwb, NSA_TK_WIN, wbufs[u])
    mc = jnp.max(lc, axis=-1, keepdims=True)
    pc = jnp.exp2(lc - mc)
    lsum = jnp.sum(pc, axis=-1, keepdims=True)
    pc = pc * jnp.where(mc > 0.5 * NEG, 1.0 / jnp.maximum(lsum, 1e-30), 0.0)

    psum = pc[0:tq]
    for r in range(1, r_):
        psum = psum + pc[r * tq:(r + 1) * tq]
    s1, s2, s3 = _split3(psum)
    dn = (((1,), (1,)), ((), ()))
    ovl = ovl_ref[...]
    imp_t = (lax.dot_general(ovl, s1, dn, preferred_element_type=F32)
             + lax.dot_general(ovl, s2, dn, preferred_element_type=F32)
             + lax.dot_general(ovl, s3, dn, preferred_element_type=F32))

    oc = _mm(pc, vc_ref[0, 0, 0])
    o_w = window_output()

    sg = _sigmoid(sm_ref[0])

    def gate(r, br):
        c0 = SM_GATE + r * 3 + br
        c1 = SM_GATE + (r_ + r) * 3 + br
        return jnp.where(g == 0, sg[:, c0:c0 + 1], sg[:, c1:c1 + 1])

    for r in range(r_):
        rs = slice(r * tq, (r + 1) * tq)
        park_ref[rs, :] = gate(r, 0) * oc[rs] + gate(r, 2) * o_w[rs]

    jb = lax.broadcasted_iota(jnp.int32, (nsel, tq), 0)
    tpos = qi * tq + lax.broadcasted_iota(jnp.int32, (nsel, tq), 1)
    tblk = tpos // SEL_BLOCK
    forced = (jb == 0) | (jb == tblk) | (jb == tblk - 1)
    score = jnp.where(jb <= tblk, imp_t + jnp.where(forced, FORCE_BONUS, 0.0), NEG)
    sl8 = SUBLANES
    sub = lax.broadcasted_iota(jnp.int32, (sl8, tq), 0)
    groups = [score[sl8 * v:sl8 * (v + 1)] for v in range(nsel // sl8)]
    counts = [jnp.zeros((sl8, tq), F32) for _ in groups]
    for jp in range(nsel):
        row = jnp.broadcast_to(score[jp:jp + 1, :], (sl8, tq))
        for v, grp in enumerate(groups):
            if sl8 * v > jp:
                beats = jnp.where(row >= grp, 1.0, 0.0)
            elif sl8 * (v + 1) <= jp:
                beats = jnp.where(row > grp, 1.0, 0.0)
            else:
                tie = jnp.where(sub + sl8 * v > jp, 1.0, 0.0)
                beats = jnp.where(row > grp, 1.0, jnp.where(row == grp, tie, 0.0))
            counts[v] = counts[v] + beats
    unsel_t = jnp.where(jnp.concatenate(counts, axis=0) < topn, 0.0, NEG)
    unsel_pad = jnp.concatenate([unsel_t, jnp.zeros((LANES - nsel, tq), F32)], axis=0).astype(BF)
    ri = lax.broadcasted_iota(jnp.int32, (tq, tq), 0)
    ci = lax.broadcasted_iota(jnp.int32, (tq, tq), 1)
    eye = jnp.where(ri == ci, 1.0, 0.0).astype(BF)
    unsel_q = lax.dot_general(eye, unsel_pad, dn, preferred_element_type=F32).astype(BF)
    for r in range(r_):
        qa_ref[r * tq:(r + 1) * tq, NSA_DH:] = unsel_q

    qa = qa_ref[...]
    slast = (qi * tq + tq - 1) // NSA_TK_SEL
    flash_init(ms_ref, accs_ref)
    logits(0, qa, ksb, NSA_TK_SEL, sa_ref)

    def sel_step(kj, s_ref):
        flash_step(kj, s_ref, vsb, ts_ref, NSA_TK_SEL, ms_ref, accs_ref)

    def pair(i, carry):
        kj = 2 * i
        logits(kj + 1, qa, ksb, NSA_TK_SEL, sb_ref)
        sel_step(kj, sa_ref)
        logits(jnp.minimum(kj + 2, slast), qa, ksb, NSA_TK_SEL, sa_ref)
        sel_step(kj + 1, sb_ref)
        return carry

    lax.fori_loop(0, (slast + 1) // 2, pair, 0)

    @pl.when(slast % 2 == 0)
    def _():
        sel_step(slast, sa_ref)

    o_s = flash_result(accs_ref)
    sg = _sigmoid(sm_ref[0])
    for r in range(r_):
        rs = slice(r * tq, (r + 1) * tq)
        o_ref[r] = (park_ref[rs, :] + gate(r, 1) * o_s[rs]).astype(o_ref.dtype)


def _nsa(p3, sm, ckv, tab_c, tab_s, tab_w, ovl_t, e_sel, qnw, knw, b_, s_):
    tq, tk, r_ = NSA_TQ, NSA_TB, NSA_REP
    nq = s_ // tq
    ncp = s_ // CMP_STRIDE
    nsel = s_ // SEL_BLOCK
    nd = tab_s.shape[1]
    nw = tab_w.shape[1]
    assert s_ % NSA_TK_SEL == 0 and NSA_TK_SEL // NSA_TB - 1 <= NSA_PAD
    assert s_ >= NSA_WIN_TILES * NSA_TK_WIN and NSA_WIN_TILES == 3
    assert NSA_TQ % NSA_TB == 0 and max(NSA_TQ, NSA_TK_WIN) % min(NSA_TQ, NSA_TK_WIN) == 0

    def kv_spec(base):
        return pl.BlockSpec((1, s_, LANES), lambda b, g, q, base=base: (base + g, b, 0))

    return pl.pallas_call(
        _nsa_body,
        grid=(b_, NSA_GROUPS, nq),
        in_specs=[pl.BlockSpec((r_, tq, LANES), lambda b, g, q: (CB_NQ // r_ + g, b * nq + q, 0)),
                  pl.BlockSpec((1, tq, LANES), lambda b, g, q: (0, b * nq + q, 0)),
                  pl.BlockSpec((1, 1, 1, ncp, NSA_DH), lambda b, g, q: (b, 0, g, 0, 0)),
                  pl.BlockSpec((1, 1, 1, ncp, NSA_DH), lambda b, g, q: (b, 1, g, 0, 0)),
                  kv_spec(CB_KS), kv_spec(CB_VS), kv_spec(CB_KW), kv_spec(CB_VW),
                  pl.BlockSpec((r_, tq, ncp), lambda b, g, q: (g, q, 0)),
                  pl.BlockSpec((r_, nd, tk, tk), lambda b, g, q: (g, 0, 0, 0)),
                  pl.BlockSpec((r_, nw, tk, tk), lambda b, g, q: (g, 0, 0, 0)),
                  pl.BlockSpec((nsel, ncp), lambda b, g, q: (0, 0)),
                  pl.BlockSpec((s_, LANES), lambda b, g, q: (0, 0)),
                  pl.BlockSpec((1, NSA_DH), lambda b, g, q: (0, 0)),
                  pl.BlockSpec((3, NSA_DH), lambda b, g, q: (0, 0))],
        out_specs=pl.BlockSpec((r_, tq, LANES), lambda b, g, q: (g, b * nq + q, 0)),
        out_shape=jax.ShapeDtypeStruct((NSA_HEADS, b_ * s_, LANES), BF),
        scratch_shapes=[pltpu.VMEM((s_, 2 * NSA_DH), BF), pltpu.VMEM((s_, 2 * NSA_DH), BF),
                        pltpu.VMEM((s_, NSA_DH), BF), pltpu.VMEM((s_, 2 * NSA_DH), BF),
                        pltpu.VMEM((r_ * tq, 2 * NSA_DH), BF),
                        pltpu.VMEM((r_ * tq, LANES), F32), pltpu.VMEM((r_ * tq, 2 * NSA_DH), F32),
                        pltpu.VMEM((r_ * tq, NSA_DH), F32),
                        pltpu.VMEM((r_ * tq, NSA_TK_SEL), F32), pltpu.VMEM((r_ * tq, NSA_TK_SEL), F32)]
        + [pltpu.VMEM((r_ * tq, NSA_TK_WIN), F32)] * NSA_WIN_TILES,
        compiler_params=pltpu.CompilerParams(dimension_semantics=("parallel", "arbitrary", "arbitrary")),
        name="nsa",
    )(p3, sm, ckv, ckv, p3, p3, p3, p3, tab_c, tab_s, tab_w, ovl_t, e_sel, qnw, knw)


def _merge_body(x_ref, ya_ref, yb_ref, ga_ref, gb_ref, wpa_ref, wpb_ref, wo_ref, o_ref):
    nh = ya_ref.shape[0]
    ya = jnp.concatenate([ya_ref[j] for j in range(nh)], axis=1)
    yb = jnp.concatenate([yb_ref[j] for j in range(nh)], axis=1)
    ga = _sigmoid(jnp.concatenate([ga_ref[j].astype(F32) for j in range(nh)], axis=1))
    gb = _sigmoid(jnp.concatenate([gb_ref[j].astype(F32) for j in range(nh)], axis=1))
    mixed = (ga * jnp.dot(ya, wpa_ref[...], preferred_element_type=F32)
             + gb * jnp.dot(yb, wpb_ref[...], preferred_element_type=F32))
    o_ref[...] = x_ref[...] + jnp.dot(mixed.astype(BF), wo_ref[...], preferred_element_type=F32)


def _merge(x2, ya, yb, p3, wpa, wpb, wo, tm):
    t, d = x2.shape
    nh = d // LANES
    hspec = pl.BlockSpec((nh, tm, LANES), lambda i: (0, i, 0))
    wspec = pl.BlockSpec((d, d), lambda i: (0, 0))
    return pl.pallas_call(
        _merge_body,
        grid=(t // tm,),
        in_specs=[pl.BlockSpec((tm, d), lambda i: (i, 0)), hspec, hspec,
                  pl.BlockSpec((nh, tm, LANES), lambda i: (CB_MGA // nh, i, 0)),
                  pl.BlockSpec((nh, tm, LANES), lambda i: (CB_MGB // nh, i, 0)),
                  wspec, wspec, wspec],
        out_specs=pl.BlockSpec((tm, d), lambda i: (i, 0)),
        out_shape=jax.ShapeDtypeStruct((t, d), F32),
        compiler_params=pltpu.CompilerParams(dimension_semantics=("parallel",)),
        name="merge",
    )(x2, ya, yb, p3, p3, wpa, wpb, wo)


def _ffn_body(x_ref, nw_ref, wg_ref, wu_ref, wd_ref, o_ref):
    x = x_ref[...]
    h = (x * lax.rsqrt(jnp.mean(x * x, axis=-1, keepdims=True) + NORM_EPS) * nw_ref[...]).astype(BF)
    gate = jnp.dot(h, wg_ref[...], preferred_element_type=F32)
    up = jnp.dot(h, wu_ref[...], preferred_element_type=F32)
    act = (_silu(gate) * up).astype(BF)
    o_ref[...] = x + jnp.dot(act, wd_ref[...], preferred_element_type=F32)


def _ffn(x2, norm_w, wg, wu, wd, tm):
    t, d = x2.shape
    f = wg.shape[1]
    return pl.pallas_call(
        _ffn_body,
        grid=(t // tm,),
        in_specs=[pl.BlockSpec((tm, d), lambda i: (i, 0)),
                  pl.BlockSpec((1, d), lambda i: (0, 0)),
                  pl.BlockSpec((d, f), lambda i: (0, 0)),
                  pl.BlockSpec((d, f), lambda i: (0, 0)),
                  pl.BlockSpec((f, d), lambda i: (0, 0))],
        out_specs=pl.BlockSpec((tm, d), lambda i: (i, 0)),
        out_shape=jax.ShapeDtypeStruct((t, d), F32),
        compiler_params=pltpu.CompilerParams(dimension_semantics=("parallel",)),
        name="ffn",
    )(x2, norm_w, wg, wu, wd)


def _arrange_body(w_ref, o_ref):
    o_ga = 4 * GDN_HEADS * GDN_DK
    o_gb = o_ga + GDN_HEADS
    o_nq = o_gb + GDN_HEADS
    o_nkv = o_nq + NSA_HEADS * NSA_DH
    o_ng = o_nkv + 6 * NSA_GROUPS * NSA_DH
    o_mg = o_ng + 3 * NSA_HEADS
    w = w_ref[...]
    pad = jnp.zeros((w.shape[0], o_ref.shape[1] - w.shape[1]), w.dtype)
    o_ref[...] = jnp.concatenate([w[:, :o_ga], w[:, o_nq:o_nkv], w[:, o_mg:], w[:, o_nkv:o_ng],
                                  w[:, o_ga:o_nq], w[:, o_ng:o_mg], pad], axis=1).astype(BF)


def _arrange_w_in(w_in):
    d, n = w_in.shape
    tr = 64
    narrow = 2 * GDN_HEADS + 3 * NSA_HEADS
    n_out = n - narrow + 2 * LANES
    assert n_out % PROJ_TN == 0 and (n_out - 2 * LANES) // LANES == CB_SMALL
    return pl.pallas_call(
        _arrange_body,
        grid=(d // tr,),
        in_specs=[pl.BlockSpec((tr, n), lambda i: (i, 0))],
        out_specs=pl.BlockSpec((tr, n_out), lambda i: (i, 0)),
        out_shape=jax.ShapeDtypeStruct((d, n_out), BF),
        compiler_params=pltpu.CompilerParams(dimension_semantics=("parallel",)),
        name="arrange",
    )(w_in)


def _overlap_t(s_):
    ncp = s_ // CMP_STRIDE
    nsel = s_ // SEL_BLOCK
    cs = np.arange(ncp) * CMP_STRIDE
    ss = np.arange(nsel) * SEL_BLOCK
    ov = (cs[None, :] < ss[:, None] + SEL_BLOCK) & (cs[None, :] + CMP_BLOCK > ss[:, None])
    ov[:, ncp - 1] = False
    return jnp.asarray(ov.astype(np.float32), BF)


def _sel_expand(s_):
    assert s_ // SEL_BLOCK <= LANES
    pos = np.arange(s_)
    e = (pos[:, None] // SEL_BLOCK == np.arange(LANES)[None, :]).astype(np.float32)
    return jnp.asarray(e, BF)


def kernel(x, norm1_w, w_in, conv_w, a_log, dt_bias, gdn_norm_w, cmp_pe, cmp_w1, cmp_w2, q_norm_w, k_norm_w,
           rel_bias, w_proj_a, w_proj_b, w_out, norm2_w, w_gate, w_up, w_down):
    b_, s_, d = x.shape
    t = b_ * s_
    x2 = x.reshape(t, d)
    tab_c, tab_s, tab_w = _bias_tables(rel_bias, s_)
    ovl_t = _overlap_t(s_)
    e_sel = _sel_expand(s_)
    for l in range(norm1_w.shape[0]):
        p3, sm = _proj(x2, norm1_w[l][None, :], _arrange_w_in(w_in[l]), tm=min(PROJ_TM, t), tn=PROJ_TN)
        conv_w3 = conv_w[l].reshape(GDN_CONV, 3 * GDN_HEADS, LANES)
        alog_b = jnp.pad(a_log[l], (0, LANES - GDN_HEADS))[None, :]
        dtb_b = jnp.pad(dt_bias[l], (0, LANES - GDN_HEADS))[None, :]
        y_a = _gdn(p3, sm, conv_w3, alog_b, dtb_b, gdn_norm_w[l][None, :], b_, s_)
        pe2 = cmp_pe[l].reshape(2, 2, CMP_STRIDE * NSA_DH)
        ckv = _cmp(p3, pe2, cmp_w1[l].astype(BF), cmp_w2[l].astype(BF), k_norm_w[l][0:1], b_, s_)
        y_b = _nsa(p3, sm, ckv, tab_c, tab_s, tab_w, ovl_t, e_sel, q_norm_w[l][None, :], k_norm_w[l], b_, s_)
        x2 = _merge(x2, y_a, y_b, p3, w_proj_a[l].astype(BF), w_proj_b[l].astype(BF), w_out[l].astype(BF),
                    tm=min(MERGE_TM, t))
        x2 = _ffn(x2, norm2_w[l][None, :], w_gate[l].astype(BF), w_up[l].astype(BF), w_down[l].astype(BF),
                  tm=min(FFN_TM, t))
    return x2.reshape(b_, s_, d)
```

```python
import math

import numpy as np
import jax
import jax.numpy as jnp
from jax import lax
from jax.experimental import pallas as pl
from jax.experimental.pallas import tpu as pltpu

F32 = jnp.float32
BF = jnp.bfloat16

LANES = 128
SUBLANES = 8
GDN_HEADS = 8
GDN_DK = 128
GDN_DV = 128
GDN_CONV = 4
GDN_CHUNK = 64
NSA_HEADS = 8
NSA_GROUPS = 2
NSA_REP = NSA_HEADS // NSA_GROUPS
NSA_DH = 128
CMP_BLOCK = 32
CMP_STRIDE = 16
CMP_HIDDEN = 256
SEL_BLOCK = 64
SEL_TOPN = 16
WINDOW = 512
FORCE_BONUS = 1000.0
REL_BUCKETS = 32
REL_MAX_DIST = 1024
NORM_EPS = 1e-6
NEG = -1e30
M_INIT = -3e38
LOG2E = 1.4426950408889634

CB_GQ, CB_GK, CB_GV, CB_GZ = 0, 8, 16, 24
CB_NQ = 32
CB_MGA, CB_MGB = 40, 48
CB_KC, CB_VC, CB_KS, CB_VS, CB_KW, CB_VW = 56, 58, 60, 62, 64, 66
CB_SMALL = 68
SM_A, SM_B, SM_GATE = 0, 8, 16

PROJ_TM, PROJ_TN = 2048, 10 * LANES
MERGE_TM = 512
FFN_TM = 512
GDN_ROWS = 256
NSA_TQ = 256
NSA_TB = 128
NSA_TK_SEL = 512
NSA_TK_WIN = 256
NSA_WIN_TILES = (WINDOW + max(NSA_TQ, NSA_TK_WIN) - 2) // NSA_TK_WIN + 1
NSA_PAD = NSA_TK_SEL // NSA_TB - 1


def _mm(a, b):
    return jnp.dot(a.astype(BF), b.astype(BF), preferred_element_type=F32)


def _mm_nt(a, b):
    return lax.dot_general(a.astype(BF), b.astype(BF), (((1,), (1,)), ((), ())),
                           preferred_element_type=F32)


def _mm_tn(a, b):
    return lax.dot_general(a.astype(BF), b.astype(BF), (((0,), (0,)), ((), ())),
                           preferred_element_type=F32)


def _split3(x):
    x1 = x.astype(BF)
    r1 = x - x1.astype(F32)
    x2 = r1.astype(BF)
    x3 = (r1 - x2.astype(F32)).astype(BF)
    return x1, x2, x3


def _sigmoid(x):
    return 0.5 * jnp.tanh(0.5 * x) + 0.5


def _silu_of_half(h):
    return h + h * jnp.tanh(h)


def _silu(x):
    return _silu_of_half(0.5 * x)


def _softplus(x):
    return jnp.maximum(x, 0.0) + jnp.log(1.0 + jnp.exp(-jnp.abs(x)))


def _rel_thresholds():
    d = np.arange(0, 4 * REL_MAX_DIST, dtype=np.int64)
    max_exact = REL_BUCKETS // 2
    d_f = np.maximum(d, 1).astype(np.float32)
    large = max_exact + (np.log(d_f / np.float32(max_exact)) / np.float32(math.log(REL_MAX_DIST / max_exact))
                         * np.float32(REL_BUCKETS - max_exact)).astype(np.int32)
    large = np.minimum(large, REL_BUCKETS - 1)
    bucket = np.where(d < max_exact, d, large)
    assert np.all(np.diff(bucket) >= 0)
    return [int(np.argmax(bucket >= k)) for k in range(REL_BUCKETS)]


REL_THR = _rel_thresholds()


def _proj_body(x_ref, nw_ref, w_ref, o_ref, sm_ref, h_ref):
    @pl.when(pl.program_id(1) == 0)
    def _():
        x = x_ref[...]
        y = x * lax.rsqrt(jnp.mean(x * x, axis=-1, keepdims=True) + NORM_EPS)
        h_ref[...] = (y * nw_ref[...]).astype(BF)

    nb = o_ref.shape[0]
    r = jnp.dot(h_ref[...], w_ref[...], preferred_element_type=F32)
    for j in range(nb):
        o_ref[j] = r[:, j * LANES:(j + 1) * LANES].astype(BF)

    @pl.when(pl.program_id(1) == CB_SMALL // nb)
    def _():
        sm_ref[0] = r[:, (CB_SMALL % nb) * LANES:(CB_SMALL % nb + 1) * LANES]


def _proj(x2, norm_w, w_all, tm, tn):
    t, d = x2.shape
    n = w_all.shape[1]
    nb = tn // LANES
    return pl.pallas_call(
        _proj_body,
        grid=(t // tm, n // tn),
        in_specs=[pl.BlockSpec((tm, d), lambda i, j: (i, 0)),
                  pl.BlockSpec((1, d), lambda i, j: (0, 0)),
                  pl.BlockSpec((d, tn), lambda i, j: (0, j))],
        out_specs=[pl.BlockSpec((nb, tm, LANES), lambda i, j: (j, i, 0)),
                   pl.BlockSpec((1, tm, LANES), lambda i, j: (0, i, 0))],
        out_shape=[jax.ShapeDtypeStruct((n // LANES, t, LANES), BF),
                   jax.ShapeDtypeStruct((1, t, LANES), F32)],
        scratch_shapes=[pltpu.VMEM((tm, d), BF)],
        compiler_params=pltpu.CompilerParams(dimension_semantics=("parallel", "arbitrary")),
        name="proj",
    )(x2, norm_w, w_all)


def _gdn_body(q_ref, k_ref, v_ref, z_ref, sm_ref, cw_ref, alog_ref, dtb_ref, nw_ref, o_ref,
              ext_ref, st_ref):
    rows = GDN_ROWS
    c = GDN_CHUNK
    nchunk = rows // c
    halo = SUBLANES
    s = pl.program_id(1)

    @pl.when(s == 0)
    def _():
        ext_ref[:, 0:halo, :] = jnp.zeros((3 * GDN_HEADS, halo, LANES), F32)
        st_ref[...] = jnp.zeros_like(st_ref)

    for j in range(GDN_HEADS):
        ext_ref[j, halo:halo + rows, :] = q_ref[j].astype(F32)
        ext_ref[GDN_HEADS + j, halo:halo + rows, :] = k_ref[j].astype(F32)
        ext_ref[2 * GDN_HEADS + j, halo:halo + rows, :] = v_ref[j].astype(F32)

    ri = lax.broadcasted_iota(jnp.int32, (rows, rows), 0)
    ci = lax.broadcasted_iota(jnp.int32, (rows, rows), 1)
    l_tril = jnp.where(((ri // c) == (ci // c)) & (ri >= ci), 1.0, 0.0).astype(BF)
    rt = lax.broadcasted_iota(jnp.int32, (LANES, LANES), 0)
    ct = lax.broadcasted_iota(jnp.int32, (LANES, LANES), 1)
    same_t = (rt // c) == (ct // c)
    tril_t = same_t & (rt >= ct)
    strict_t = same_t & (rt > ct)
    eye = jnp.where(rt == ct, 1.0, 0.0)

    sm = sm_ref[0]
    lane = lax.broadcasted_iota(jnp.int32, (rows, LANES), 1)
    gall = jnp.where(lane < GDN_HEADS, -jnp.exp(alog_ref[...]) * _softplus(sm + dtb_ref[...]), 0.0)
    g1 = gall.astype(BF).astype(F32)
    r1 = gall - g1
    g2 = r1.astype(BF).astype(F32)
    packed = g1 + pltpu.roll(g2, GDN_HEADS, 1) + pltpu.roll(r1 - g2, 2 * GDN_HEADS, 1)
    gc = jnp.dot(l_tril, packed.astype(BF), preferred_element_type=F32)
    gcum_all = gc + pltpu.roll(gc, LANES - GDN_HEADS, 1) + pltpu.roll(gc, LANES - 2 * GDN_HEADS, 1)

    def conv_silu(j):
        first = halo - (GDN_CONV - 1)
        acc = (0.5 * cw_ref[0, pl.ds(j, 1), :]) * ext_ref[j, pl.ds(first, rows), :]
        for i in range(1, GDN_CONV):
            acc = acc + (0.5 * cw_ref[i, pl.ds(j, 1), :]) * ext_ref[j, pl.ds(first + i, rows), :]
        return _silu_of_half(acc)

    def head_setup(h):
        qh = conv_silu(h)
        kh = conv_silu(GDN_HEADS + h)
        vv = conv_silu(2 * GDN_HEADS + h)
        qn = qh * lax.rsqrt(jnp.sum(qh * qh, axis=-1, keepdims=True) + NORM_EPS) * (GDN_DK ** -0.5)
        kn = kh * lax.rsqrt(jnp.sum(kh * kh, axis=-1, keepdims=True) + NORM_EPS)

        beta = _sigmoid(jnp.broadcast_to(sm[:, SM_B + h:SM_B + h + 1], (rows, LANES)))
        gcum = jnp.broadcast_to(gcum_all[:, h:h + 1], (rows, LANES))
        glast = jnp.concatenate(
            [jnp.broadcast_to(gcum[(n + 1) * c - 1:(n + 1) * c, :], (c, LANES)) for n in range(nchunk)],
            axis=0)
        gct = gcum.T
        kb = kn * beta
        eg = jnp.exp(gcum)
        knb = kn.astype(BF)
        a_t, intra_t = [], []
        for u in range(rows // LANES):
            rs = slice(u * LANES, (u + 1) * LANES)
            diff = gcum[rs] - gct[:, rs]
            decay = jnp.where(tril_t, jnp.exp(jnp.where(tril_t, diff, 0.0)), 0.0)
            a_t.append(jnp.where(strict_t, _mm_nt(kb[rs], knb[rs]) * decay, 0.0))
            intra_t.append(_mm_nt(qn[rs], knb[rs]) * decay)
        return dict(
            a=a_t, intra=intra_t,
            rhs=jnp.concatenate([vv * beta, kb * eg], axis=1),
            qg=qn * eg, kdec=kn * jnp.exp(glast - gcum), cd=jnp.exp(glast))

    def all_heads():
        hs = list(range(GDN_HEADS))
        nt = rows // LANES
        w = [head_setup(h) for h in hs]
        a_all = [a_ for d in w for a_ in d["a"]]
        p = [_mm(a_, a_) for a_ in a_all]
        tg = [eye - a_ for a_ in a_all]
        for j in range(1, 6):
            tp = [_mm(ti, pi) for ti, pi in zip(tg, p)]
            if j < 5:
                p = [_mm(pi, pi) for pi in p]
            tg = [ti + tpi for ti, tpi in zip(tg, tp)]
        t = [tg[i * nt:(i + 1) * nt] for i in range(len(w))]
        sol = [[d["rhs"][u * LANES:(u + 1) * LANES] + _mm(tu - eye, d["rhs"][u * LANES:(u + 1) * LANES])
                for u, tu in enumerate(ti)] for d, ti in zip(w, t)]
        st = [st_ref[h] for h in hs]
        outs = [[] for _ in hs]
        cpt = LANES // c
        for n in range(nchunk):
            sl = slice(n * c, (n + 1) * c)
            lo = slice((n % cpt) * c, (n % cpt + 1) * c)
            ks = [_mm(jnp.concatenate([s_[n // cpt][lo, GDN_DV:], d["qg"][sl]], axis=0), si)
                  for s_, d, si in zip(sol, w, st)]
            vn = [s_[n // cpt][lo, :GDN_DV] - k_[:c] for s_, k_ in zip(sol, ks)]
            for u, (d, k_, v_) in enumerate(zip(w, ks, vn)):
                outs[u].append(k_[c:] + _mm(d["intra"][n // cpt][lo, lo], v_))
            st = [si * jnp.concatenate([d["cd"][sl], d["cd"][sl]], axis=0) + _mm_tn(d["kdec"][sl], v_)
                  for si, d, v_ in zip(st, w, vn)]
        for u, h in enumerate(hs):
            st_ref[h] = st[u]
            o = jnp.concatenate(outs[u], axis=0)
            on = o * lax.rsqrt(jnp.mean(o * o, axis=-1, keepdims=True) + NORM_EPS) * nw_ref[...]
            o_ref[h] = (on * _silu(z_ref[h].astype(F32))).astype(o_ref.dtype)

    all_heads()

    for j in range(3 * GDN_HEADS):
        ext_ref[j, 0:halo, :] = ext_ref[j, rows:rows + halo, :]


def _gdn(p3, sm, conv_w3, alog_b, dtb_b, gdn_norm_w, b_, s_):
    rows = GDN_ROWS
    ns = s_ // rows
    hb = GDN_HEADS

    def cb(base):
        return pl.BlockSpec((hb, rows, LANES), lambda b, s, base=base: (base // hb, b * ns + s, 0))

    return pl.pallas_call(
        _gdn_body,
        grid=(b_, ns),
        in_specs=[cb(CB_GQ), cb(CB_GK), cb(CB_GV), cb(CB_GZ),
                  pl.BlockSpec((1, rows, LANES), lambda b, s: (0, b * ns + s, 0)),
                  pl.BlockSpec((GDN_CONV, 3 * hb, LANES), lambda b, s: (0, 0, 0)),
                  pl.BlockSpec((1, LANES), lambda b, s: (0, 0)),
                  pl.BlockSpec((1, LANES), lambda b, s: (0, 0)),
                  pl.BlockSpec((1, LANES), lambda b, s: (0, 0))],
        out_specs=pl.BlockSpec((hb, rows, LANES), lambda b, s: (0, b * ns + s, 0)),
        out_shape=jax.ShapeDtypeStruct((hb, b_ * s_, LANES), BF),
        scratch_shapes=[pltpu.VMEM((3 * hb, rows + SUBLANES, LANES), F32),
                        pltpu.VMEM((hb, GDN_DK, GDN_DV), F32)],
        compiler_params=pltpu.CompilerParams(dimension_semantics=("parallel", "arbitrary")),
        name="gdn",
    )(p3, p3, p3, p3, sm, conv_w3, alog_b, dtb_b, gdn_norm_w)


def _cmp_body(x_ref, pe_ref, w1_ref, w2_ref, nw_ref, o_ref, c_ref, xf_ref):
    kv = pl.program_id(1)
    nch = c_ref.shape[0]
    half = CMP_STRIDE * NSA_DH
    xf_ref[...] = x_ref[0].astype(F32)
    for p in range(CMP_STRIDE):
        c_ref[:, p * NSA_DH:(p + 1) * NSA_DH] = xf_ref[pl.ds(p, nch, stride=CMP_STRIDE), :]
    cc = c_ref[...]
    u = _mm(cc + pe_ref[0, 0:1, :], w1_ref[0, 0:half, :])
    v = _mm(cc + pe_ref[0, 1:2, :], w1_ref[0, half:2 * half, :])
    v_next = jnp.concatenate([v[1:], v[:1]], axis=0)
    hid = _silu(u + v_next)
    out = _mm(hid, w2_ref[0])
    normed = out * lax.rsqrt(jnp.mean(out * out, axis=-1, keepdims=True) + NORM_EPS) * nw_ref[...]
    o_ref[0, 0, 0] = jnp.where(kv == 0, normed, out)


def _cmp(p3, pe2, w1, w2, knw0, b_, s_):
    nch = s_ // CMP_STRIDE
    g_ = NSA_GROUPS
    return pl.pallas_call(
        _cmp_body,
        grid=(b_, 2, g_),
        in_specs=[pl.BlockSpec((1, s_, LANES), lambda b, kv, g: (CB_KC + 2 * kv + g, b, 0)),
                  pl.BlockSpec((1, 2, CMP_STRIDE * NSA_DH), lambda b, kv, g: (kv, 0, 0)),
                  pl.BlockSpec((1, CMP_BLOCK * NSA_DH, CMP_HIDDEN), lambda b, kv, g: (kv, 0, 0)),
                  pl.BlockSpec((1, CMP_HIDDEN, NSA_DH), lambda b, kv, g: (kv, 0, 0)),
                  pl.BlockSpec((1, NSA_DH), lambda b, kv, g: (0, 0))],
        out_specs=pl.BlockSpec((1, 1, 1, nch, NSA_DH), lambda b, kv, g: (b, kv, g, 0, 0)),
        out_shape=jax.ShapeDtypeStruct((b_, 2, g_, nch, NSA_DH), F32),
        scratch_shapes=[pltpu.VMEM((nch, CMP_STRIDE * NSA_DH), F32), pltpu.VMEM((s_, LANES), F32)],
        compiler_params=pltpu.CompilerParams(dimension_semantics=("parallel", "arbitrary", "arbitrary")),
        name="cmp",
    )(p3, pe2, w1, w2, knw0)


def _bias_of(d, rb_ref, h):
    val = jnp.full(d.shape, rb_ref[0, h], F32)
    for k in range(1, REL_BUCKETS):
        val = jnp.where(d >= REL_THR[k], rb_ref[k, h], val)
    return val * LOG2E


def _bias_body(rb_ref, tc_ref, ts_ref, tw_ref):
    h = pl.program_id(0)
    _, s_, ncp = tc_ref.shape

    r = lax.broadcasted_iota(jnp.int32, (CMP_STRIDE, 2 * ncp), 0)
    k = lax.broadcasted_iota(jnp.int32, (CMP_STRIDE, 2 * ncp), 1)
    d = CMP_STRIDE * (ncp - 1 - k) + r - (CMP_BLOCK - 1)
    gen = jnp.where(d >= 0, _bias_of(d, rb_ref, h), NEG)

    def row_group(a, carry):
        row0 = pl.multiple_of(a * CMP_STRIDE, CMP_STRIDE)
        tc_ref[0, pl.ds(row0, CMP_STRIDE), :] = pltpu.roll(gen, (ncp + 1 + a) % (2 * ncp), 1)[:, :ncp]
        return carry

    lax.fori_loop(0, s_ // CMP_STRIDE, row_group, 0, unroll=8)
    i = lax.broadcasted_iota(jnp.int32, (NSA_TB, NSA_TB), 0)
    j = lax.broadcasted_iota(jnp.int32, (NSA_TB, NSA_TB), 1)
    for e in range(ts_ref.shape[1]):
        d = (e - NSA_PAD) * NSA_TB + i - j
        ts_ref[0, e] = jnp.where(d >= 0, _bias_of(d, rb_ref, h), NEG)
    for e in range(tw_ref.shape[1]):
        d = (e - NSA_PAD) * NSA_TB + i - j
        tw_ref[0, e] = jnp.where((d >= 0) & (d < WINDOW), _bias_of(d, rb_ref, h), NEG)


def _sel_table_len():
    a = 0
    while a * NSA_TB - (NSA_TB - 1) < REL_THR[REL_BUCKETS - 1]:
        a += 1
    return a + 1 + NSA_PAD


def _win_table_len():
    return (WINDOW + NSA_TB - 1) // NSA_TB + 2 + NSA_PAD


def _bias_tables(rel_bias, s_):
    ncp = s_ // CMP_STRIDE
    nd = _sel_table_len()
    nw = _win_table_len()
    return pl.pallas_call(
        _bias_body,
        grid=(NSA_HEADS,),
        in_specs=[pl.BlockSpec(memory_space=pltpu.SMEM)],
        out_specs=[pl.BlockSpec((1, s_, ncp), lambda h: (h, 0, 0)),
                   pl.BlockSpec((1, nd, NSA_TB, NSA_TB), lambda h: (h, 0, 0, 0)),
                   pl.BlockSpec((1, nw, NSA_TB, NSA_TB), lambda h: (h, 0, 0, 0))],
        out_shape=[jax.ShapeDtypeStruct((NSA_HEADS, s_, ncp), F32),
                   jax.ShapeDtypeStruct((NSA_HEADS, nd, NSA_TB, NSA_TB), F32),
                   jax.ShapeDtypeStruct((NSA_HEADS, nw, NSA_TB, NSA_TB), F32)],
        compiler_params=pltpu.CompilerParams(dimension_semantics=("parallel",)),
        name="bias",
    )(rel_bias)


def _rms_rows(x, w):
    return x * lax.rsqrt(jnp.mean(x * x, axis=-1, keepdims=True) + NORM_EPS) * w


def _nsa_body(q_ref, sm_ref, kc_ref, vc_ref, ks_ref, vs_ref, kw_ref, vw_ref, tc_ref, ts_ref, tw_ref,
              ovl_ref, e_ref, qnw_ref, knw_ref, o_ref,
              ksb, vsb, kwb, vwb, qa_ref, ms_ref, accs_ref, park_ref, sa_ref, sb_ref, wa_ref, wb_ref, wc_ref):
    tq, r_, tb = NSA_TQ, NSA_REP, NSA_TB
    g = pl.program_id(1)
    qi = pl.program_id(2)
    nsel = ovl_ref.shape[0]
    topn = min(SEL_TOPN, nsel)

    @pl.when(qi == 0)
    def _():
        ones = jnp.ones(vs_ref.shape[1:], BF)
        ksb[:, :NSA_DH] = _rms_rows(ks_ref[0].astype(F32), knw_ref[1:2, :]).astype(BF)
        ksb[:, NSA_DH:] = e_ref[...]
        kwb[...] = _rms_rows(kw_ref[0].astype(F32), knw_ref[2:3, :]).astype(BF)
        vsb[:, :NSA_DH] = vs_ref[0]
        vsb[:, NSA_DH:] = ones
        vwb[:, :NSA_DH] = vw_ref[0]
        vwb[:, NSA_DH:] = ones

    qscale = NSA_DH ** -0.5 * LOG2E
    for r in range(r_):
        qa_ref[r * tq:(r + 1) * tq, :NSA_DH] = (
            _rms_rows(q_ref[r].astype(F32), qnw_ref[...]) * qscale).astype(BF)
    qs = qa_ref[:, :NSA_DH]

    def logits(kj, q, k_sc, tk, dst_ref):
        off = pl.multiple_of(kj * tk, tk)
        dst_ref[:, :tk] = _mm_nt(q, k_sc[pl.ds(off, tk), :])

    def flash_init(m_ref, acc_ref):
        m_ref[...] = jnp.full(m_ref.shape, M_INIT, F32)
        acc_ref[...] = jnp.zeros_like(acc_ref)

    def flash_step(kj, s_ref, v_sc, tab_ref, tk, m_ref, acc_ref):
        nct = tk // tb
        nrt = tq // tb
        off = pl.multiple_of(kj * tk, tk)
        e0 = qi * nrt - kj * nct + NSA_PAD
        idx = {o: jnp.clip(e0 + o, 0, tab_ref.shape[1] - 1) for o in range(-(nct - 1), nrt)}
        m_prev = m_ref[...]
        m_rows, p_rows = [], []
        for rb in range(r_ * nrt):
            r, rho = divmod(rb, nrt)
            rs = slice(rb * tb, (rb + 1) * tb)
            pieces = [s_ref[rs, c * tb:(c + 1) * tb] + tab_ref[r, idx[rho - c]] for c in range(nct)]
            mx = pieces[0]
            for c in range(1, nct):
                mx = jnp.maximum(mx, pieces[c])
            m_next = jnp.maximum(m_prev[rs], jnp.max(mx, axis=-1, keepdims=True))
            m_rows.append(m_next)
            p_rows.append(jnp.concatenate([jnp.exp2(pc_ - m_next).astype(BF) for pc_ in pieces], axis=1))
        m_next = jnp.concatenate(m_rows, axis=0)
        p = jnp.concatenate(p_rows, axis=0)
        alpha = jnp.exp2(m_prev - m_next)
        acc_ref[...] = (jnp.concatenate([alpha, alpha], axis=1) * acc_ref[...]
                        + jnp.dot(p, v_sc[pl.ds(off, tk), :], preferred_element_type=F32))
        m_ref[...] = m_next

    def flash_result(acc_ref):
        acc = acc_ref[...]
        return acc[:, :NSA_DH] / jnp.maximum(acc[:, NSA_DH:], 1e-30)

    wfirst = jnp.maximum((qi * tq + tq - 1) // NSA_TK_WIN - (NSA_WIN_TILES - 1), 0)
    wbufs = (wa_ref, wb_ref, wc_ref)

    def window_output():
        nct = NSA_TK_WIN // tb
        nrt = tq // tb
        wk = NSA_WIN_TILES * NSA_TK_WIN
        idx = {(u, o): jnp.clip(qi * nrt - (wfirst + u) * nct + NSA_PAD + o, 0, tw_ref.shape[1] - 1)
               for u in range(NSA_WIN_TILES) for o in range(-(nct - 1), nrt)}
        p_rows = []
        for rb in range(r_ * nrt):
            r, rho = divmod(rb, nrt)
            rs = slice(rb * tb, (rb + 1) * tb)
            pieces = [wbufs[u][rs, c * tb:(c + 1) * tb] + tw_ref[r, idx[(u, rho - c)]]
                      for u in range(NSA_WIN_TILES) for c in range(nct)]
            mx = pieces[0]
            for pc_ in pieces[1:]:
                mx = jnp.maximum(mx, pc_)
            m = jnp.max(mx, axis=-1, keepdims=True)
            p_rows.append(jnp.concatenate([jnp.exp2(pc_ - m).astype(BF) for pc_ in pieces], axis=1))
        p = jnp.concatenate(p_rows, axis=0)
        off = pl.multiple_of(wfirst * NSA_TK_WIN, NSA_TK_WIN)
        pv = jnp.dot(p, vwb[pl.ds(off, wk), :], preferred_element_type=F32)
        return pv[:, :NSA_DH] / jnp.maximum(pv[:, NSA_DH:], 1e-30)

    lc = _mm_nt(qs, kc_ref[0, 0, 0]) + jnp.concatenate([tc_ref[r] for r in range(r_)], axis=0)
    for u in range(NSA_WIN_TILES):
        logits(wfirst + u, qs, kwb, NSA_TK_WIN, wbufs[u])
    mc = jnp.max(lc, axis=-1, keepdims=True)
    pc = jnp.exp2(lc - mc)
    lsum = jnp.sum(pc, axis=-1, keepdims=True)
    pc = pc * jnp.where(mc > 0.5 * NEG, 1.0 / jnp.maximum(lsum, 1e-30), 0.0)

    psum = pc[0:tq]
    for r in range(1, r_):
        psum = psum + pc[r * tq:(r + 1) * tq]
    s1, s2, s3 = _split3(psum)
    dn = (((1,), (1,)), ((), ()))
    ovl = ovl_ref[...]
    imp_t = (lax.dot_general(ovl, s1, dn, preferred_element_type=F32)
             + lax.dot_general(ovl, s2, dn, preferred_element_type=F32)
             + lax.dot_general(ovl, s3, dn, preferred_element_type=F32))

    oc = _mm(pc, vc_ref[0, 0, 0])
    o_w = window_output()

    sg = _sigmoid(sm_ref[0])

    def gate(r, br):
        c0 = SM_GATE + r * 3 + br
        c1 = SM_GATE + (r_ + r) * 3 + br
        return jnp.where(g == 0, sg[:, c0:c0 + 1], sg[:, c1:c1 + 1])

    for r in range(r_):
        rs = slice(r * tq, (r + 1) * tq)
        park_ref[rs, :] = gate(r, 0) * oc[rs] + gate(r, 2) * o_w[rs]

    jb = lax.broadcasted_iota(jnp.int32, (nsel, tq), 0)
    tpos = qi * tq + lax.broadcasted_iota(jnp.int32, (nsel, tq), 1)
    tblk = tpos // SEL_BLOCK
    forced = (jb == 0) | (jb == tblk) | (jb == tblk - 1)
    score = jnp.where(jb <= tblk, imp_t + jnp.where(forced, FORCE_BONUS, 0.0), NEG)
    sl8 = SUBLANES
    sub = lax.broadcasted_iota(jnp.int32, (sl8, tq), 0)
    groups = [score[sl8 * v:sl8 * (v + 1)] for v in range(nsel // sl8)]
    counts = [jnp.zeros((sl8, tq), F32) for _ in groups]
    for jp in range(nsel):
        row = jnp.broadcast_to(score[jp:jp + 1, :], (sl8, tq))
        for v, grp in enumerate(groups):
            if sl8 * v > jp:
                beats = jnp.where(row >= grp, 1.0, 0.0)
            elif sl8 * (v + 1) <= jp:
                beats = jnp.where(row > grp, 1.0, 0.0)
            else:
                tie = jnp.where(sub + sl8 * v > jp, 1.0, 0.0)
                beats = jnp.where(row > grp, 1.0, jnp.where(row == grp, tie, 0.0))
            counts[v] = counts[v] + beats
    unsel_t = jnp.where(jnp.concatenate(counts, axis=0) < topn, 0.0, NEG)
    unsel_pad = jnp.concatenate([unsel_t, jnp.zeros((LANES - nsel, tq), F32)], axis=0).astype(BF)
    ri = lax.broadcasted_iota(jnp.int32, (tq, tq), 0)
    ci = lax.broadcasted_iota(jnp.int32, (tq, tq), 1)
    eye = jnp.where(ri == ci, 1.0, 0.0).astype(BF)
    unsel_q = lax.dot_general(eye, unsel_pad, dn, preferred_element_type=F32).astype(BF)
    for r in range(r_):
        qa_ref[r * tq:(r + 1) * tq, NSA_DH:] = unsel_q

    qa = qa_ref[...]
    slast = (qi * tq + tq - 1) // NSA_TK_SEL
    flash_init(ms_ref, accs_ref)
    logits(0, qa, ksb, NSA_TK_SEL, sa_ref)

    def sel_step(kj, s_ref):
        flash_step(kj, s_ref, vsb, ts_ref, NSA_TK_SEL, ms_ref, accs_ref)

    def pair(i, carry):
        kj = 2 * i
        logits(kj + 1, qa, ksb, NSA_TK_SEL, sb_ref)
        sel_step(kj, sa_ref)
        logits(jnp.minimum(kj + 2, slast), qa, ksb, NSA_TK_SEL, sa_ref)
        sel_step(kj + 1, sb_ref)
        return carry

    lax.fori_loop(0, (slast + 1) // 2, pair, 0)

    @pl.when(slast % 2 == 0)
    def _():
        sel_step(slast, sa_ref)

    o_s = flash_result(accs_ref)
    sg = _sigmoid(sm_ref[0])
    for r in range(r_):
        rs = slice(r * tq, (r + 1) * tq)
        o_ref[r] = (park_ref[rs, :] + gate(r, 1) * o_s[rs]).astype(o_ref.dtype)


def _nsa(p3, sm, ckv, tab_c, tab_s, tab_w, ovl_t, e_sel, qnw, knw, b_, s_):
    tq, tk, r_ = NSA_TQ, NSA_TB, NSA_REP
    nq = s_ // tq
    ncp = s_ // CMP_STRIDE
    nsel = s_ // SEL_BLOCK
    nd = tab_s.shape[1]
    nw = tab_w.shape[1]
    assert s_ % NSA_TK_SEL == 0 and NSA_TK_SEL // NSA_TB - 1 <= NSA_PAD
    assert s_ >= NSA_WIN_TILES * NSA_TK_WIN and NSA_WIN_TILES == 3
    assert NSA_TQ % NSA_TB == 0 and max(NSA_TQ, NSA_TK_WIN) % min(NSA_TQ, NSA_TK_WIN) == 0

    def kv_spec(base):
        return pl.BlockSpec((1, s_, LANES), lambda b, g, q, base=base: (base + g, b, 0))

    return pl.pallas_call(
        _nsa_body,
        grid=(b_, NSA_GROUPS, nq),
        in_specs=[pl.BlockSpec((r_, tq, LANES), lambda b, g, q: (CB_NQ // r_ + g, b * nq + q, 0)),
                  pl.BlockSpec((1, tq, LANES), lambda b, g, q: (0, b * nq + q, 0)),
                  pl.BlockSpec((1, 1, 1, ncp, NSA_DH), lambda b, g, q: (b, 0, g, 0, 0)),
                  pl.BlockSpec((1, 1, 1, ncp, NSA_DH), lambda b, g, q: (b, 1, g, 0, 0)),
                  kv_spec(CB_KS), kv_spec(CB_VS), kv_spec(CB_KW), kv_spec(CB_VW),
                  pl.BlockSpec((r_, tq, ncp), lambda b, g, q: (g, q, 0)),
                  pl.BlockSpec((r_, nd, tk, tk), lambda b, g, q: (g, 0, 0, 0)),
                  pl.BlockSpec((r_, nw, tk, tk), lambda b, g, q: (g, 0, 0, 0)),
                  pl.BlockSpec((nsel, ncp), lambda b, g, q: (0, 0)),
                  pl.BlockSpec((s_, LANES), lambda b, g, q: (0, 0)),
                  pl.BlockSpec((1, NSA_DH), lambda b, g, q: (0, 0)),
                  pl.BlockSpec((3, NSA_DH), lambda b, g, q: (0, 0))],
        out_specs=pl.BlockSpec((r_, tq, LANES), lambda b, g, q: (g, b * nq + q, 0)),
        out_shape=jax.ShapeDtypeStruct((NSA_HEADS, b_ * s_, LANES), BF),
        scratch_shapes=[pltpu.VMEM((s_, 2 * NSA_DH), BF), pltpu.VMEM((s_, 2 * NSA_DH), BF),
                        pltpu.VMEM((s_, NSA_DH), BF), pltpu.VMEM((s_, 2 * NSA_DH), BF),
                        pltpu.VMEM((r_ * tq, 2 * NSA_DH), BF),
                        pltpu.VMEM((r_ * tq, LANES), F32), pltpu.VMEM((r_ * tq, 2 * NSA_DH), F32),
                        pltpu.VMEM((r_ * tq, NSA_DH), F32),
                        pltpu.VMEM((r_ * tq, NSA_TK_SEL), F32), pltpu.VMEM((r_ * tq, NSA_TK_SEL), F32)]
        + [pltpu.VMEM((r_ * tq, NSA_TK_WIN), F32)] * NSA_WIN_TILES,
        compiler_params=pltpu.CompilerParams(dimension_semantics=("parallel", "arbitrary", "arbitrary")),
        name="nsa",
    )(p3, sm, ckv, ckv, p3, p3, p3, p3, tab_c, tab_s, tab_w, ovl_t, e_sel, qnw, knw)


def _merge_body(x_ref, ya_ref, yb_ref, ga_ref, gb_ref, wpa_ref, wpb_ref, wo_ref, o_ref):
    nh = ya_ref.shape[0]
    ya = jnp.concatenate([ya_ref[j] for j in range(nh)], axis=1)
    yb = jnp.concatenate([yb_ref[j] for j in range(nh)], axis=1)
    ga = _sigmoid(jnp.concatenate([ga_ref[j].astype(F32) for j in range(nh)], axis=1))
    gb = _sigmoid(jnp.concatenate([gb_ref[j].astype(F32) for j in range(nh)], axis=1))
    mixed = (ga * jnp.dot(ya, wpa_ref[...], preferred_element_type=F32)
             + gb * jnp.dot(yb, wpb_ref[...], preferred_element_type=F32))
    o_ref[...] = x_ref[...] + jnp.dot(mixed.astype(BF), wo_ref[...], preferred_element_type=F32)


def _merge(x2, ya, yb, p3, wpa, wpb, wo, tm):
    t, d = x2.shape
    nh = d // LANES
    hspec = pl.BlockSpec((nh, tm, LANES), lambda i: (0, i, 0))
    wspec = pl.BlockSpec((d, d), lambda i: (0, 0))
    return pl.pallas_call(
        _merge_body,
        grid=(t // tm,),
        in_specs=[pl.BlockSpec((tm, d), lambda i: (i, 0)), hspec, hspec,
                  pl.BlockSpec((nh, tm, LANES), lambda i: (CB_MGA // nh, i, 0)),
                  pl.BlockSpec((nh, tm, LANES), lambda i: (CB_MGB // nh, i, 0)),
                  wspec, wspec, wspec],
        out_specs=pl.BlockSpec((tm, d), lambda i: (i, 0)),
        out_shape=jax.ShapeDtypeStruct((t, d), F32),
        compiler_params=pltpu.CompilerParams(dimension_semantics=("parallel",)),
        name="merge",
    )(x2, ya, yb, p3, p3, wpa, wpb, wo)


def _ffn_body(x_ref, nw_ref, wg_ref, wu_ref, wd_ref, o_ref):
    x = x_ref[...]
    h = (x * lax.rsqrt(jnp.mean(x * x, axis=-1, keepdims=True) + NORM_EPS) * nw_ref[...]).astype(BF)
    gate = jnp.dot(h, wg_ref[...], preferred_element_type=F32)
    up = jnp.dot(h, wu_ref[...], preferred_element_type=F32)
    act = (_silu(gate) * up).astype(BF)
    o_ref[...] = x + jnp.dot(act, wd_ref[...], preferred_element_type=F32)


def _ffn(x2, norm_w, wg, wu, wd, tm):
    t, d = x2.shape
    f = wg.shape[1]
    return pl.pallas_call(
        _ffn_body,
        grid=(t // tm,),
        in_specs=[pl.BlockSpec((tm, d), lambda i: (i, 0)),
                  pl.BlockSpec((1, d), lambda i: (0, 0)),
                  pl.BlockSpec((d, f), lambda i: (0, 0)),
                  pl.BlockSpec((d, f), lambda i: (0, 0)),
                  pl.BlockSpec((f, d), lambda i: (0, 0))],
        out_specs=pl.BlockSpec((tm, d), lambda i: (i, 0)),
        out_shape=jax.ShapeDtypeStruct((t, d), F32),
        compiler_params=pltpu.CompilerParams(dimension_semantics=("parallel",)),
        name="ffn",
    )(x2, norm_w, wg, wu, wd)


def _arrange_body(w_ref, o_ref):
    o_ga = 4 * GDN_HEADS * GDN_DK
    o_gb = o_ga + GDN_HEADS
    o_nq = o_gb + GDN_HEADS
    o_nkv = o_nq + NSA_HEADS * NSA_DH
    o_ng = o_nkv + 6 * NSA_GROUPS * NSA_DH
    o_mg = o_ng + 3 * NSA_HEADS
    w = w_ref[0]
    pad = jnp.zeros((w.shape[0], o_ref.shape[1] - w.shape[1]), w.dtype)
    o_ref[...] = jnp.concatenate([w[:, :o_ga], w[:, o_nq:o_nkv], w[:, o_mg:], w[:, o_nkv:o_ng],
                                  w[:, o_ga:o_nq], w[:, o_ng:o_mg], pad], axis=1).astype(BF)


def _arrange_w_in(w_in, layer):
    _, d, n = w_in.shape
    tr = 64
    narrow = 2 * GDN_HEADS + 3 * NSA_HEADS
    n_out = n - narrow + 2 * LANES
    assert n_out % PROJ_TN == 0 and (n_out - 2 * LANES) // LANES == CB_SMALL
    return pl.pallas_call(
        _arrange_body,
        grid=(d // tr,),
        in_specs=[pl.BlockSpec((1, tr, n), lambda i: (layer, i, 0))],
        out_specs=pl.BlockSpec((tr, n_out), lambda i: (i, 0)),
        out_shape=jax.ShapeDtypeStruct((d, n_out), BF),
        compiler_params=pltpu.CompilerParams(dimension_semantics=("parallel",)),
        name="arrange",
    )(w_in)


def _overlap_t(s_):
    ncp = s_ // CMP_STRIDE
    nsel = s_ // SEL_BLOCK
    cs = np.arange(ncp) * CMP_STRIDE
    ss = np.arange(nsel) * SEL_BLOCK
    ov = (cs[None, :] < ss[:, None] + SEL_BLOCK) & (cs[None, :] + CMP_BLOCK > ss[:, None])
    ov[:, ncp - 1] = False
    return jnp.asarray(ov.astype(np.float32), BF)


def _sel_expand(s_):
    assert s_ // SEL_BLOCK <= LANES
    pos = np.arange(s_)
    e = (pos[:, None] // SEL_BLOCK == np.arange(LANES)[None, :]).astype(np.float32)
    return jnp.asarray(e, BF)


def kernel(x, norm1_w, w_in, conv_w, a_log, dt_bias, gdn_norm_w, cmp_pe, cmp_w1, cmp_w2, q_norm_w, k_norm_w,
           rel_bias, w_proj_a, w_proj_b, w_out, norm2_w, w_gate, w_up, w_down):
    b_, s_, d = x.shape
    t = b_ * s_
    x2 = x.reshape(t, d)
    tab_c, tab_s, tab_w = _bias_tables(rel_bias, s_)
    ovl_t = _overlap_t(s_)
    e_sel = _sel_expand(s_)
    for l in range(norm1_w.shape[0]):
        p3, sm = _proj(x2, norm1_w[l][None, :], _arrange_w_in(w_in, l), tm=min(PROJ_TM, t), tn=PROJ_TN)
        conv_w3 = conv_w[l].reshape(GDN_CONV, 3 * GDN_HEADS, LANES)
        alog_b = jnp.pad(a_log[l], (0, LANES - GDN_HEADS))[None, :]
        dtb_b = jnp.pad(dt_bias[l], (0, LANES - GDN_HEADS))[None, :]
        y_a = _gdn(p3, sm, conv_w3, alog_b, dtb_b, gdn_norm_w[l][None, :], b_, s_)
        pe2 = cmp_pe[l].reshape(2, 2, CMP_STRIDE * NSA_DH)
        ckv = _cmp(p3, pe2, cmp_w1[l].astype(BF), cmp_w2[l].astype(BF), k_norm_w[l][0:1], b_, s_)
        y_b = _nsa(p3, sm, ckv, tab_c, tab_s, tab_w, ovl_t, e_sel, q_norm_w[l][None, :], k_norm_w[l], b_, s_)
        x2 = _merge(x2, y_a, y_b, p3, w_proj_a[l].astype(BF), w_proj_b[l].astype(BF), w_out[l].astype(BF),
                    tm=min(MERGE_TM, t))
        x2 = _ffn(x2, norm2_w[l][None, :], w_gate[l].astype(BF), w_up[l].astype(BF), w_down[l].astype(BF),
                  tm=min(FFN_TM, t))
    return x2.reshape(b_, s_, d)
```
